```python
import jax, jax.numpy as jnp
from jax import lax
import numpy as np

D_MODEL = 1024
BATCH = 8
SEQ = 8192
DEPTH = 4

N_MIXERS = 3
RMS_EPS = 1e-6
NEG_INF = -1e30
CONV_WIDTH = 3
MLA_HEADS = 8
MLA_Q_LORA = 384
MLA_KV_LORA = 256
MLA_NOPE = 128
MLA_ROPE = 64
MLA_V = 128
ROPE_THETA = 10000.0
Q_BLOCK = 128
SWA_Q_HEADS = 16
SWA_KV_HEADS = 2
SWA_GROUP = SWA_Q_HEADS // SWA_KV_HEADS
SWA_HEAD_DIM = 64
WINDOW = 128
D_FF = 2816
N_EXPERTS = 8
TOP_K = 2
D_FF_EXPERT = 3584
N_CONV_LAYERS = (DEPTH + 2) // 3
N_MLA_LAYERS = (DEPTH + 1) // 3
N_SWA_LAYERS = DEPTH // 3
N_DENSE_LAYERS = (DEPTH + 1) // 2
N_MOE_LAYERS = DEPTH // 2

kernel_name = 'hybrid_conv_mla_swa_moe_trunk'


def rmsnorm(x, g):
    xf = x.astype(jnp.float32)
    y = xf * lax.rsqrt(jnp.mean(xf * xf, axis=-1, keepdims=True) + RMS_EPS)
    return (y * g.astype(jnp.float32)).astype(x.dtype)


def short_conv(h, w_in, w_conv, w_out):
    S = h.shape[1]
    b_gate, c_gate, u = jnp.split(h @ w_in, 3, axis=-1)
    vp = jnp.pad(c_gate * u, ((0, 0), (CONV_WIDTH - 1, 0), (0, 0)))
    conv = vp[:, 0:S] * w_conv[0]
    for j in range(1, CONV_WIDTH):
        conv = conv + vp[:, j:j + S] * w_conv[j]
    return (b_gate * conv) @ w_out


def rope_angles(positions, dim):
    inv = ROPE_THETA ** (-jnp.arange(0, dim, 2, dtype=jnp.float32) / dim)
    ang = positions.astype(jnp.float32)[..., None] * inv
    return jnp.cos(ang), jnp.sin(ang)


def apply_rope(x, cos, sin):
    x1, x2 = jnp.split(x, 2, axis=-1)
    cos = cos.astype(x.dtype)
    sin = sin.astype(x.dtype)
    return jnp.concatenate([x1 * cos - x2 * sin, x2 * cos + x1 * sin], axis=-1)


def mla_attention(h, positions, w_down, q_norm, w_uq, kv_norm, w_ukv, w_o):
    B, S, _ = h.shape
    c_q, c_kv, k_pe = jnp.split(h @ w_down, [MLA_Q_LORA, MLA_Q_LORA + MLA_KV_LORA], axis=-1)
    q = (rmsnorm(c_q, q_norm) @ w_uq).reshape(B, S, MLA_HEADS, MLA_NOPE + MLA_ROPE)
    kv = (rmsnorm(c_kv, kv_norm) @ w_ukv).reshape(B, S, MLA_HEADS, MLA_NOPE + MLA_V)
    q_nope, q_pe = jnp.split(q, [MLA_NOPE], axis=-1)
    k_nope, v = jnp.split(kv, [MLA_NOPE], axis=-1)
    cos, sin = rope_angles(positions, MLA_ROPE)
    q_pe = apply_rope(q_pe, cos[:, :, None], sin[:, :, None])
    k_pe = apply_rope(k_pe, cos, sin)
    scale = (MLA_NOPE + MLA_ROPE) ** -0.5
    nb = S // Q_BLOCK

    def to_blocks(t):
        return t.reshape((B, nb, Q_BLOCK) + t.shape[2:]).swapaxes(0, 1)

    k_idx = jnp.arange(S)

    def block(args):
        qn, qp, start = args
        s = (jnp.einsum('bqhd,bkhd->bhqk', qn, k_nope)
             + jnp.einsum('bqhr,bkr->bhqk', qp, k_pe)).astype(jnp.float32) * scale
        q_idx = start + jnp.arange(Q_BLOCK)
        s = jnp.where(k_idx[None, :] <= q_idx[:, None], s, NEG_INF)
        p = jax.nn.softmax(s, axis=-1).astype(v.dtype)
        return jnp.einsum('bhqk,bkhd->bqhd', p, v)

    starts = jnp.arange(nb, dtype=jnp.int32) * Q_BLOCK
    o = lax.map(block, (to_blocks(q_nope), to_blocks(q_pe), starts))
    o = o.swapaxes(0, 1).reshape(B, S, MLA_HEADS * MLA_V)
    return o @ w_o


def alibi_slopes(n):
    return 2.0 ** (-8.0 * jnp.arange(1, n + 1, dtype=jnp.float32) / n)


def swa_attention(h, positions, w_qkv, b_qkv, sinks, w_o, b_o):
    B, S, _ = h.shape
    W = WINDOW
    nb = S // W
    qkv = h @ w_qkv + b_qkv
    nq = SWA_Q_HEADS * SWA_HEAD_DIM
    nk = SWA_KV_HEADS * SWA_HEAD_DIM
    q = qkv[..., :nq].reshape(B, nb, W, SWA_KV_HEADS, SWA_GROUP, SWA_HEAD_DIM)
    k = qkv[..., nq:nq + nk].reshape(B, S, SWA_KV_HEADS, SWA_HEAD_DIM)
    v = qkv[..., nq + nk:].reshape(B, S, SWA_KV_HEADS, SWA_HEAD_DIM)

    def band(t):
        tp = jnp.pad(t, ((0, 0), (W, 0)) + ((0, 0),) * (t.ndim - 2))
        tp = tp.reshape((B, nb + 1, W) + t.shape[2:])
        return jnp.concatenate([tp[:, :-1], tp[:, 1:]], axis=2)

    kb, vb = band(k), band(v)
    pk = band(positions)
    pq = positions.reshape(B, nb, W)
    s = jnp.einsum('bnqkgd,bnskd->bnkgqs', q, kb).astype(jnp.float32) * (SWA_HEAD_DIM ** -0.5)
    dist = jnp.abs(pq[:, :, :, None] - pk[:, :, None, :]).astype(jnp.float32)
    slopes = alibi_slopes(SWA_Q_HEADS).reshape(SWA_KV_HEADS, SWA_GROUP, 1, 1)
    s = s - slopes * dist[:, :, None, None]
    qi = W + jnp.arange(W)
    ki = jnp.arange(2 * W)
    diff = qi[:, None] - ki[None, :]
    in_band = (diff >= 0) & (diff < W)
    in_seq = (jnp.arange(nb)[:, None, None] * W - W + ki[None, None, :]) >= 0
    valid = in_band[None] & in_seq
    s = jnp.where(valid[None, :, None, None], s, NEG_INF)
    sink = jnp.broadcast_to(sinks.astype(jnp.float32).reshape(1, 1, SWA_KV_HEADS, SWA_GROUP, 1, 1),
                            s.shape[:-1] + (1,))
    p = jax.nn.softmax(jnp.concatenate([s, sink], axis=-1), axis=-1)[..., :-1].astype(vb.dtype)
    o = jnp.einsum('bnkgqs,bnskd->bnqkgd', p, vb).reshape(B, S, nq)
    return o @ w_o + b_o


def swiglu(h, w_gate_up, w_down):
    g, u = jnp.split(h @ w_gate_up, 2, axis=-1)
    return (jax.nn.silu(g) * u) @ w_down


def moe_swiglu(h, router, w_gate_up, w_down):
    logits = (h @ router).astype(jnp.float32)
    top_v, top_i = lax.top_k(logits, TOP_K)
    top_w = jax.nn.softmax(top_v, axis=-1)
    combine = jnp.sum(jax.nn.one_hot(top_i, N_EXPERTS, dtype=jnp.float32) * top_w[..., None], axis=-2)
    combine = combine.astype(h.dtype)
    out = combine[..., 0:1] * swiglu(h, w_gate_up[0], w_down[0])
    for e in range(1, N_EXPERTS):
        out = out + combine[..., e:e + 1] * swiglu(h, w_gate_up[e], w_down[e])
    return out


def setup_inputs(seed: int = 0) -> dict:
    key = jax.random.key(seed)
    ks = iter(jax.random.split(key, 32))

    def w(shape, fan_in):
        return jax.random.normal(next(ks), shape, jnp.float32) * (fan_in ** -0.5)

    def gain(shape):
        return 1.0 + 0.05 * jax.random.normal(next(ks), shape, jnp.float32)

    def small(shape, scale):
        return scale * jax.random.normal(next(ks), shape, jnp.float32)

    D = D_MODEL
    x = jax.random.normal(next(ks), (BATCH, SEQ, D), jnp.float32)
    offset = jax.random.randint(next(ks), (BATCH, 1), 0, 4096, dtype=jnp.int32)
    positions = offset + jnp.arange(SEQ, dtype=jnp.int32)[None, :]
    n_swa_qkv = (SWA_Q_HEADS + 2 * SWA_KV_HEADS) * SWA_HEAD_DIM
    return {
        'x': x,
        'positions': positions,
        'norm_mix': gain((DEPTH, D)),
        'norm_ffn': gain((DEPTH, D)),
        'norm_final': gain((D,)),
        'conv_w_in': w((N_CONV_LAYERS, D, 3 * D), D),
        'conv_w': w((N_CONV_LAYERS, CONV_WIDTH, D), CONV_WIDTH),
        'conv_w_out': w((N_CONV_LAYERS, D, D), D),
        'mla_w_down': w((N_MLA_LAYERS, D, MLA_Q_LORA + MLA_KV_LORA + MLA_ROPE), D),
        'mla_q_norm': gain((N_MLA_LAYERS, MLA_Q_LORA)),
        'mla_w_uq': w((N_MLA_LAYERS, MLA_Q_LORA, MLA_HEADS * (MLA_NOPE + MLA_ROPE)), MLA_Q_LORA),
        'mla_kv_norm': gain((N_MLA_LAYERS, MLA_KV_LORA)),
        'mla_w_ukv': w((N_MLA_LAYERS, MLA_KV_LORA, MLA_HEADS * (MLA_NOPE + MLA_V)), MLA_KV_LORA),
        'mla_w_o': w((N_MLA_LAYERS, MLA_HEADS * MLA_V, D), MLA_HEADS * MLA_V),
        'swa_w_qkv': w((N_SWA_LAYERS, D, n_swa_qkv), D),
        'swa_b_qkv': small((N_SWA_LAYERS, n_swa_qkv), 0.02),
        'swa_sinks': small((N_SWA_LAYERS, SWA_Q_HEADS), 0.5),
        'swa_w_o': w((N_SWA_LAYERS, SWA_Q_HEADS * SWA_HEAD_DIM, D), SWA_Q_HEADS * SWA_HEAD_DIM),
        'swa_b_o': small((N_SWA_LAYERS, D), 0.02),
        'ffn_w_gate_up': w((N_DENSE_LAYERS, D, 2 * D_FF), D),
        'ffn_w_down': w((N_DENSE_LAYERS, D_FF, D), D_FF),
        'moe_router': w((N_MOE_LAYERS, D, N_EXPERTS), D),
        'moe_w_gate_up': w((N_MOE_LAYERS, N_EXPERTS, D, 2 * D_FF_EXPERT), D),
        'moe_w_down': w((N_MOE_LAYERS, N_EXPERTS, D_FF_EXPERT, D), D_FF_EXPERT),
    }


def reference(x, positions, norm_mix, norm_ffn, norm_final,
              conv_w_in, conv_w, conv_w_out,
              mla_w_down, mla_q_norm, mla_w_uq, mla_kv_norm, mla_w_ukv, mla_w_o,
              swa_w_qkv, swa_b_qkv, swa_sinks, swa_w_o, swa_b_o,
              ffn_w_gate_up, ffn_w_down,
              moe_router, moe_w_gate_up, moe_w_down):
    for i in range(DEPTH):
        h = rmsnorm(x, norm_mix[i])
        kind = i % N_MIXERS
        j = i // N_MIXERS
        if kind == 0:
            m = short_conv(h, conv_w_in[j], conv_w[j], conv_w_out[j])
        elif kind == 1:
            m = mla_attention(h, positions, mla_w_down[j], mla_q_norm[j], mla_w_uq[j],
                              mla_kv_norm[j], mla_w_ukv[j], mla_w_o[j])
        else:
            m = swa_attention(h, positions, swa_w_qkv[j], swa_b_qkv[j], swa_sinks[j],
                              swa_w_o[j], swa_b_o[j])
        x = x + m
        h = rmsnorm(x, norm_ffn[i])
        f_idx = i // 2
        if i % 2 == 0:
            f = swiglu(h, ffn_w_gate_up[f_idx], ffn_w_down[f_idx])
        else:
            f = moe_swiglu(h, moe_router[f_idx], moe_w_gate_up[f_idx], moe_w_down[f_idx])
        x = x + f
    return rmsnorm(x, norm_final)
```

```python
import functools

import jax
import jax.numpy as jnp
from jax import lax
from jax.experimental import pallas as pl
from jax.experimental.pallas import tpu as pltpu

F32 = jnp.float32
BF16 = jnp.bfloat16

RMS_EPS = 1e-6
NEG_INF = -1e30
CONV_WIDTH = 3
MLA_HEADS = 8
MLA_Q_LORA = 384
MLA_KV_LORA = 256
MLA_NOPE = 128
MLA_ROPE = 64
MLA_V = 128
MLA_QK_PAD = 256
ROPE_THETA = 10000.0
SWA_Q_HEADS = 16
SWA_KV_HEADS = 2
SWA_GROUP = SWA_Q_HEADS // SWA_KV_HEADS
SWA_HEAD_DIM = 64
WINDOW = 128
N_EXPERTS = 8
TOP_K = 2
N_MIXERS = 3

VMEM_LIMIT = 56 * 1024 * 1024


def _cparams(sem):
    return pltpu.CompilerParams(dimension_semantics=sem, vmem_limit_bytes=VMEM_LIMIT)


def _rms(xf, g):
    ms = jnp.mean(xf * xf, axis=-1, keepdims=True)
    return xf * lax.rsqrt(ms + RMS_EPS) * g


def _const_spec(shape):
    nd = len(shape)
    return pl.BlockSpec(shape, lambda *_: (0,) * nd)


def _tile(n, pref):
    t = min(n, pref)
    assert n % t == 0, (n, t)
    return t


def _conv_body(x_ref, g_ref, win_ref, wc_ref, wout_ref, o_ref, vpad_ref, *, tm, d, tiles_per_seq):
    i = pl.program_id(0)
    x = x_ref[...]
    h = _rms(x, g_ref[...]).astype(BF16)
    bg = jnp.dot(h, win_ref[:, 0:d], preferred_element_type=F32)
    cg = jnp.dot(h, win_ref[:, d:2 * d], preferred_element_type=F32)
    u = jnp.dot(h, win_ref[:, 2 * d:3 * d], preferred_element_type=F32)
    v = cg * u

    @pl.when(i % tiles_per_seq == 0)
    def _():
        vpad_ref[0:8, :] = jnp.zeros((8, d), F32)

    vpad_ref[8:8 + tm, :] = v
    conv = vpad_ref[6:6 + tm, :] * wc_ref[0:1, :]
    conv = conv + vpad_ref[7:7 + tm, :] * wc_ref[1:2, :]
    conv = conv + v * wc_ref[2:3, :]
    vpad_ref[0:8, :] = vpad_ref[tm:tm + 8, :]
    y = (bg * conv).astype(BF16)
    o_ref[...] = x + jnp.dot(y, wout_ref[...], preferred_element_type=F32)


def conv_mixer(x, g, w_in, w_conv, w_out, seq):
    t, d = x.shape
    tm = _tile(seq, 256)
    body = functools.partial(_conv_body, tm=tm, d=d, tiles_per_seq=seq // tm)
    return pl.pallas_call(
        body,
        grid=(t // tm,),
        in_specs=[
            pl.BlockSpec((tm, d), lambda i: (i, 0)),
            _const_spec((1, d)),
            _const_spec((d, 3 * d)),
            _const_spec((CONV_WIDTH, d)),
            _const_spec((d, d)),
        ],
        out_specs=pl.BlockSpec((tm, d), lambda i: (i, 0)),
        out_shape=jax.ShapeDtypeStruct((t, d), F32),
        scratch_shapes=[pltpu.VMEM((tm + 8, d), F32)],
        compiler_params=_cparams(("arbitrary",)),
        name="conv_mixer",
    )(x, g.reshape(1, d), w_in.astype(BF16), w_conv, w_out.astype(BF16))


def _ffn_body(x_ref, g_ref, wg_ref, wu_ref, wd_ref, o_ref, h_ref, acc_ref, *, nchunks):
    j = pl.program_id(1)

    @pl.when(j == 0)
    def _():
        h_ref[...] = _rms(x_ref[...], g_ref[...]).astype(BF16)
        acc_ref[...] = jnp.zeros_like(acc_ref)

    h = h_ref[...]
    gate = jnp.dot(h, wg_ref[...], preferred_element_type=F32)
    up = jnp.dot(h, wu_ref[...], preferred_element_type=F32)
    a = (gate * jax.nn.sigmoid(gate) * up).astype(BF16)
    acc_ref[...] += jnp.dot(a, wd_ref[...], preferred_element_type=F32)

    @pl.when(j == nchunks - 1)
    def _():
        o_ref[...] = x_ref[...] + acc_ref[...]


def _ff_chunk(dff, pref):
    best = None
    for c in range(128, dff + 1, 128):
        if dff % c == 0 and c <= pref:
            best = c
    return best if best is not None else dff


def dense_ffn(x, g, w_gate_up, w_down):
    t, d = x.shape
    dff = w_down.shape[0]
    tm = _tile(t, 512)
    tf = _ff_chunk(dff, 1408)
    nchunks = dff // tf
    body = functools.partial(_ffn_body, nchunks=nchunks)
    wgu = w_gate_up.astype(BF16)
    return pl.pallas_call(
        body,
        grid=(t // tm, nchunks),
        in_specs=[
            pl.BlockSpec((tm, d), lambda i, j: (i, 0)),
            _const_spec((1, d)),
            pl.BlockSpec((d, tf), lambda i, j: (0, j)),
            pl.BlockSpec((d, tf), lambda i, j: (0, j + nchunks)),
            pl.BlockSpec((tf, d), lambda i, j: (j, 0)),
        ],
        out_specs=pl.BlockSpec((tm, d), lambda i, j: (i, 0)),
        out_shape=jax.ShapeDtypeStruct((t, d), F32),
        scratch_shapes=[pltpu.VMEM((tm, d), BF16), pltpu.VMEM((tm, d), F32)],
        compiler_params=_cparams(("arbitrary", "arbitrary")),
        name="dense_ffn",
    )(x, g.reshape(1, d), wgu, wgu, w_down.astype(BF16))


def _linres_body(x_ref, a_ref, w_ref, b_ref, o_ref):
    o_ref[...] = (x_ref[...] + b_ref[...]) + jnp.dot(a_ref[...], w_ref[...], preferred_element_type=F32)


def linear_residual(x, a, w, b):
    t, d = x.shape
    k = a.shape[1]
    tm = _tile(t, 1024)
    return pl.pallas_call(
        _linres_body,
        grid=(t // tm,),
        in_specs=[
            pl.BlockSpec((tm, d), lambda i: (i, 0)),
            pl.BlockSpec((tm, k), lambda i: (i, 0)),
            _const_spec((k, d)),
            _const_spec((1, d)),
        ],
        out_specs=pl.BlockSpec((tm, d), lambda i: (i, 0)),
        out_shape=jax.ShapeDtypeStruct((t, d), F32),
        compiler_params=_cparams(("arbitrary",)),
        name="linear_residual",
    )(x, a, w.astype(BF16), b.reshape(1, d).astype(F32))


def _rmslin_body(x_ref, g_ref, w_ref, b_ref, o_ref):
    h = _rms(x_ref[...], g_ref[...]).astype(BF16)
    o_ref[...] = (jnp.dot(h, w_ref[...], preferred_element_type=F32) + b_ref[...]).astype(o_ref.dtype)


def rms_linear(x, g, w, b):
    t, d = x.shape
    n = w.shape[1]
    tm = _tile(t, 512)
    return pl.pallas_call(
        _rmslin_body,
        grid=(t // tm,),
        in_specs=[
            pl.BlockSpec((tm, d), lambda i: (i, 0)),
            _const_spec((1, d)),
            _const_spec((d, n)),
            _const_spec((1, n)),
        ],
        out_specs=pl.BlockSpec((tm, n), lambda i: (i, 0)),
        out_shape=jax.ShapeDtypeStruct((t, n), BF16),
        compiler_params=_cparams(("arbitrary",)),
        name="rms_linear",
    )(x, g.reshape(1, d), w.astype(BF16), b.reshape(1, n).astype(F32))


def _mla_proj_body(x_ref, pos_ref, g_ref, wd_ref, qn_ref, kvn_ref, wqa_ref, wqb_ref, wkv_ref, frq_ref,
                   q_ref, k_ref, v_ref):
    h = _rms(x_ref[...], g_ref[...]).astype(BF16)
    down = jnp.dot(h, wd_ref[...], preferred_element_type=F32)
    c_q = down[:, 0:MLA_Q_LORA]
    c_kv = down[:, MLA_Q_LORA:MLA_Q_LORA + MLA_KV_LORA]
    o = MLA_Q_LORA + MLA_KV_LORA
    kpe = down[:, o:o + 128]
    kpe_sw = down[:, o + 128:o + 256]
    hq = _rms(c_q, qn_ref[...]).astype(BF16)
    hkv = _rms(c_kv, kvn_ref[...]).astype(BF16)
    qa = jnp.dot(hq, wqa_ref[...], preferred_element_type=F32)
    qb = jnp.dot(hq, wqb_ref[...], preferred_element_type=F32)
    kv = jnp.dot(hkv, wkv_ref[...], preferred_element_type=F32)
    ang = pos_ref[...] * frq_ref[0:1, :]
    cos = jnp.cos(ang)
    sin = jnp.sin(ang) * frq_ref[1:2, :]
    kpe_r = (kpe * cos + kpe_sw * sin).astype(BF16)
    for hd in range(MLA_HEADS):
        qo = hd * MLA_QK_PAD
        q_ref[hd, :, 0:MLA_NOPE] = qa[:, qo:qo + MLA_NOPE].astype(BF16)
        qpe = qa[:, qo + MLA_NOPE:qo + MLA_QK_PAD] * cos + qb[:, hd * 128:(hd + 1) * 128] * sin
        q_ref[hd, :, MLA_NOPE:MLA_QK_PAD] = qpe.astype(BF16)
        ko = hd * (MLA_NOPE + MLA_V)
        k_ref[hd, :, 0:MLA_NOPE] = kv[:, ko:ko + MLA_NOPE].astype(BF16)
        k_ref[hd, :, MLA_NOPE:MLA_QK_PAD] = kpe_r
        v_ref[hd, :, :] = kv[:, ko + MLA_NOPE:ko + MLA_NOPE + MLA_V].astype(BF16)


def _rope_swap(w):
    half = MLA_ROPE // 2
    return jnp.concatenate([w[..., half:], w[..., :half]], axis=-1)


def mla_project(x, pos_col, g, w_down, q_norm, w_uq, kv_norm, w_ukv):
    t, d = x.shape
    tm = _tile(t, 256)
    hn = MLA_HEADS
    z64 = jnp.zeros((d, 64), F32)
    o = MLA_Q_LORA + MLA_KV_LORA
    kpe_w = w_down[:, o:o + MLA_ROPE]
    wd = jnp.concatenate([w_down[:, :o], kpe_w, z64, _rope_swap(kpe_w), z64], axis=1).astype(BF16)
    wq = w_uq.reshape(MLA_Q_LORA, hn, MLA_NOPE + MLA_ROPE)
    zq = jnp.zeros((MLA_Q_LORA, hn, 64), F32)
    wqa = jnp.concatenate([wq, zq], axis=-1).reshape(MLA_Q_LORA, hn * MLA_QK_PAD).astype(BF16)
    wqb = jnp.concatenate([_rope_swap(wq[..., MLA_NOPE:]), zq], axis=-1).reshape(MLA_Q_LORA, hn * 128).astype(BF16)
    inv = ROPE_THETA ** (-jnp.arange(0, MLA_ROPE, 2, dtype=F32) / MLA_ROPE)
    half = MLA_ROPE // 2
    frq = jnp.stack([
        jnp.concatenate([inv, inv, jnp.zeros((64,), F32)]),
        jnp.concatenate([-jnp.ones((half,), F32), jnp.ones((half,), F32), jnp.zeros((64,), F32)]),
    ])
    nd = wd.shape[1]
    outs = pl.pallas_call(
        _mla_proj_body,
        grid=(t // tm,),
        in_specs=[
            pl.BlockSpec((tm, d), lambda i: (i, 0)),
            pl.BlockSpec((tm, 1), lambda i: (i, 0)),
            _const_spec((1, d)),
            _const_spec((d, nd)),
            _const_spec((1, MLA_Q_LORA)),
            _const_spec((1, MLA_KV_LORA)),
            _const_spec((MLA_Q_LORA, hn * MLA_QK_PAD)),
            _const_spec((MLA_Q_LORA, hn * 128)),
            _const_spec((MLA_KV_LORA, hn * (MLA_NOPE + MLA_V))),
            _const_spec((2, 128)),
        ],
        out_specs=[
            pl.BlockSpec((hn, tm, MLA_QK_PAD), lambda i: (0, i, 0)),
            pl.BlockSpec((hn, tm, MLA_QK_PAD), lambda i: (0, i, 0)),
            pl.BlockSpec((hn, tm, MLA_V), lambda i: (0, i, 0)),
        ],
        out_shape=[
            jax.ShapeDtypeStruct((hn, t, MLA_QK_PAD), BF16),
            jax.ShapeDtypeStruct((hn, t, MLA_QK_PAD), BF16),
            jax.ShapeDtypeStruct((hn, t, MLA_V), BF16),
        ],
        compiler_params=_cparams(("arbitrary",)),
        name="mla_project",
    )(x, pos_col, g.reshape(1, d), wd, q_norm.reshape(1, -1), kv_norm.reshape(1, -1), wqa, wqb,
      w_ukv.astype(BF16), frq)
    return outs


def _flash_body(qi_ref, kj_ref, q_ref, k_ref, v_ref, o_ref, m_ref, l_ref, acc_ref, *, tq, tk, scale):
    pair = pl.program_id(2)
    i = qi_ref[pair]
    j = kj_ref[pair]
    last_j = ((i + 1) * tq - 1) // tk

    @pl.when(j == 0)
    def _():
        m_ref[...] = jnp.full_like(m_ref, NEG_INF)
        l_ref[...] = jnp.zeros_like(l_ref)
        acc_ref[...] = jnp.zeros_like(acc_ref)

    def step(masked):
        s = lax.dot_general(q_ref[0], k_ref[0], (((1,), (1,)), ((), ())), preferred_element_type=F32) * scale
        if masked:
            qi = i * tq + lax.broadcasted_iota(jnp.int32, (tq, tk), 0)
            ki = j * tk + lax.broadcasted_iota(jnp.int32, (tq, tk), 1)
            s = jnp.where(ki <= qi, s, NEG_INF)
        m_prev = m_ref[...]
        m_new = jnp.maximum(m_prev, jnp.max(s, axis=-1, keepdims=True))
        alpha = jnp.exp(m_prev - m_new)
        p = jnp.exp(s - m_new)
        l_ref[...] = alpha * l_ref[...] + jnp.sum(p, axis=-1, keepdims=True)
        acc_ref[...] = alpha * acc_ref[...] + jnp.dot(p.astype(BF16), v_ref[0], preferred_element_type=F32)
        m_ref[...] = m_new

    needs_mask = (j + 1) * tk - 1 > i * tq

    @pl.when(jnp.logical_not(needs_mask))
    def _():
        step(False)

    @pl.when(needs_mask)
    def _():
        step(True)

    @pl.when(j == last_j)
    def _():
        o_ref[...] = (acc_ref[...] / l_ref[...]).astype(o_ref.dtype)


def mla_flash(q, k, v, batch, seq):
    hn, t, dq = q.shape
    dv = v.shape[2]
    tq = _tile(seq, 1024)
    tk = _tile(seq, 1024)
    nq, nk = seq // tq, seq // tk
    scale = float((MLA_NOPE + MLA_ROPE) ** -0.5)
    body = functools.partial(_flash_body, tq=tq, tk=tk, scale=scale)
    pairs = [(i, j) for i in range(nq) for j in range(((i + 1) * tq - 1) // tk + 1)]
    qi = jnp.asarray([p[0] for p in pairs], jnp.int32)
    kj = jnp.asarray([p[1] for p in pairs], jnp.int32)
    grid_spec = pltpu.PrefetchScalarGridSpec(
        num_scalar_prefetch=2,
        grid=(batch, hn, len(pairs)),
        in_specs=[
            pl.BlockSpec((1, tq, dq), lambda b, h, p, qi, kj: (h, b * nq + qi[p], 0)),
            pl.BlockSpec((1, tk, dq), lambda b, h, p, qi, kj: (h, b * nk + kj[p], 0)),
            pl.BlockSpec((1, tk, dv), lambda b, h, p, qi, kj: (h, b * nk + kj[p], 0)),
        ],
        out_specs=pl.BlockSpec((tq, dv), lambda b, h, p, qi, kj: (b * nq + qi[p], h)),
        scratch_shapes=[pltpu.VMEM((tq, 1), F32), pltpu.VMEM((tq, 1), F32), pltpu.VMEM((tq, dv), F32)],
    )
    return pl.pallas_call(
        body,
        grid_spec=grid_spec,
        out_shape=jax.ShapeDtypeStruct((t, hn * dv), BF16),
        compiler_params=_cparams(("arbitrary", "arbitrary", "arbitrary")),
        name="mla_flash",
    )(qi, kj, q, k, v)


def _alibi_slopes():
    return [2.0 ** (-8.0 * (h + 1) / SWA_Q_HEADS) for h in range(SWA_Q_HEADS)]


def _swa_body(sink_ref, q_ref, kc_ref, kp_ref, vc_ref, vp_ref, pq_ref, pkc_ref, pkp_ref, o_ref, *, blocks_per_seq):
    w = WINDOW
    n = pl.program_id(0) % blocks_per_seq
    pq = pq_ref[...]
    dist_c = jnp.abs(pq - pkc_ref[0])
    dist_p = jnp.abs(pq - pkp_ref[0])
    iq = lax.broadcasted_iota(jnp.int32, (w, w), 0)
    ik = lax.broadcasted_iota(jnp.int32, (w, w), 1)
    valid_c = ik <= iq
    valid_p = jnp.logical_and(ik > iq, n > 0)
    lane = lax.broadcasted_iota(jnp.int32, (w, 2 * SWA_HEAD_DIM), 1)
    left = lane < SWA_HEAD_DIM
    slopes = _alibi_slopes()
    pairs = SWA_GROUP // 2
    for kh in range(SWA_KV_HEADS):
        kc = kc_ref[:, kh * 128:(kh + 1) * 128]
        kp = kp_ref[:, kh * 128:(kh + 1) * 128]
        vc_l = vc_ref[:, kh * 256:kh * 256 + 128]
        vc_r = vc_ref[:, kh * 256 + 128:kh * 256 + 256]
        vp_l = vp_ref[:, kh * 256:kh * 256 + 128]
        vp_r = vp_ref[:, kh * 256 + 128:kh * 256 + 256]
        zero = jnp.zeros((w, 128), BF16)
        rows = []
        for half in range(2):
            for p in range(pairs):
                qp = q_ref[:, (kh * pairs + p) * 128:(kh * pairs + p + 1) * 128]
                rows.append(jnp.where(left if half == 0 else jnp.logical_not(left), qp, zero))
        qs = jnp.concatenate(rows, axis=0)
        dn = (((1,), (1,)), ((), ()))
        s_c = lax.dot_general(qs, kc, dn, preferred_element_type=F32)
        s_p = lax.dot_general(qs, kp, dn, preferred_element_type=F32)
        pc_rows, pp_rows = [], []
        for half in range(2):
            for p in range(pairs):
                hd = kh * SWA_GROUP + 2 * p + half
                r0 = (half * pairs + p) * w
                sink = sink_ref[hd]
                a_c = s_c[r0:r0 + w] * (SWA_HEAD_DIM ** -0.5) - slopes[hd] * dist_c
                a_p = s_p[r0:r0 + w] * (SWA_HEAD_DIM ** -0.5) - slopes[hd] * dist_p
                a_c = jnp.where(valid_c, a_c, NEG_INF)
                a_p = jnp.where(valid_p, a_p, NEG_INF)
                m = jnp.maximum(jnp.max(a_c, axis=-1, keepdims=True), jnp.max(a_p, axis=-1, keepdims=True))
                m = jnp.maximum(m, sink)
                e_c = jnp.exp(a_c - m)
                e_p = jnp.exp(a_p - m)
                den = jnp.sum(e_c, axis=-1, keepdims=True) + jnp.sum(e_p, axis=-1, keepdims=True) + jnp.exp(sink - m)
                pc_rows.append((e_c / den).astype(BF16))
                pp_rows.append((e_p / den).astype(BF16))
        hw = pairs * w
        pc = jnp.concatenate(pc_rows, axis=0)
        pp = jnp.concatenate(pp_rows, axis=0)
        o_pair = (jnp.dot(pc[0:hw], vc_l, preferred_element_type=F32)
                  + jnp.dot(pp[0:hw], vp_l, preferred_element_type=F32)
                  + jnp.dot(pc[hw:2 * hw], vc_r, preferred_element_type=F32)
                  + jnp.dot(pp[hw:2 * hw], vp_r, preferred_element_type=F32))
        for p in range(pairs):
            c0 = (kh * pairs + p) * 128
            o_ref[:, c0:c0 + 128] = o_pair[p * w:(p + 1) * w].astype(o_ref.dtype)


def swa_attention(qkv, sinks, pos_col, pos_row, batch, seq):
    t = qkv.shape[0]
    w = WINDOW
    nb = seq // w
    nq = SWA_Q_HEADS * SWA_HEAD_DIM
    body = functools.partial(_swa_body, blocks_per_seq=nb)

    def prev(i):
        return jnp.where(i % nb == 0, i, i - 1)

    kcol = (nq + 4 * 128) // 256
    return pl.pallas_call(
        body,
        grid=(t // w,),
        in_specs=[
            pl.BlockSpec(memory_space=pltpu.SMEM),
            pl.BlockSpec((w, nq), lambda i: (i, 0)),
            pl.BlockSpec((w, 256), lambda i: (i, kcol)),
            pl.BlockSpec((w, 256), lambda i: (prev(i), kcol)),
            pl.BlockSpec((w, 512), lambda i: (i, nq // 512)),
            pl.BlockSpec((w, 512), lambda i: (prev(i), nq // 512)),
            pl.BlockSpec((w, 1), lambda i: (i, 0)),
            pl.BlockSpec((1, 1, w), lambda i: (i, 0, 0)),
            pl.BlockSpec((1, 1, w), lambda i: (prev(i), 0, 0)),
        ],
        out_specs=pl.BlockSpec((w, nq), lambda i: (i, 0)),
        out_shape=jax.ShapeDtypeStruct((t, nq), BF16),
        compiler_params=_cparams(("arbitrary",)),
        name="swa_attention",
    )(sinks.astype(F32), qkv, qkv, qkv, qkv, qkv, pos_col, pos_row, pos_row)


def swa_qkv_weights(w_qkv, b_qkv):
    nq = SWA_Q_HEADS * SWA_HEAD_DIM
    nk = SWA_KV_HEADS * SWA_HEAD_DIM
    hd = SWA_HEAD_DIM

    def arrange(m):
        q = m[..., :nq]
        k = m[..., nq:nq + nk]
        v = m[..., nq + nk:]
        z = jnp.zeros(m.shape[:-1] + (hd,), m.dtype)
        parts = [q]
        for kh in range(SWA_KV_HEADS):
            vh = v[..., kh * hd:(kh + 1) * hd]
            parts += [vh, z, z, vh]
        for kh in range(SWA_KV_HEADS):
            kk = k[..., kh * hd:(kh + 1) * hd]
            parts += [kk, kk]
        return jnp.concatenate(parts, axis=-1)

    return arrange(w_qkv), arrange(b_qkv)


def _router_body(x_ref, g_ref, r_ref, eid_ref, rank_ref, wgt_ref, cnt_ref, run_ref, *, tm):
    i = pl.program_id(0)

    @pl.when(i == 0)
    def _():
        run_ref[...] = jnp.zeros_like(run_ref)

    h = _rms(x_ref[...], g_ref[...]).astype(BF16)
    logits = jnp.dot(h, r_ref[...], preferred_element_type=F32)
    lane = lax.broadcasted_iota(jnp.int32, logits.shape, 1)
    logits = jnp.where(lane < N_EXPERTS, logits, -jnp.inf)
    m1 = jnp.max(logits, axis=-1, keepdims=True)
    i1 = jnp.min(jnp.where(logits == m1, lane, 128), axis=-1, keepdims=True)
    rest = jnp.where(lane == i1, -jnp.inf, logits)
    m2 = jnp.max(rest, axis=-1, keepdims=True)
    i2 = jnp.min(jnp.where(rest == m2, lane, 128), axis=-1, keepdims=True)
    e2 = jnp.exp(m2 - m1)
    w1 = 1.0 / (1.0 + e2)
    w2 = e2 / (1.0 + e2)
    oh1 = (lane == i1)
    oh2 = (lane == i2)
    sel = jnp.where(jnp.logical_or(oh1, oh2), 1.0, 0.0).astype(BF16)
    r = lax.broadcasted_iota(jnp.int32, (tm, tm), 0)
    c = lax.broadcasted_iota(jnp.int32, (tm, tm), 1)
    tri = jnp.where(c < r, 1.0, 0.0).astype(BF16)
    before = jnp.dot(tri, sel, preferred_element_type=F32) + run_ref[...]
    rank1 = jnp.sum(jnp.where(oh1, before, 0.0), axis=-1, keepdims=True)
    rank2 = jnp.sum(jnp.where(oh2, before, 0.0), axis=-1, keepdims=True)
    run_ref[...] = run_ref[...] + jnp.sum(sel.astype(F32), axis=0, keepdims=True)
    eid_ref[...] = jnp.where(lane == 0, i1, jnp.where(lane == 1, i2, 0))[:, 0:TOP_K]
    rank_ref[...] = jnp.where(lane == 0, rank1, jnp.where(lane == 1, rank2, 0.0))[:, 0:TOP_K].astype(jnp.int32)
    wgt_ref[...] = jnp.where(lane == 0, w1, jnp.where(lane == 1, w2, 0.0))[:, 0:TOP_K]
    cnt_ref[...] = run_ref[...].astype(jnp.int32)


def moe_route(x, g, router):
    t, d = x.shape
    tm = _tile(t, 512)
    rpad = jnp.zeros((d, 128), F32).at[:, :N_EXPERTS].set(router).astype(BF16)
    body = functools.partial(_router_body, tm=tm)
    return pl.pallas_call(
        body,
        grid=(t // tm,),
        in_specs=[
            pl.BlockSpec((tm, d), lambda i: (i, 0)),
            _const_spec((1, d)),
            _const_spec((d, 128)),
        ],
        out_specs=[
            pl.BlockSpec((tm, TOP_K), lambda i: (i, 0)),
            pl.BlockSpec((tm, TOP_K), lambda i: (i, 0)),
            pl.BlockSpec((tm, TOP_K), lambda i: (i, 0)),
            _const_spec((1, 128)),
        ],
        out_shape=[
            jax.ShapeDtypeStruct((t, TOP_K), jnp.int32),
            jax.ShapeDtypeStruct((t, TOP_K), jnp.int32),
            jax.ShapeDtypeStruct((t, TOP_K), F32),
            jax.ShapeDtypeStruct((1, 128), jnp.int32),
        ],
        scratch_shapes=[pltpu.VMEM((1, 128), F32)],
        compiler_params=_cparams(("arbitrary",)),
        name="moe_route",
    )(x, g.reshape(1, d), rpad)


def _dispatch_body(dest_ref, x_ref, xs_in_ref, xs_ref, sem, *, tm):
    del xs_in_ref
    i = pl.program_id(0)

    def row_copy(src, dst):
        return pltpu.make_async_copy(x_ref.at[pl.ds(src, 1)], xs_ref.at[pl.ds(dst, 1)], sem)

    def issue(r, carry):
        for kk in range(TOP_K):
            row_copy(i * tm + r, dest_ref[(i * tm + r) * TOP_K + kk]).start()
        return carry

    lax.fori_loop(0, tm, issue, 0)

    def drain(r, carry):
        for kk in range(TOP_K):
            row_copy(0, 0).wait()
        return carry

    lax.fori_loop(0, tm, drain, 0)


def moe_dispatch(x, dest_flat, xs_init):
    t = x.shape[0]
    tm = _tile(t, 512)
    body = functools.partial(_dispatch_body, tm=tm)
    grid_spec = pltpu.PrefetchScalarGridSpec(
        num_scalar_prefetch=1,
        grid=(t // tm,),
        in_specs=[
            pl.BlockSpec(memory_space=pl.ANY),
            pl.BlockSpec(memory_space=pl.ANY),
        ],
        out_specs=pl.BlockSpec(memory_space=pl.ANY),
        scratch_shapes=[pltpu.SemaphoreType.DMA(())],
    )
    return pl.pallas_call(
        body,
        grid_spec=grid_spec,
        out_shape=jax.ShapeDtypeStruct(xs_init.shape, xs_init.dtype),
        input_output_aliases={2: 0},
        compiler_params=_cparams(("arbitrary",)),
        name="moe_dispatch",
    )(dest_flat, x, xs_init)


def _expert_body(te_ref, nt_ref, xs_ref, g_ref, wg_ref, wu_ref, wd_ref, ys_ref, h_ref, acc_ref, *, nchunks):
    i = pl.program_id(0)
    j = pl.program_id(1)

    @pl.when(i < nt_ref[0])
    def _():
        @pl.when(j == 0)
        def _():
            h_ref[...] = _rms(xs_ref[...], g_ref[...]).astype(BF16)
            acc_ref[...] = jnp.zeros_like(acc_ref)

        h = h_ref[...]
        gate = jnp.dot(h, wg_ref[0], preferred_element_type=F32)
        up = jnp.dot(h, wu_ref[0], preferred_element_type=F32)
        a = (gate * jax.nn.sigmoid(gate) * up).astype(BF16)
        acc_ref[...] += jnp.dot(a, wd_ref[0], preferred_element_type=F32)

        @pl.when(j == nchunks - 1)
        def _():
            ys_ref[...] = acc_ref[...]

    @pl.when(jnp.logical_and(i >= nt_ref[0], j == nchunks - 1))
    def _():
        ys_ref[...] = jnp.zeros_like(ys_ref)


def moe_experts(xs, g, tile_expert, num_tiles, w_gate_up, w_down, tr):
    p, d = xs.shape
    ne, dff = w_down.shape[0], w_down.shape[1]
    tf = _ff_chunk(dff, 896)
    nchunks = dff // tf
    body = functools.partial(_expert_body, nchunks=nchunks)

    def row(i, nt):
        return jnp.minimum(i, nt[0] - 1)

    def chunk(i, j, nt):
        return jnp.where(i < nt[0], j, nchunks - 1)

    grid_spec = pltpu.PrefetchScalarGridSpec(
        num_scalar_prefetch=2,
        grid=(p // tr, nchunks),
        in_specs=[
            pl.BlockSpec((tr, d), lambda i, j, te, nt: (row(i, nt), 0)),
            pl.BlockSpec((1, d), lambda i, j, te, nt: (0, 0)),
            pl.BlockSpec((1, d, tf), lambda i, j, te, nt: (te[row(i, nt)], 0, chunk(i, j, nt))),
            pl.BlockSpec((1, d, tf), lambda i, j, te, nt: (te[row(i, nt)], 0, chunk(i, j, nt) + nchunks)),
            pl.BlockSpec((1, tf, d), lambda i, j, te, nt: (te[row(i, nt)], chunk(i, j, nt), 0)),
        ],
        out_specs=pl.BlockSpec((tr, d), lambda i, j, te, nt: (i, 0)),
        scratch_shapes=[pltpu.VMEM((tr, d), BF16), pltpu.VMEM((tr, d), F32)],
    )
    wgu = w_gate_up.astype(BF16)
    return pl.pallas_call(
        body,
        grid_spec=grid_spec,
        out_shape=jax.ShapeDtypeStruct((p, d), F32),
        compiler_params=_cparams(("arbitrary", "arbitrary")),
        name="moe_experts",
    )(tile_expert, num_tiles, xs, g.reshape(1, d), wgu, wgu, w_down.astype(BF16))


def _combine_body(dest_ref, x_ref, w_ref, gf_ref, ys_ref, o_ref, buf_ref, sem, *, tm, final_norm):
    i = pl.program_id(0)

    def row_copy(src, kk, r):
        return pltpu.make_async_copy(ys_ref.at[pl.ds(src, 1)], buf_ref.at[kk, pl.ds(r, 1)], sem)

    def issue(r, carry):
        for kk in range(TOP_K):
            row_copy(dest_ref[(i * tm + r) * TOP_K + kk], kk, r).start()
        return carry

    lax.fori_loop(0, tm, issue, 0)

    def drain(r, carry):
        for kk in range(TOP_K):
            row_copy(0, kk, r).wait()
        return carry

    lax.fori_loop(0, tm, drain, 0)
    w = w_ref[...]
    out = x_ref[...] + (w[:, 0:1] * buf_ref[0] + w[:, 1:2] * buf_ref[1])
    if final_norm:
        out = _rms(out, gf_ref[...])
    o_ref[...] = out


def moe_combine(x, wgt, dest_flat, ys, g_final, final_norm):
    t, d = x.shape
    tm = _tile(t, 512)
    body = functools.partial(_combine_body, tm=tm, final_norm=final_norm)
    grid_spec = pltpu.PrefetchScalarGridSpec(
        num_scalar_prefetch=1,
        grid=(t // tm,),
        in_specs=[
            pl.BlockSpec((tm, d), lambda i, dd: (i, 0)),
            pl.BlockSpec((tm, TOP_K), lambda i, dd: (i, 0)),
            pl.BlockSpec((1, d), lambda i, dd: (0, 0)),
            pl.BlockSpec(memory_space=pl.ANY),
        ],
        out_specs=pl.BlockSpec((tm, d), lambda i, dd: (i, 0)),
        scratch_shapes=[pltpu.VMEM((TOP_K, tm, d), F32), pltpu.SemaphoreType.DMA(())],
    )
    return pl.pallas_call(
        body,
        grid_spec=grid_spec,
        out_shape=jax.ShapeDtypeStruct((t, d), F32),
        compiler_params=_cparams(("arbitrary",)),
        name="moe_combine",
    )(dest_flat, x, wgt, g_final.reshape(1, d), ys)


def moe_layer(x, g, router, w_gate_up, w_down, g_final, final_norm):
    t, d = x.shape
    tr = min(1024, t)
    eid, rank, wgt, cnt = moe_route(x, g, router)
    counts = cnt[0, :N_EXPERTS]
    padded = ((counts + tr - 1) // tr) * tr
    ends = jnp.cumsum(padded)
    base = ends - padded
    dest = (base[eid] + rank).reshape(-1).astype(jnp.int32)
    ntiles_max = (t * TOP_K) // tr + N_EXPERTS
    num_tiles = (ends[-1] // tr).astype(jnp.int32).reshape(1)
    tile_start = jnp.arange(ntiles_max, dtype=jnp.int32) * tr
    tile_expert = jnp.minimum(jnp.sum(tile_start[:, None] >= ends[None, :], axis=1), N_EXPERTS - 1).astype(jnp.int32)
    p = ntiles_max * tr
    xs = moe_dispatch(x, dest, jnp.zeros((p, d), F32))
    ys = moe_experts(xs, g, tile_expert, num_tiles, w_gate_up, w_down, tr)
    return moe_combine(x, wgt, dest, ys, g_final, final_norm)


def kernel(x, positions, norm_mix, norm_ffn, norm_final, conv_w_in, conv_w, conv_w_out, mla_w_down, mla_q_norm,
           mla_w_uq, mla_kv_norm, mla_w_ukv, mla_w_o, swa_w_qkv, swa_b_qkv, swa_sinks, swa_w_o, swa_b_o,
           ffn_w_gate_up, ffn_w_down, moe_router, moe_w_gate_up, moe_w_down):
    batch, seq, d = x.shape
    depth = norm_mix.shape[0]
    t = batch * seq
    xf = x.reshape(t, d)
    pos_col = positions.reshape(t, 1).astype(F32)
    pos_row = positions.reshape(t // WINDOW, 1, WINDOW).astype(F32)
    zeros_d = jnp.zeros((d,), F32)
    for i in range(depth):
        kind = i % N_MIXERS
        j = i // N_MIXERS
        if kind == 0:
            xf = conv_mixer(xf, norm_mix[i], conv_w_in[j], conv_w[j], conv_w_out[j], seq)
        elif kind == 1:
            q, k, v = mla_project(xf, pos_col, norm_mix[i], mla_w_down[j], mla_q_norm[j], mla_w_uq[j],
                                  mla_kv_norm[j], mla_w_ukv[j])
            o = mla_flash(q, k, v, batch, seq)
            xf = linear_residual(xf, o, mla_w_o[j], zeros_d)
        else:
            wq, bq = swa_qkv_weights(swa_w_qkv[j], swa_b_qkv[j])
            qkv = rms_linear(xf, norm_mix[i], wq, bq)
            o = swa_attention(qkv, swa_sinks[j], pos_col, pos_row, batch, seq)
            xf = linear_residual(xf, o, swa_w_o[j], swa_b_o[j])
        f_idx = i // 2
        last = i == depth - 1
        if i % 2 == 0:
            xf = dense_ffn(xf, norm_ffn[i], ffn_w_gate_up[f_idx], ffn_w_down[f_idx])
            if last:
                xf = final_rmsnorm(xf, norm_final)
        else:
            xf = moe_layer(xf, norm_ffn[i], moe_router[f_idx], moe_w_gate_up[f_idx], moe_w_down[f_idx],
                           norm_final, last)
    return xf.reshape(batch, seq, d)


def _final_norm_body(x_ref, g_ref, o_ref):
    o_ref[...] = _rms(x_ref[...], g_ref[...])


def final_rmsnorm(x, g):
    t, d = x.shape
    tm = _tile(t, 1024)
    return pl.pallas_call(
        _final_norm_body,
        grid=(t // tm,),
        in_specs=[pl.BlockSpec((tm, d), lambda i: (i, 0)), _const_spec((1, d))],
        out_specs=pl.BlockSpec((tm, d), lambda i: (i, 0)),
        out_shape=jax.ShapeDtypeStruct((t, d), F32),
        compiler_params=_cparams(("arbitrary",)),
        name="final_rmsnorm",
    )(x, g.reshape(1, d))
```

```python
import functools

import jax
import jax.numpy as jnp
from jax import lax
from jax.experimental import pallas as pl
from jax.experimental.pallas import tpu as pltpu

F32 = jnp.float32
BF16 = jnp.bfloat16

RMS_EPS = 1e-6
NEG_INF = -1e30
CONV_WIDTH = 3
MLA_HEADS = 8
MLA_Q_LORA = 384
MLA_KV_LORA = 256
MLA_NOPE = 128
MLA_ROPE = 64
MLA_V = 128
MLA_QK_PAD = 256
MLA_Q_SCALE = float((MLA_NOPE + MLA_ROPE) ** -0.5 * 1.4426950408889634)
ROPE_THETA = 10000.0
SWA_Q_HEADS = 16
SWA_KV_HEADS = 2
SWA_GROUP = SWA_Q_HEADS // SWA_KV_HEADS
SWA_HEAD_DIM = 64
WINDOW = 128
N_EXPERTS = 8
TOP_K = 2
N_MIXERS = 3

VMEM_LIMIT = 56 * 1024 * 1024


def _cparams(sem):
    return pltpu.CompilerParams(dimension_semantics=sem, vmem_limit_bytes=VMEM_LIMIT)


def _rms(xf, g):
    ms = jnp.mean(xf * xf, axis=-1, keepdims=True)
    return xf * lax.rsqrt(ms + RMS_EPS) * g


def _const_spec(shape):
    nd = len(shape)
    return pl.BlockSpec(shape, lambda *_: (0,) * nd)


def _tile(n, pref):
    t = min(n, pref)
    assert n % t == 0, (n, t)
    return t


def _conv_body(x_ref, g_ref, win_ref, wc_ref, wout_ref, o_ref, vpad_ref, *, tm, d, tiles_per_seq):
    i = pl.program_id(0)
    x = x_ref[...]
    h = _rms(x, g_ref[...]).astype(BF16)
    bg = jnp.dot(h, win_ref[:, 0:d], preferred_element_type=F32)
    cg = jnp.dot(h, win_ref[:, d:2 * d], preferred_element_type=F32)
    u = jnp.dot(h, win_ref[:, 2 * d:3 * d], preferred_element_type=F32)
    v = cg * u

    @pl.when(i % tiles_per_seq == 0)
    def _():
        vpad_ref[0:8, :] = jnp.zeros((8, d), F32)

    vpad_ref[8:8 + tm, :] = v
    conv = vpad_ref[6:6 + tm, :] * wc_ref[0:1, :]
    conv = conv + vpad_ref[7:7 + tm, :] * wc_ref[1:2, :]
    conv = conv + v * wc_ref[2:3, :]
    vpad_ref[0:8, :] = vpad_ref[tm:tm + 8, :]
    y = (bg * conv).astype(BF16)
    o_ref[...] = x + jnp.dot(y, wout_ref[...], preferred_element_type=F32)


def conv_mixer(x, g, w_in, w_conv, w_out, seq):
    t, d = x.shape
    tm = _tile(seq, 256)
    body = functools.partial(_conv_body, tm=tm, d=d, tiles_per_seq=seq // tm)
    return pl.pallas_call(
        body,
        grid=(t // tm,),
        in_specs=[
            pl.BlockSpec((tm, d), lambda i: (i, 0)),
            _const_spec((1, d)),
            _const_spec((d, 3 * d)),
            _const_spec((CONV_WIDTH, d)),
            _const_spec((d, d)),
        ],
        out_specs=pl.BlockSpec((tm, d), lambda i: (i, 0)),
        out_shape=jax.ShapeDtypeStruct((t, d), F32),
        scratch_shapes=[pltpu.VMEM((tm + 8, d), F32)],
        compiler_params=_cparams(("arbitrary",)),
        name="conv_mixer",
    )(x, g.reshape(1, d), w_in.astype(BF16), w_conv, w_out.astype(BF16))


def _ffn_body(x_ref, g_ref, wg_ref, wu_ref, wd_ref, o_ref, h_ref, acc_ref, *, nchunks):
    j = pl.program_id(1)

    @pl.when(j == 0)
    def _():
        h_ref[...] = _rms(x_ref[...], g_ref[...]).astype(BF16)
        acc_ref[...] = jnp.zeros_like(acc_ref)

    h = h_ref[...]
    gate = jnp.dot(h, wg_ref[...], preferred_element_type=F32)
    up = jnp.dot(h, wu_ref[...], preferred_element_type=F32)
    a = (gate * jax.nn.sigmoid(gate) * up).astype(BF16)
    acc_ref[...] += jnp.dot(a, wd_ref[...], preferred_element_type=F32)

    @pl.when(j == nchunks - 1)
    def _():
        o_ref[...] = x_ref[...] + acc_ref[...]


def _ff_chunk(dff, pref):
    best = None
    for c in range(128, dff + 1, 128):
        if dff % c == 0 and c <= pref:
            best = c
    return best if best is not None else dff


def dense_ffn(x, g, w_gate_up, w_down):
    t, d = x.shape
    dff = w_down.shape[0]
    tm = _tile(t, 512)
    tf = _ff_chunk(dff, 1408)
    nchunks = dff // tf
    body = functools.partial(_ffn_body, nchunks=nchunks)
    wgu = w_gate_up.astype(BF16)
    return pl.pallas_call(
        body,
        grid=(t // tm, nchunks),
        in_specs=[
            pl.BlockSpec((tm, d), lambda i, j: (i, 0)),
            _const_spec((1, d)),
            pl.BlockSpec((d, tf), lambda i, j: (0, j)),
            pl.BlockSpec((d, tf), lambda i, j: (0, j + nchunks)),
            pl.BlockSpec((tf, d), lambda i, j: (j, 0)),
        ],
        out_specs=pl.BlockSpec((tm, d), lambda i, j: (i, 0)),
        out_shape=jax.ShapeDtypeStruct((t, d), F32),
        scratch_shapes=[pltpu.VMEM((tm, d), BF16), pltpu.VMEM((tm, d), F32)],
        compiler_params=_cparams(("arbitrary", "arbitrary")),
        name="dense_ffn",
    )(x, g.reshape(1, d), wgu, wgu, w_down.astype(BF16))


def _linres_body(x_ref, a_ref, w_ref, b_ref, o_ref):
    o_ref[...] = (x_ref[...] + b_ref[...]) + jnp.dot(a_ref[...], w_ref[...], preferred_element_type=F32)


def linear_residual(x, a, w, b):
    t, d = x.shape
    k = a.shape[1]
    tm = _tile(t, 1024)
    return pl.pallas_call(
        _linres_body,
        grid=(t // tm,),
        in_specs=[
            pl.BlockSpec((tm, d), lambda i: (i, 0)),
            pl.BlockSpec((tm, k), lambda i: (i, 0)),
            _const_spec((k, d)),
            _const_spec((1, d)),
        ],
        out_specs=pl.BlockSpec((tm, d), lambda i: (i, 0)),
        out_shape=jax.ShapeDtypeStruct((t, d), F32),
        compiler_params=_cparams(("arbitrary",)),
        name="linear_residual",
    )(x, a, w.astype(BF16), b.reshape(1, d).astype(F32))


def _rmslin_body(x_ref, g_ref, w_ref, b_ref, o_ref):
    h = _rms(x_ref[...], g_ref[...]).astype(BF16)
    o_ref[...] = (jnp.dot(h, w_ref[...], preferred_element_type=F32) + b_ref[...]).astype(o_ref.dtype)


def rms_linear(x, g, w, b):
    t, d = x.shape
    n = w.shape[1]
    tm = _tile(t, 512)
    return pl.pallas_call(
        _rmslin_body,
        grid=(t // tm,),
        in_specs=[
            pl.BlockSpec((tm, d), lambda i: (i, 0)),
            _const_spec((1, d)),
            _const_spec((d, n)),
            _const_spec((1, n)),
        ],
        out_specs=pl.BlockSpec((tm, n), lambda i: (i, 0)),
        out_shape=jax.ShapeDtypeStruct((t, n), BF16),
        compiler_params=_cparams(("arbitrary",)),
        name="rms_linear",
    )(x, g.reshape(1, d), w.astype(BF16), b.reshape(1, n).astype(F32))


def _mla_proj_body(x_ref, pos_ref, g_ref, wd_ref, qn_ref, kvn_ref, wqa_ref, wqb_ref, wkv_ref, frq_ref,
                   q_ref, k_ref, v_ref):
    h = _rms(x_ref[...], g_ref[...]).astype(BF16)
    down = jnp.dot(h, wd_ref[...], preferred_element_type=F32)
    c_q = down[:, 0:MLA_Q_LORA]
    c_kv = down[:, MLA_Q_LORA:MLA_Q_LORA + MLA_KV_LORA]
    o = MLA_Q_LORA + MLA_KV_LORA
    kpe = down[:, o:o + 128]
    kpe_sw = down[:, o + 128:o + 256]
    hq = _rms(c_q, qn_ref[...]).astype(BF16)
    hkv = _rms(c_kv, kvn_ref[...]).astype(BF16)
    qa = jnp.dot(hq, wqa_ref[...], preferred_element_type=F32)
    qb = jnp.dot(hq, wqb_ref[...], preferred_element_type=F32)
    kv = jnp.dot(hkv, wkv_ref[...], preferred_element_type=F32)
    ang = pos_ref[...] * frq_ref[0:1, :]
    cos = jnp.cos(ang)
    sin = jnp.sin(ang) * frq_ref[1:2, :]
    kpe_r = (kpe * cos + kpe_sw * sin).astype(BF16)
    for hd in range(MLA_HEADS):
        qo = hd * MLA_QK_PAD
        q_ref[hd, :, 0:MLA_NOPE] = (qa[:, qo:qo + MLA_NOPE] * MLA_Q_SCALE).astype(BF16)
        qpe = qa[:, qo + MLA_NOPE:qo + MLA_QK_PAD] * cos + qb[:, hd * 128:(hd + 1) * 128] * sin
        q_ref[hd, :, MLA_NOPE:MLA_QK_PAD] = (qpe * MLA_Q_SCALE).astype(BF16)
        ko = hd * (MLA_NOPE + MLA_V)
        k_ref[hd, :, 0:MLA_NOPE] = kv[:, ko:ko + MLA_NOPE].astype(BF16)
        k_ref[hd, :, MLA_NOPE:MLA_QK_PAD] = kpe_r
        v_ref[hd, :, :] = kv[:, ko + MLA_NOPE:ko + MLA_NOPE + MLA_V].astype(BF16)


def _rope_swap(w):
    half = MLA_ROPE // 2
    return jnp.concatenate([w[..., half:], w[..., :half]], axis=-1)


def mla_project(x, pos_col, g, w_down, q_norm, w_uq, kv_norm, w_ukv):
    t, d = x.shape
    tm = _tile(t, 256)
    hn = MLA_HEADS
    z64 = jnp.zeros((d, 64), F32)
    o = MLA_Q_LORA + MLA_KV_LORA
    kpe_w = w_down[:, o:o + MLA_ROPE]
    wd = jnp.concatenate([w_down[:, :o], kpe_w, z64, _rope_swap(kpe_w), z64], axis=1).astype(BF16)
    wq = w_uq.reshape(MLA_Q_LORA, hn, MLA_NOPE + MLA_ROPE)
    zq = jnp.zeros((MLA_Q_LORA, hn, 64), F32)
    wqa = jnp.concatenate([wq, zq], axis=-1).reshape(MLA_Q_LORA, hn * MLA_QK_PAD).astype(BF16)
    wqb = jnp.concatenate([_rope_swap(wq[..., MLA_NOPE:]), zq], axis=-1).reshape(MLA_Q_LORA, hn * 128).astype(BF16)
    inv = ROPE_THETA ** (-jnp.arange(0, MLA_ROPE, 2, dtype=F32) / MLA_ROPE)
    half = MLA_ROPE // 2
    frq = jnp.stack([
        jnp.concatenate([inv, inv, jnp.zeros((64,), F32)]),
        jnp.concatenate([-jnp.ones((half,), F32), jnp.ones((half,), F32), jnp.zeros((64,), F32)]),
    ])
    nd = wd.shape[1]
    outs = pl.pallas_call(
        _mla_proj_body,
        grid=(t // tm,),
        in_specs=[
            pl.BlockSpec((tm, d), lambda i: (i, 0)),
            pl.BlockSpec((tm, 1), lambda i: (i, 0)),
            _const_spec((1, d)),
            _const_spec((d, nd)),
            _const_spec((1, MLA_Q_LORA)),
            _const_spec((1, MLA_KV_LORA)),
            _const_spec((MLA_Q_LORA, hn * MLA_QK_PAD)),
            _const_spec((MLA_Q_LORA, hn * 128)),
            _const_spec((MLA_KV_LORA, hn * (MLA_NOPE + MLA_V))),
            _const_spec((2, 128)),
        ],
        out_specs=[
            pl.BlockSpec((hn, tm, MLA_QK_PAD), lambda i: (0, i, 0)),
            pl.BlockSpec((hn, tm, MLA_QK_PAD), lambda i: (0, i, 0)),
            pl.BlockSpec((hn, tm, MLA_V), lambda i: (0, i, 0)),
        ],
        out_shape=[
            jax.ShapeDtypeStruct((hn, t, MLA_QK_PAD), BF16),
            jax.ShapeDtypeStruct((hn, t, MLA_QK_PAD), BF16),
            jax.ShapeDtypeStruct((hn, t, MLA_V), BF16),
        ],
        compiler_params=_cparams(("arbitrary",)),
        name="mla_project",
    )(x, pos_col, g.reshape(1, d), wd, q_norm.reshape(1, -1), kv_norm.reshape(1, -1), wqa, wqb,
      w_ukv.astype(BF16), frq)
    return outs


def _flash_body(qi_ref, kj_ref, q_ref, k_ref, v_ref, o_ref, m_ref, l_ref, acc_ref, *, tq, tqs, hp, dv):
    pair = pl.program_id(2)
    i = qi_ref[pair]
    j = kj_ref[pair]

    @pl.when(j == 0)
    def _():
        m_ref[...] = jnp.full_like(m_ref, NEG_INF)
        l_ref[...] = jnp.zeros_like(l_ref)
        acc_ref[...] = jnp.zeros_like(acc_ref)

    def step(diagonal):
        nsub = tq // tqs
        units = [(h, r) for h in range(hp) for r in range(nsub)]

        def scores(u):
            h, r = u
            nk = (r + 1) * tqs if diagonal else tq
            return lax.dot_general(q_ref[h, r * tqs:(r + 1) * tqs, :], k_ref[h, 0:nk, :], (((1,), (1,)), ((), ())),
                                   preferred_element_type=F32)

        s_next = scores(units[0])
        for n, (h, r) in enumerate(units):
            rows = slice(r * tqs, (r + 1) * tqs)
            nk = (r + 1) * tqs if diagonal else tq
            s = s_next
            if n + 1 < len(units):
                s_next = scores(units[n + 1])
            if diagonal:
                qi = r * tqs + lax.broadcasted_iota(jnp.int32, (tqs, nk), 0)
                ki = lax.broadcasted_iota(jnp.int32, (tqs, nk), 1)
                s = jnp.where(ki <= qi, s, NEG_INF)
            m_prev = m_ref[h, rows, :]
            m_new = jnp.maximum(m_prev, jnp.max(s, axis=-1, keepdims=True))
            alpha = jnp.exp2(m_prev - m_new)
            p = jnp.exp2(s - m_new)
            l_ref[h, rows, :] = alpha * l_ref[h, rows, :] + jnp.sum(p, axis=-1, keepdims=True)
            pv = jnp.dot(p.astype(BF16), v_ref[h, 0:nk, :], preferred_element_type=F32)
            acc_ref[h, rows, :] = alpha * acc_ref[h, rows, :] + pv
            m_ref[h, rows, :] = m_new

    @pl.when(j < i)
    def _():
        step(False)

    @pl.when(j == i)
    def _():
        step(True)
        for h in range(hp):
            o_ref[:, h * dv:(h + 1) * dv] = (acc_ref[h] / l_ref[h]).astype(o_ref.dtype)


def mla_flash(q, k, v, batch, seq):
    hn, t, dq = q.shape
    dv = v.shape[2]
    tq = tk = _tile(seq, 1024)
    nq = nk = seq // tq
    hp = 2
    body = functools.partial(_flash_body, tq=tq, tqs=_tile(tq, 256), hp=hp, dv=dv)
    pairs = [(i, j) for i in range(nq) for j in range(i + 1)]
    qi = jnp.asarray([p[0] for p in pairs], jnp.int32)
    kj = jnp.asarray([p[1] for p in pairs], jnp.int32)
    grid_spec = pltpu.PrefetchScalarGridSpec(
        num_scalar_prefetch=2,
        grid=(batch, hn // hp, len(pairs)),
        in_specs=[
            pl.BlockSpec((hp, tq, dq), lambda b, h, p, qi, kj: (h, b * nq + qi[p], 0)),
            pl.BlockSpec((hp, tk, dq), lambda b, h, p, qi, kj: (h, b * nk + kj[p], 0)),
            pl.BlockSpec((hp, tk, dv), lambda b, h, p, qi, kj: (h, b * nk + kj[p], 0)),
        ],
        out_specs=pl.BlockSpec((tq, hp * dv), lambda b, h, p, qi, kj: (b * nq + qi[p], h)),
        scratch_shapes=[pltpu.VMEM((hp, tq, 1), F32), pltpu.VMEM((hp, tq, 1), F32), pltpu.VMEM((hp, tq, dv), F32)],
    )
    return pl.pallas_call(
        body,
        grid_spec=grid_spec,
        out_shape=jax.ShapeDtypeStruct((t, hn * dv), BF16),
        compiler_params=_cparams(("arbitrary", "arbitrary", "arbitrary")),
        name="mla_flash",
    )(qi, kj, q, k, v)


def _alibi_slopes():
    return [2.0 ** (-8.0 * (h + 1) / SWA_Q_HEADS) for h in range(SWA_Q_HEADS)]


def _swa_body(sink_ref, q_ref, kc_ref, kp_ref, vc_ref, vp_ref, pq_ref, pkc_ref, pkp_ref, o_ref, *, blocks_per_seq):
    w = WINDOW
    n = pl.program_id(0) % blocks_per_seq
    pq = pq_ref[...]
    dist_c = jnp.abs(pq - pkc_ref[0])
    dist_p = jnp.abs(pq - pkp_ref[0])
    iq = lax.broadcasted_iota(jnp.int32, (w, w), 0)
    ik = lax.broadcasted_iota(jnp.int32, (w, w), 1)
    valid_c = ik <= iq
    valid_p = jnp.logical_and(ik > iq, n > 0)
    lane = lax.broadcasted_iota(jnp.int32, (w, 2 * SWA_HEAD_DIM), 1)
    left = lane < SWA_HEAD_DIM
    slopes = _alibi_slopes()
    pairs = SWA_GROUP // 2
    for kh in range(SWA_KV_HEADS):
        kc = kc_ref[:, kh * 128:(kh + 1) * 128]
        kp = kp_ref[:, kh * 128:(kh + 1) * 128]
        vc_l = vc_ref[:, kh * 256:kh * 256 + 128]
        vc_r = vc_ref[:, kh * 256 + 128:kh * 256 + 256]
        vp_l = vp_ref[:, kh * 256:kh * 256 + 128]
        vp_r = vp_ref[:, kh * 256 + 128:kh * 256 + 256]
        zero = jnp.zeros((w, 128), BF16)
        rows = []
        for half in range(2):
            for p in range(pairs):
                qp = q_ref[:, (kh * pairs + p) * 128:(kh * pairs + p + 1) * 128]
                rows.append(jnp.where(left if half == 0 else jnp.logical_not(left), qp, zero))
        qs = jnp.concatenate(rows, axis=0)
        dn = (((1,), (1,)), ((), ()))
        s_c = lax.dot_general(qs, kc, dn, preferred_element_type=F32)
        s_p = lax.dot_general(qs, kp, dn, preferred_element_type=F32)
        pc_rows, pp_rows = [], []
        for half in range(2):
            for p in range(pairs):
                hd = kh * SWA_GROUP + 2 * p + half
                r0 = (half * pairs + p) * w
                sink = sink_ref[hd]
                a_c = s_c[r0:r0 + w] * (SWA_HEAD_DIM ** -0.5) - slopes[hd] * dist_c
                a_p = s_p[r0:r0 + w] * (SWA_HEAD_DIM ** -0.5) - slopes[hd] * dist_p
                a_c = jnp.where(valid_c, a_c, NEG_INF)
                a_p = jnp.where(valid_p, a_p, NEG_INF)
                m = jnp.maximum(jnp.max(a_c, axis=-1, keepdims=True), jnp.max(a_p, axis=-1, keepdims=True))
                m = jnp.maximum(m, sink)
                e_c = jnp.exp(a_c - m)
                e_p = jnp.exp(a_p - m)
                den = jnp.sum(e_c, axis=-1, keepdims=True) + jnp.sum(e_p, axis=-1, keepdims=True) + jnp.exp(sink - m)
                pc_rows.append((e_c / den).astype(BF16))
                pp_rows.append((e_p / den).astype(BF16))
        hw = pairs * w
        pc = jnp.concatenate(pc_rows, axis=0)
        pp = jnp.concatenate(pp_rows, axis=0)
        o_pair = (jnp.dot(pc[0:hw], vc_l, preferred_element_type=F32)
                  + jnp.dot(pp[0:hw], vp_l, preferred_element_type=F32)
                  + jnp.dot(pc[hw:2 * hw], vc_r, preferred_element_type=F32)
                  + jnp.dot(pp[hw:2 * hw], vp_r, preferred_element_type=F32))
        for p in range(pairs):
            c0 = (kh * pairs + p) * 128
            o_ref[:, c0:c0 + 128] = o_pair[p * w:(p + 1) * w].astype(o_ref.dtype)


def swa_attention(qkv, sinks, pos_col, pos_row, batch, seq):
    t = qkv.shape[0]
    w = WINDOW
    nb = seq // w
    nq = SWA_Q_HEADS * SWA_HEAD_DIM
    body = functools.partial(_swa_body, blocks_per_seq=nb)

    def prev(i):
        return jnp.where(i % nb == 0, i, i - 1)

    kcol = (nq + 4 * 128) // 256
    return pl.pallas_call(
        body,
        grid=(t // w,),
        in_specs=[
            pl.BlockSpec(memory_space=pltpu.SMEM),
            pl.BlockSpec((w, nq), lambda i: (i, 0)),
            pl.BlockSpec((w, 256), lambda i: (i, kcol)),
            pl.BlockSpec((w, 256), lambda i: (prev(i), kcol)),
            pl.BlockSpec((w, 512), lambda i: (i, nq // 512)),
            pl.BlockSpec((w, 512), lambda i: (prev(i), nq // 512)),
            pl.BlockSpec((w, 1), lambda i: (i, 0)),
            pl.BlockSpec((1, 1, w), lambda i: (i, 0, 0)),
            pl.BlockSpec((1, 1, w), lambda i: (prev(i), 0, 0)),
        ],
        out_specs=pl.BlockSpec((w, nq), lambda i: (i, 0)),
        out_shape=jax.ShapeDtypeStruct((t, nq), BF16),
        compiler_params=_cparams(("arbitrary",)),
        name="swa_attention",
    )(sinks.astype(F32), qkv, qkv, qkv, qkv, qkv, pos_col, pos_row, pos_row)


def swa_qkv_weights(w_qkv, b_qkv):
    nq = SWA_Q_HEADS * SWA_HEAD_DIM
    nk = SWA_KV_HEADS * SWA_HEAD_DIM
    hd = SWA_HEAD_DIM

    def arrange(m):
        q = m[..., :nq]
        k = m[..., nq:nq + nk]
        v = m[..., nq + nk:]
        z = jnp.zeros(m.shape[:-1] + (hd,), m.dtype)
        parts = [q]
        for kh in range(SWA_KV_HEADS):
            vh = v[..., kh * hd:(kh + 1) * hd]
            parts += [vh, z, z, vh]
        for kh in range(SWA_KV_HEADS):
            kk = k[..., kh * hd:(kh + 1) * hd]
            parts += [kk, kk]
        return jnp.concatenate(parts, axis=-1)

    return arrange(w_qkv), arrange(b_qkv)


def _router_body(x_ref, g_ref, r_ref, eid_ref, rank_ref, wgt_ref, cnt_ref, run_ref, *, tm):
    i = pl.program_id(0)

    @pl.when(i == 0)
    def _():
        run_ref[...] = jnp.zeros_like(run_ref)

    h = _rms(x_ref[...], g_ref[...]).astype(BF16)
    logits = jnp.dot(h, r_ref[...], preferred_element_type=F32)
    lane = lax.broadcasted_iota(jnp.int32, logits.shape, 1)
    logits = jnp.where(lane < N_EXPERTS, logits, -jnp.inf)
    m1 = jnp.max(logits, axis=-1, keepdims=True)
    i1 = jnp.min(jnp.where(logits == m1, lane, 128), axis=-1, keepdims=True)
    rest = jnp.where(lane == i1, -jnp.inf, logits)
    m2 = jnp.max(rest, axis=-1, keepdims=True)
    i2 = jnp.min(jnp.where(rest == m2, lane, 128), axis=-1, keepdims=True)
    e2 = jnp.exp(m2 - m1)
    w1 = 1.0 / (1.0 + e2)
    w2 = e2 / (1.0 + e2)
    oh1 = (lane == i1)
    oh2 = (lane == i2)
    sel = jnp.where(jnp.logical_or(oh1, oh2), 1.0, 0.0).astype(BF16)
    r = lax.broadcasted_iota(jnp.int32, (tm, tm), 0)
    c = lax.broadcasted_iota(jnp.int32, (tm, tm), 1)
    tri = jnp.where(c < r, 1.0, 0.0).astype(BF16)
    before = jnp.dot(tri, sel, preferred_element_type=F32) + run_ref[...]
    rank1 = jnp.sum(jnp.where(oh1, before, 0.0), axis=-1, keepdims=True)
    rank2 = jnp.sum(jnp.where(oh2, before, 0.0), axis=-1, keepdims=True)
    run_ref[...] = run_ref[...] + jnp.sum(sel.astype(F32), axis=0, keepdims=True)
    eid_ref[...] = jnp.where(lane == 0, i1, jnp.where(lane == 1, i2, 0))[:, 0:TOP_K]
    rank_ref[...] = jnp.where(lane == 0, rank1, jnp.where(lane == 1, rank2, 0.0))[:, 0:TOP_K].astype(jnp.int32)
    wgt_ref[...] = jnp.where(lane == 0, w1, jnp.where(lane == 1, w2, 0.0))[:, 0:TOP_K]
    cnt_ref[...] = run_ref[...].astype(jnp.int32)


def moe_route(x, g, router):
    t, d = x.shape
    tm = _tile(t, 512)
    rpad = jnp.zeros((d, 128), F32).at[:, :N_EXPERTS].set(router).astype(BF16)
    body = functools.partial(_router_body, tm=tm)
    return pl.pallas_call(
        body,
        grid=(t // tm,),
        in_specs=[
            pl.BlockSpec((tm, d), lambda i: (i, 0)),
            _const_spec((1, d)),
            _const_spec((d, 128)),
        ],
        out_specs=[
            pl.BlockSpec((tm, TOP_K), lambda i: (i, 0)),
            pl.BlockSpec((tm, TOP_K), lambda i: (i, 0)),
            pl.BlockSpec((tm, TOP_K), lambda i: (i, 0)),
            _const_spec((1, 128)),
        ],
        out_shape=[
            jax.ShapeDtypeStruct((t, TOP_K), jnp.int32),
            jax.ShapeDtypeStruct((t, TOP_K), jnp.int32),
            jax.ShapeDtypeStruct((t, TOP_K), F32),
            jax.ShapeDtypeStruct((1, 128), jnp.int32),
        ],
        scratch_shapes=[pltpu.VMEM((1, 128), F32)],
        compiler_params=_cparams(("arbitrary",)),
        name="moe_route",
    )(x, g.reshape(1, d), rpad)


def _dispatch_body(dest_ref, x_ref, xs_in_ref, xs_ref, sem, *, tm):
    del xs_in_ref

    def issue(r, carry):
        for kk in range(TOP_K):
            dst = dest_ref[r * TOP_K + kk]
            pltpu.make_async_copy(x_ref.at[pl.ds(r, 1)], xs_ref.at[pl.ds(dst, 1)], sem).start()
        return carry

    lax.fori_loop(0, tm, issue, 0, unroll=8)
    for kk in range(TOP_K):
        pltpu.make_async_copy(x_ref, xs_ref.at[pl.ds(0, tm)], sem).wait()


def moe_dispatch(x, dest_flat, xs_init):
    t, d = x.shape
    tm = _tile(t, 512)
    body = functools.partial(_dispatch_body, tm=tm)
    return pl.pallas_call(
        body,
        grid=(t // tm,),
        in_specs=[
            pl.BlockSpec((tm * TOP_K,), lambda i: (i,), memory_space=pltpu.SMEM),
            pl.BlockSpec((tm, d), lambda i: (i, 0)),
            pl.BlockSpec(memory_space=pl.ANY),
        ],
        out_specs=pl.BlockSpec(memory_space=pl.ANY),
        out_shape=jax.ShapeDtypeStruct(xs_init.shape, xs_init.dtype),
        scratch_shapes=[pltpu.SemaphoreType.DMA(())],
        input_output_aliases={2: 0},
        compiler_params=_cparams(("arbitrary",)),
        name="moe_dispatch",
    )(dest_flat, x, xs_init)


def _expert_body(te_ref, nt_ref, xs_ref, g_ref, wg_ref, wu_ref, wd_ref, ys_ref, h_ref, acc_ref, *, nchunks):
    i = pl.program_id(0)
    j = pl.program_id(1)

    @pl.when(i < nt_ref[0])
    def _():
        @pl.when(j == 0)
        def _():
            h_ref[...] = _rms(xs_ref[...], g_ref[...]).astype(BF16)
            acc_ref[...] = jnp.zeros_like(acc_ref)

        h = h_ref[...]
        gate = jnp.dot(h, wg_ref[0], preferred_element_type=F32)
        up = jnp.dot(h, wu_ref[0], preferred_element_type=F32)
        a = (gate * jax.nn.sigmoid(gate) * up).astype(BF16)
        acc_ref[...] += jnp.dot(a, wd_ref[0], preferred_element_type=F32)

        @pl.when(j == nchunks - 1)
        def _():
            ys_ref[...] = acc_ref[...]

    @pl.when(jnp.logical_and(i >= nt_ref[0], j == nchunks - 1))
    def _():
        ys_ref[...] = jnp.zeros_like(ys_ref)


def moe_experts(xs, g, tile_expert, num_tiles, w_gate_up, w_down, tr):
    p, d = xs.shape
    ne, dff = w_down.shape[0], w_down.shape[1]
    tf = _ff_chunk(dff, 896)
    nchunks = dff // tf
    body = functools.partial(_expert_body, nchunks=nchunks)

    def row(i, nt):
        return jnp.minimum(i, nt[0] - 1)

    def chunk(i, j, nt):
        return jnp.where(i < nt[0], j, nchunks - 1)

    grid_spec = pltpu.PrefetchScalarGridSpec(
        num_scalar_prefetch=2,
        grid=(p // tr, nchunks),
        in_specs=[
            pl.BlockSpec((tr, d), lambda i, j, te, nt: (row(i, nt), 0)),
            pl.BlockSpec((1, d), lambda i, j, te, nt: (0, 0)),
            pl.BlockSpec((1, d, tf), lambda i, j, te, nt: (te[row(i, nt)], 0, chunk(i, j, nt))),
            pl.BlockSpec((1, d, tf), lambda i, j, te, nt: (te[row(i, nt)], 0, chunk(i, j, nt) + nchunks)),
            pl.BlockSpec((1, tf, d), lambda i, j, te, nt: (te[row(i, nt)], chunk(i, j, nt), 0)),
        ],
        out_specs=pl.BlockSpec((tr, d), lambda i, j, te, nt: (i, 0)),
        scratch_shapes=[pltpu.VMEM((tr, d), BF16), pltpu.VMEM((tr, d), F32)],
    )
    wgu = w_gate_up.astype(BF16)
    return pl.pallas_call(
        body,
        grid_spec=grid_spec,
        out_shape=jax.ShapeDtypeStruct((p, d), F32),
        compiler_params=_cparams(("arbitrary", "arbitrary")),
        name="moe_experts",
    )(tile_expert, num_tiles, xs, g.reshape(1, d), wgu, wgu, w_down.astype(BF16))


def _combine_body(dest_ref, x_ref, w_ref, gf_ref, ys_ref, o_ref, buf_ref, sem, *, tm, final_norm):
    def issue(r, carry):
        for kk in range(TOP_K):
            src = dest_ref[r * TOP_K + kk]
            pltpu.make_async_copy(ys_ref.at[pl.ds(src, 1)], buf_ref.at[kk, pl.ds(r, 1)], sem).start()
        return carry

    lax.fori_loop(0, tm, issue, 0, unroll=8)
    for kk in range(TOP_K):
        pltpu.make_async_copy(ys_ref.at[pl.ds(0, tm)], buf_ref.at[kk], sem).wait()
    w = w_ref[...]
    out = x_ref[...] + (w[:, 0:1] * buf_ref[0] + w[:, 1:2] * buf_ref[1])
    if final_norm:
        out = _rms(out, gf_ref[...])
    o_ref[...] = out


def moe_combine(x, wgt, dest_flat, ys, g_final, final_norm):
    t, d = x.shape
    tm = _tile(t, 512)
    body = functools.partial(_combine_body, tm=tm, final_norm=final_norm)
    return pl.pallas_call(
        body,
        grid=(t // tm,),
        in_specs=[
            pl.BlockSpec((tm * TOP_K,), lambda i: (i,), memory_space=pltpu.SMEM),
            pl.BlockSpec((tm, d), lambda i: (i, 0)),
            pl.BlockSpec((tm, TOP_K), lambda i: (i, 0)),
            pl.BlockSpec((1, d), lambda i: (0, 0)),
            pl.BlockSpec(memory_space=pl.ANY),
        ],
        out_specs=pl.BlockSpec((tm, d), lambda i: (i, 0)),
        out_shape=jax.ShapeDtypeStruct((t, d), F32),
        scratch_shapes=[pltpu.VMEM((TOP_K, tm, d), F32), pltpu.SemaphoreType.DMA(())],
        compiler_params=_cparams(("arbitrary",)),
        name="moe_combine",
    )(dest_flat, x, wgt, g_final.reshape(1, d), ys)


def moe_layer(x, g, router, w_gate_up, w_down, g_final, final_norm):
    t, d = x.shape
    tr = min(1024, t)
    eid, rank, wgt, cnt = moe_route(x, g, router)
    counts = cnt[0, :N_EXPERTS]
    padded = ((counts + tr - 1) // tr) * tr
    ends = jnp.cumsum(padded)
    base = ends - padded
    dest = (base[eid] + rank).reshape(-1).astype(jnp.int32)
    ntiles_max = (t * TOP_K) // tr + N_EXPERTS
    num_tiles = (ends[-1] // tr).astype(jnp.int32).reshape(1)
    tile_start = jnp.arange(ntiles_max, dtype=jnp.int32) * tr
    tile_expert = jnp.minimum(jnp.sum(tile_start[:, None] >= ends[None, :], axis=1), N_EXPERTS - 1).astype(jnp.int32)
    p = ntiles_max * tr
    xs = moe_dispatch(x, dest, jnp.zeros((p, d), F32))
    ys = moe_experts(xs, g, tile_expert, num_tiles, w_gate_up, w_down, tr)
    return moe_combine(x, wgt, dest, ys, g_final, final_norm)


def kernel(x, positions, norm_mix, norm_ffn, norm_final, conv_w_in, conv_w, conv_w_out, mla_w_down, mla_q_norm,
           mla_w_uq, mla_kv_norm, mla_w_ukv, mla_w_o, swa_w_qkv, swa_b_qkv, swa_sinks, swa_w_o, swa_b_o,
           ffn_w_gate_up, ffn_w_down, moe_router, moe_w_gate_up, moe_w_down):
    batch, seq, d = x.shape
    depth = norm_mix.shape[0]
    t = batch * seq
    xf = x.reshape(t, d)
    pos_col = positions.reshape(t, 1).astype(F32)
    pos_row = positions.reshape(t // WINDOW, 1, WINDOW).astype(F32)
    zeros_d = jnp.zeros((d,), F32)
    for i in range(depth):
        kind = i % N_MIXERS
        j = i // N_MIXERS
        if kind == 0:
            xf = conv_mixer(xf, norm_mix[i], conv_w_in[j], conv_w[j], conv_w_out[j], seq)
        elif kind == 1:
            q, k, v = mla_project(xf, pos_col, norm_mix[i], mla_w_down[j], mla_q_norm[j], mla_w_uq[j],
                                  mla_kv_norm[j], mla_w_ukv[j])
            o = mla_flash(q, k, v, batch, seq)
            xf = linear_residual(xf, o, mla_w_o[j], zeros_d)
        else:
            wq, bq = swa_qkv_weights(swa_w_qkv[j], swa_b_qkv[j])
            qkv = rms_linear(xf, norm_mix[i], wq, bq)
            o = swa_attention(qkv, swa_sinks[j], pos_col, pos_row, batch, seq)
            xf = linear_residual(xf, o, swa_w_o[j], swa_b_o[j])
        f_idx = i // 2
        last = i == depth - 1
        if i % 2 == 0:
            xf = dense_ffn(xf, norm_ffn[i], ffn_w_gate_up[f_idx], ffn_w_down[f_idx])
            if last:
                xf = final_rmsnorm(xf, norm_final)
        else:
            xf = moe_layer(xf, norm_ffn[i], moe_router[f_idx], moe_w_gate_up[f_idx], moe_w_down[f_idx],
                           norm_final, last)
    return xf.reshape(batch, seq, d)


def _final_norm_body(x_ref, g_ref, o_ref):
    o_ref[...] = _rms(x_ref[...], g_ref[...])


def final_rmsnorm(x, g):
    t, d = x.shape
    tm = _tile(t, 1024)
    return pl.pallas_call(
        _final_norm_body,
        grid=(t // tm,),
        in_specs=[pl.BlockSpec((tm, d), lambda i: (i, 0)), _const_spec((1, d))],
        out_specs=pl.BlockSpec((tm, d), lambda i: (i, 0)),
        out_shape=jax.ShapeDtypeStruct((t, d), F32),
        compiler_params=_cparams(("arbitrary",)),
        name="final_rmsnorm",
    )(x, g.reshape(1, d))
```

```python
import functools

import jax
import jax.numpy as jnp
from jax import lax
from jax.experimental import pallas as pl
from jax.experimental.pallas import tpu as pltpu

F32 = jnp.float32
BF16 = jnp.bfloat16

RMS_EPS = 1e-6
NEG_INF = -1e30
CONV_WIDTH = 3
MLA_HEADS = 8
MLA_Q_LORA = 384
MLA_KV_LORA = 256
MLA_NOPE = 128
MLA_ROPE = 64
MLA_V = 128
MLA_QK_PAD = 256
LOG2E = 1.4426950408889634
MLA_Q_SCALE = float((MLA_NOPE + MLA_ROPE) ** -0.5 * LOG2E)
ROPE_THETA = 10000.0
SWA_Q_HEADS = 16
SWA_KV_HEADS = 2
SWA_GROUP = SWA_Q_HEADS // SWA_KV_HEADS
SWA_HEAD_DIM = 64
WINDOW = 128
SWA_Q_SCALE = float(SWA_HEAD_DIM ** -0.5 * LOG2E)
SWA_MASKED_DIST = 1e32
N_EXPERTS = 8
TOP_K = 2
N_MIXERS = 3

VMEM_LIMIT = 56 * 1024 * 1024


def _cparams(sem):
    return pltpu.CompilerParams(dimension_semantics=sem, vmem_limit_bytes=VMEM_LIMIT)


def _rms(xf, g):
    ms = jnp.mean(xf * xf, axis=-1, keepdims=True)
    return xf * lax.rsqrt(ms + RMS_EPS) * g


def _const_spec(shape):
    nd = len(shape)
    return pl.BlockSpec(shape, lambda *_: (0,) * nd)


def _resident_spec(shape):
    nd = len(shape)
    return pl.BlockSpec(shape, lambda *_: (0,) * nd, pipeline_mode=pl.Buffered(1))


def _tile(n, pref):
    t = min(n, pref)
    assert n % t == 0, (n, t)
    return t


def _conv_body(x_ref, g_ref, win_ref, wc_ref, wout_ref, o_ref, vpad_ref, *, tm, d, tiles_per_seq):
    i = pl.program_id(0)
    x = x_ref[...]
    h = _rms(x, g_ref[...]).astype(BF16)
    bg = jnp.dot(h, win_ref[:, 0:d], preferred_element_type=F32)
    cg = jnp.dot(h, win_ref[:, d:2 * d], preferred_element_type=F32)
    u = jnp.dot(h, win_ref[:, 2 * d:3 * d], preferred_element_type=F32)
    v = cg * u

    @pl.when(i % tiles_per_seq == 0)
    def _():
        vpad_ref[0:8, :] = jnp.zeros((8, d), F32)

    vpad_ref[8:8 + tm, :] = v
    conv = vpad_ref[6:6 + tm, :] * wc_ref[0:1, :]
    conv = conv + vpad_ref[7:7 + tm, :] * wc_ref[1:2, :]
    conv = conv + v * wc_ref[2:3, :]
    vpad_ref[0:8, :] = vpad_ref[tm:tm + 8, :]
    y = (bg * conv).astype(BF16)
    o_ref[...] = x + jnp.dot(y, wout_ref[...], preferred_element_type=F32)


def conv_mixer(x, g, w_in, w_conv, w_out, seq):
    t, d = x.shape
    tm = _tile(seq, 512)
    body = functools.partial(_conv_body, tm=tm, d=d, tiles_per_seq=seq // tm)
    return pl.pallas_call(
        body,
        grid=(t // tm,),
        in_specs=[
            pl.BlockSpec((tm, d), lambda i: (i, 0)),
            _const_spec((1, d)),
            _resident_spec((d, 3 * d)),
            _const_spec((CONV_WIDTH, d)),
            _resident_spec((d, d)),
        ],
        out_specs=pl.BlockSpec((tm, d), lambda i: (i, 0)),
        out_shape=jax.ShapeDtypeStruct((t, d), F32),
        scratch_shapes=[pltpu.VMEM((tm + 8, d), F32)],
        compiler_params=_cparams(("arbitrary",)),
        name="conv_mixer",
    )(x, g.reshape(1, d), w_in.astype(BF16), w_conv, w_out.astype(BF16))


def _swiglu(h, wgu_cols, wd, a_ref, dff, tf):
    def gate_up(c):
        gate = jnp.dot(h, wgu_cols(c * tf, (c + 1) * tf), preferred_element_type=F32)
        up = jnp.dot(h, wgu_cols(dff + c * tf, dff + (c + 1) * tf), preferred_element_type=F32)
        return gate, up

    nxt = gate_up(0)
    for c in range(dff // tf):
        gate, up = nxt
        if c + 1 < dff // tf:
            nxt = gate_up(c + 1)
        a_ref[:, c * tf:(c + 1) * tf] = (gate * jax.nn.sigmoid(gate) * up).astype(BF16)
    return jnp.dot(a_ref[...], wd, preferred_element_type=F32)


def _ffn_body(x_ref, g_ref, wgu_ref, wd_ref, o_ref, a_ref, *, dff, tf):
    x = x_ref[...]
    h = _rms(x, g_ref[...]).astype(BF16)
    o_ref[...] = x + _swiglu(h, lambda lo, hi: wgu_ref[:, lo:hi], wd_ref[...], a_ref, dff, tf)


def _ff_chunk(dff, pref):
    best = None
    for c in range(128, dff + 1, 128):
        if dff % c == 0 and c <= pref:
            best = c
    return best if best is not None else dff


def dense_ffn(x, g, w_gate_up, w_down):
    t, d = x.shape
    dff = w_down.shape[0]
    tm = _tile(t, 512)
    tf = _ff_chunk(dff, 768)
    body = functools.partial(_ffn_body, dff=dff, tf=tf)
    return pl.pallas_call(
        body,
        grid=(t // tm,),
        in_specs=[
            pl.BlockSpec((tm, d), lambda i: (i, 0)),
            _const_spec((1, d)),
            _resident_spec((d, 2 * dff)),
            _resident_spec((dff, d)),
        ],
        out_specs=pl.BlockSpec((tm, d), lambda i: (i, 0)),
        out_shape=jax.ShapeDtypeStruct((t, d), F32),
        scratch_shapes=[pltpu.VMEM((tm, dff), BF16)],
        compiler_params=_cparams(("arbitrary",)),
        name="dense_ffn",
    )(x, g.reshape(1, d), w_gate_up.astype(BF16), w_down.astype(BF16))


def _linres_body(x_ref, a_ref, w_ref, b_ref, o_ref):
    o_ref[...] = (x_ref[...] + b_ref[...]) + jnp.dot(a_ref[...], w_ref[...], preferred_element_type=F32)


def linear_residual(x, a, w, b):
    t, d = x.shape
    k = a.shape[1]
    tm = _tile(t, 1024)
    return pl.pallas_call(
        _linres_body,
        grid=(t // tm,),
        in_specs=[
            pl.BlockSpec((tm, d), lambda i: (i, 0)),
            pl.BlockSpec((tm, k), lambda i: (i, 0)),
            _const_spec((k, d)),
            _const_spec((1, d)),
        ],
        out_specs=pl.BlockSpec((tm, d), lambda i: (i, 0)),
        out_shape=jax.ShapeDtypeStruct((t, d), F32),
        compiler_params=_cparams(("arbitrary",)),
        name="linear_residual",
    )(x, a, w.astype(BF16), b.reshape(1, d).astype(F32))


def _rmslin_body(x_ref, g_ref, w_ref, b_ref, o_ref):
    h = _rms(x_ref[...], g_ref[...]).astype(BF16)
    o_ref[...] = (jnp.dot(h, w_ref[...], preferred_element_type=F32) + b_ref[...]).astype(o_ref.dtype)


def rms_linear(x, g, w, b):
    t, d = x.shape
    n = w.shape[1]
    tm = _tile(t, 512)
    return pl.pallas_call(
        _rmslin_body,
        grid=(t // tm,),
        in_specs=[
            pl.BlockSpec((tm, d), lambda i: (i, 0)),
            _const_spec((1, d)),
            _const_spec((d, n)),
            _const_spec((1, n)),
        ],
        out_specs=pl.BlockSpec((tm, n), lambda i: (i, 0)),
        out_shape=jax.ShapeDtypeStruct((t, n), BF16),
        compiler_params=_cparams(("arbitrary",)),
        name="rms_linear",
    )(x, g.reshape(1, d), w.astype(BF16), b.reshape(1, n).astype(F32))


def _mla_proj_body(x_ref, pos_ref, g_ref, wd_ref, qn_ref, kvn_ref, wqa_ref, wqb_ref, wkv_ref, frq_ref,
                   q_ref, k_ref, v_ref):
    h = _rms(x_ref[...], g_ref[...]).astype(BF16)
    down = jnp.dot(h, wd_ref[...], preferred_element_type=F32)
    c_q = down[:, 0:MLA_Q_LORA]
    c_kv = down[:, MLA_Q_LORA:MLA_Q_LORA + MLA_KV_LORA]
    o = MLA_Q_LORA + MLA_KV_LORA
    kpe = down[:, o:o + 128]
    kpe_sw = down[:, o + 128:o + 256]
    hq = _rms(c_q, qn_ref[...]).astype(BF16)
    hkv = _rms(c_kv, kvn_ref[...]).astype(BF16)
    qa = jnp.dot(hq, wqa_ref[...], preferred_element_type=F32)
    qb = jnp.dot(hq, wqb_ref[...], preferred_element_type=F32)
    kv = jnp.dot(hkv, wkv_ref[...], preferred_element_type=F32)
    ang = pos_ref[...] * frq_ref[0:1, :]
    cos = jnp.cos(ang)
    sin = jnp.sin(ang) * frq_ref[1:2, :]
    kpe_r = (kpe * cos + kpe_sw * sin).astype(BF16)
    ones_col = jnp.where(lax.broadcasted_iota(jnp.int32, cos.shape, 1) == 0, 1.0, 0.0).astype(BF16)
    for hd in range(MLA_HEADS):
        qo = hd * MLA_QK_PAD
        q_ref[hd, :, 0:MLA_NOPE] = (qa[:, qo:qo + MLA_NOPE] * MLA_Q_SCALE).astype(BF16)
        qpe = qa[:, qo + MLA_NOPE:qo + MLA_QK_PAD] * cos + qb[:, hd * 128:(hd + 1) * 128] * sin
        q_ref[hd, :, MLA_NOPE:MLA_QK_PAD] = (qpe * MLA_Q_SCALE).astype(BF16)
        ko = hd * (MLA_NOPE + MLA_V)
        k_ref[hd, :, 0:MLA_NOPE] = kv[:, ko:ko + MLA_NOPE].astype(BF16)
        k_ref[hd, :, MLA_NOPE:MLA_QK_PAD] = kpe_r
        v_ref[hd, :, 0:MLA_V] = kv[:, ko + MLA_NOPE:ko + MLA_NOPE + MLA_V].astype(BF16)
        v_ref[hd, :, MLA_V:2 * MLA_V] = ones_col


def _rope_swap(w):
    half = MLA_ROPE // 2
    return jnp.concatenate([w[..., half:], w[..., :half]], axis=-1)


def mla_project(x, pos_col, g, w_down, q_norm, w_uq, kv_norm, w_ukv):
    t, d = x.shape
    tm = _tile(t, 256)
    hn = MLA_HEADS
    z64 = jnp.zeros((d, 64), F32)
    o = MLA_Q_LORA + MLA_KV_LORA
    kpe_w = w_down[:, o:o + MLA_ROPE]
    wd = jnp.concatenate([w_down[:, :o], kpe_w, z64, _rope_swap(kpe_w), z64], axis=1).astype(BF16)
    wq = w_uq.reshape(MLA_Q_LORA, hn, MLA_NOPE + MLA_ROPE)
    zq = jnp.zeros((MLA_Q_LORA, hn, 64), F32)
    wqa = jnp.concatenate([wq, zq], axis=-1).reshape(MLA_Q_LORA, hn * MLA_QK_PAD).astype(BF16)
    wqb = jnp.concatenate([_rope_swap(wq[..., MLA_NOPE:]), zq], axis=-1).reshape(MLA_Q_LORA, hn * 128).astype(BF16)
    inv = ROPE_THETA ** (-jnp.arange(0, MLA_ROPE, 2, dtype=F32) / MLA_ROPE)
    half = MLA_ROPE // 2
    frq = jnp.stack([
        jnp.concatenate([inv, inv, jnp.zeros((64,), F32)]),
        jnp.concatenate([-jnp.ones((half,), F32), jnp.ones((half,), F32), jnp.zeros((64,), F32)]),
    ])
    nd = wd.shape[1]
    outs = pl.pallas_call(
        _mla_proj_body,
        grid=(t // tm,),
        in_specs=[
            pl.BlockSpec((tm, d), lambda i: (i, 0)),
            pl.BlockSpec((tm, 1), lambda i: (i, 0)),
            _const_spec((1, d)),
            _const_spec((d, nd)),
            _const_spec((1, MLA_Q_LORA)),
            _const_spec((1, MLA_KV_LORA)),
            _const_spec((MLA_Q_LORA, hn * MLA_QK_PAD)),
            _const_spec((MLA_Q_LORA, hn * 128)),
            _const_spec((MLA_KV_LORA, hn * (MLA_NOPE + MLA_V))),
            _const_spec((2, 128)),
        ],
        out_specs=[
            pl.BlockSpec((hn, tm, MLA_QK_PAD), lambda i: (0, i, 0)),
            pl.BlockSpec((hn, tm, MLA_QK_PAD), lambda i: (0, i, 0)),
            pl.BlockSpec((hn, tm, 2 * MLA_V), lambda i: (0, i, 0)),
        ],
        out_shape=[
            jax.ShapeDtypeStruct((hn, t, MLA_QK_PAD), BF16),
            jax.ShapeDtypeStruct((hn, t, MLA_QK_PAD), BF16),
            jax.ShapeDtypeStruct((hn, t, 2 * MLA_V), BF16),
        ],
        compiler_params=_cparams(("arbitrary",)),
        name="mla_project",
    )(x, pos_col, g.reshape(1, d), wd, q_norm.reshape(1, -1), kv_norm.reshape(1, -1), wqa, wqb,
      w_ukv.astype(BF16), frq)
    return outs


def _flash_body(qi_ref, kj_ref, q_ref, k_ref, v_ref, o_ref, m_ref, acc_ref, *, tq, tqs, hp, dv):
    pair = pl.program_id(2)
    i = qi_ref[pair]
    j = kj_ref[pair]

    @pl.when(j == 0)
    def _():
        m_ref[...] = jnp.full_like(m_ref, NEG_INF)
        acc_ref[...] = jnp.zeros_like(acc_ref)

    def step(diagonal):
        nsub = tq // tqs
        units = [(h, r) for h in range(hp) for r in range(nsub)]
        causal = (lax.broadcasted_iota(jnp.int32, (tqs, tqs), 1) <= lax.broadcasted_iota(jnp.int32, (tqs, tqs), 0))

        def scores(u):
            h, r = u
            nk = (r + 1) * tqs if diagonal else tq
            return lax.dot_general(q_ref[h, r * tqs:(r + 1) * tqs, :], k_ref[h, 0:nk, :], (((1,), (1,)), ((), ())),
                                   preferred_element_type=F32)

        s_next = scores(units[0])
        for n, (h, r) in enumerate(units):
            rows = slice(r * tqs, (r + 1) * tqs)
            nk = (r + 1) * tqs if diagonal else tq
            s = s_next
            if n + 1 < len(units):
                s_next = scores(units[n + 1])
            if diagonal:
                s_diag = jnp.where(causal, s[:, nk - tqs:nk], NEG_INF)
                s = s_diag if r == 0 else jnp.concatenate([s[:, 0:nk - tqs], s_diag], axis=1)
            m_prev = m_ref[h, rows, :]
            m_new = jnp.maximum(m_prev, jnp.max(s, axis=-1, keepdims=True))
            alpha = jnp.exp2(m_prev - m_new)
            p = jnp.exp2(s - m_new)
            pv = jnp.dot(p.astype(BF16), v_ref[h, 0:nk, :], preferred_element_type=F32)
            acc_ref[h, rows, :] = alpha * acc_ref[h, rows, :] + pv
            m_ref[h, rows, :] = m_new

    @pl.when(j < i)
    def _():
        step(False)

    @pl.when(j == i)
    def _():
        step(True)
        for h in range(hp):
            o_ref[:, h * dv:(h + 1) * dv] = (acc_ref[h, :, 0:dv] / acc_ref[h, :, dv:dv + 1]).astype(o_ref.dtype)


def mla_flash(q, k, v, batch, seq):
    hn, t, dq = q.shape
    dvp = v.shape[2]
    dv = MLA_V
    tq = tk = _tile(seq, 1024)
    nq = nk = seq // tq
    hp = 2
    body = functools.partial(_flash_body, tq=tq, tqs=_tile(tq, 256), hp=hp, dv=dv)
    pairs = [(i, j) for i in range(nq) for j in range(i + 1)]
    qi = jnp.asarray([p[0] for p in pairs], jnp.int32)
    kj = jnp.asarray([p[1] for p in pairs], jnp.int32)
    grid_spec = pltpu.PrefetchScalarGridSpec(
        num_scalar_prefetch=2,
        grid=(batch, hn // hp, len(pairs)),
        in_specs=[
            pl.BlockSpec((hp, tq, dq), lambda b, h, p, qi, kj: (h, b * nq + qi[p], 0)),
            pl.BlockSpec((hp, tk, dq), lambda b, h, p, qi, kj: (h, b * nk + kj[p], 0)),
            pl.BlockSpec((hp, tk, dvp), lambda b, h, p, qi, kj: (h, b * nk + kj[p], 0)),
        ],
        out_specs=pl.BlockSpec((tq, hp * dv), lambda b, h, p, qi, kj: (b * nq + qi[p], h)),
        scratch_shapes=[pltpu.VMEM((hp, tq, 1), F32), pltpu.VMEM((hp, tq, dvp), F32)],
    )
    return pl.pallas_call(
        body,
        grid_spec=grid_spec,
        out_shape=jax.ShapeDtypeStruct((t, hn * dv), BF16),
        compiler_params=_cparams(("arbitrary", "arbitrary", "arbitrary")),
        name="mla_flash",
    )(qi, kj, q, k, v)


def _alibi_slopes():
    return [2.0 ** (-8.0 * (h + 1) / SWA_Q_HEADS) for h in range(SWA_Q_HEADS)]


def _swa_body(sink_ref, q_ref, kc_ref, kp_ref, vc_ref, vp_ref, pq_ref, pkc_ref, pkp_ref, o_ref, *, blocks_per_seq):
    w = WINDOW
    n = pl.program_id(0) % blocks_per_seq
    pq = pq_ref[...]
    dist_c = jnp.abs(pq - pkc_ref[0])
    dist_p = jnp.abs(pq - pkp_ref[0])
    iq = lax.broadcasted_iota(jnp.int32, (w, w), 0)
    ik = lax.broadcasted_iota(jnp.int32, (w, w), 1)
    dist_c = jnp.where(ik <= iq, dist_c, SWA_MASKED_DIST)
    dist_p = jnp.where(jnp.logical_and(ik > iq, n > 0), dist_p, SWA_MASKED_DIST)
    lane = lax.broadcasted_iota(jnp.int32, (w, 2 * SWA_HEAD_DIM), 1)
    left = lane < SWA_HEAD_DIM
    slopes = _alibi_slopes()
    pairs = SWA_GROUP // 2
    for kh in range(SWA_KV_HEADS):
        kc = kc_ref[:, kh * 128:(kh + 1) * 128]
        kp = kp_ref[:, kh * 128:(kh + 1) * 128]
        vc_l = vc_ref[:, kh * 256:kh * 256 + 128]
        vc_r = vc_ref[:, kh * 256 + 128:kh * 256 + 256]
        vp_l = vp_ref[:, kh * 256:kh * 256 + 128]
        vp_r = vp_ref[:, kh * 256 + 128:kh * 256 + 256]
        zero = jnp.zeros((w, 128), BF16)
        rows = []
        for half in range(2):
            for p in range(pairs):
                qp = q_ref[:, (kh * pairs + p) * 128:(kh * pairs + p + 1) * 128]
                rows.append(jnp.where(left if half == 0 else jnp.logical_not(left), qp, zero))
        qs = jnp.concatenate(rows, axis=0)
        dn = (((1,), (1,)), ((), ()))
        s_c = lax.dot_general(qs, kc, dn, preferred_element_type=F32)
        s_p = lax.dot_general(qs, kp, dn, preferred_element_type=F32)
        pc_rows, pp_rows = [], []
        for half in range(2):
            for p in range(pairs):
                hd = kh * SWA_GROUP + 2 * p + half
                r0 = (half * pairs + p) * w
                sink = sink_ref[hd] * LOG2E
                a_c = s_c[r0:r0 + w] - (slopes[hd] * LOG2E) * dist_c
                a_p = s_p[r0:r0 + w] - (slopes[hd] * LOG2E) * dist_p
                m = jnp.maximum(jnp.max(jnp.maximum(a_c, a_p), axis=-1, keepdims=True), sink)
                e_c = jnp.exp2(a_c - m)
                e_p = jnp.exp2(a_p - m)
                den = jnp.sum(e_c + e_p, axis=-1, keepdims=True) + jnp.exp2(sink - m)
                inv = 1.0 / den
                pc_rows.append((e_c * inv).astype(BF16))
                pp_rows.append((e_p * inv).astype(BF16))
        hw = pairs * w
        pc = jnp.concatenate(pc_rows, axis=0)
        pp = jnp.concatenate(pp_rows, axis=0)
        o_pair = (jnp.dot(pc[0:hw], vc_l, preferred_element_type=F32)
                  + jnp.dot(pp[0:hw], vp_l, preferred_element_type=F32)
                  + jnp.dot(pc[hw:2 * hw], vc_r, preferred_element_type=F32)
                  + jnp.dot(pp[hw:2 * hw], vp_r, preferred_element_type=F32))
        for p in range(pairs):
            c0 = (kh * pairs + p) * 128
            o_ref[:, c0:c0 + 128] = o_pair[p * w:(p + 1) * w].astype(o_ref.dtype)


def swa_attention(qkv, sinks, pos_col, pos_row, batch, seq):
    t = qkv.shape[0]
    w = WINDOW
    nb = seq // w
    nq = SWA_Q_HEADS * SWA_HEAD_DIM
    body = functools.partial(_swa_body, blocks_per_seq=nb)

    def prev(i):
        return jnp.where(i % nb == 0, i, i - 1)

    kcol = (nq + 4 * 128) // 256
    return pl.pallas_call(
        body,
        grid=(t // w,),
        in_specs=[
            pl.BlockSpec(memory_space=pltpu.SMEM),
            pl.BlockSpec((w, nq), lambda i: (i, 0)),
            pl.BlockSpec((w, 256), lambda i: (i, kcol)),
            pl.BlockSpec((w, 256), lambda i: (prev(i), kcol)),
            pl.BlockSpec((w, 512), lambda i: (i, nq // 512)),
            pl.BlockSpec((w, 512), lambda i: (prev(i), nq // 512)),
            pl.BlockSpec((w, 1), lambda i: (i, 0)),
            pl.BlockSpec((1, 1, w), lambda i: (i, 0, 0)),
            pl.BlockSpec((1, 1, w), lambda i: (prev(i), 0, 0)),
        ],
        out_specs=pl.BlockSpec((w, nq), lambda i: (i, 0)),
        out_shape=jax.ShapeDtypeStruct((t, nq), BF16),
        compiler_params=_cparams(("arbitrary",)),
        name="swa_attention",
    )(sinks.astype(F32), qkv, qkv, qkv, qkv, qkv, pos_col, pos_row, pos_row)


def swa_qkv_weights(w_qkv, b_qkv):
    nq = SWA_Q_HEADS * SWA_HEAD_DIM
    nk = SWA_KV_HEADS * SWA_HEAD_DIM
    hd = SWA_HEAD_DIM

    def arrange(m):
        q = m[..., :nq] * SWA_Q_SCALE
        k = m[..., nq:nq + nk]
        v = m[..., nq + nk:]
        z = jnp.zeros(m.shape[:-1] + (hd,), m.dtype)
        parts = [q]
        for kh in range(SWA_KV_HEADS):
            vh = v[..., kh * hd:(kh + 1) * hd]
            parts += [vh, z, z, vh]
        for kh in range(SWA_KV_HEADS):
            kk = k[..., kh * hd:(kh + 1) * hd]
            parts += [kk, kk]
        return jnp.concatenate(parts, axis=-1)

    return arrange(w_qkv), arrange(b_qkv)


def _router_body(x_ref, g_ref, r_ref, eid_ref, rank_ref, wgt_ref, cnt_ref, run_ref, *, tm):
    i = pl.program_id(0)

    @pl.when(i == 0)
    def _():
        run_ref[...] = jnp.zeros_like(run_ref)

    h = _rms(x_ref[...], g_ref[...]).astype(BF16)
    logits = jnp.dot(h, r_ref[...], preferred_element_type=F32)
    lane = lax.broadcasted_iota(jnp.int32, logits.shape, 1)
    logits = jnp.where(lane < N_EXPERTS, logits, -jnp.inf)
    m1 = jnp.max(logits, axis=-1, keepdims=True)
    i1 = jnp.min(jnp.where(logits == m1, lane, 128), axis=-1, keepdims=True)
    rest = jnp.where(lane == i1, -jnp.inf, logits)
    m2 = jnp.max(rest, axis=-1, keepdims=True)
    i2 = jnp.min(jnp.where(rest == m2, lane, 128), axis=-1, keepdims=True)
    e2 = jnp.exp(m2 - m1)
    w1 = 1.0 / (1.0 + e2)
    w2 = e2 / (1.0 + e2)
    oh1 = (lane == i1)
    oh2 = (lane == i2)
    sel = jnp.where(jnp.logical_or(oh1, oh2), 1.0, 0.0).astype(BF16)
    r = lax.broadcasted_iota(jnp.int32, (tm, tm), 0)
    c = lax.broadcasted_iota(jnp.int32, (tm, tm), 1)
    tri = jnp.where(c < r, 1.0, 0.0).astype(BF16)
    before = jnp.dot(tri, sel, preferred_element_type=F32) + run_ref[...]
    rank1 = jnp.sum(jnp.where(oh1, before, 0.0), axis=-1, keepdims=True)
    rank2 = jnp.sum(jnp.where(oh2, before, 0.0), axis=-1, keepdims=True)
    run_ref[...] = run_ref[...] + jnp.sum(sel.astype(F32), axis=0, keepdims=True)
    eid_ref[...] = jnp.where(lane == 0, i1, jnp.where(lane == 1, i2, 0))[:, 0:TOP_K]
    rank_ref[...] = jnp.where(lane == 0, rank1, jnp.where(lane == 1, rank2, 0.0))[:, 0:TOP_K].astype(jnp.int32)
    wgt_ref[...] = jnp.where(lane == 0, w1, jnp.where(lane == 1, w2, 0.0))[:, 0:TOP_K]
    cnt_ref[...] = run_ref[...].astype(jnp.int32)


def moe_route(x, g, router):
    t, d = x.shape
    tm = _tile(t, 512)
    rpad = jnp.zeros((d, 128), F32).at[:, :N_EXPERTS].set(router).astype(BF16)
    body = functools.partial(_router_body, tm=tm)
    return pl.pallas_call(
        body,
        grid=(t // tm,),
        in_specs=[
            pl.BlockSpec((tm, d), lambda i: (i, 0)),
            _const_spec((1, d)),
            _const_spec((d, 128)),
        ],
        out_specs=[
            pl.BlockSpec((tm, TOP_K), lambda i: (i, 0)),
            pl.BlockSpec((tm, TOP_K), lambda i: (i, 0)),
            pl.BlockSpec((tm, TOP_K), lambda i: (i, 0)),
            _const_spec((1, 128)),
        ],
        out_shape=[
            jax.ShapeDtypeStruct((t, TOP_K), jnp.int32),
            jax.ShapeDtypeStruct((t, TOP_K), jnp.int32),
            jax.ShapeDtypeStruct((t, TOP_K), F32),
            jax.ShapeDtypeStruct((1, 128), jnp.int32),
        ],
        scratch_shapes=[pltpu.VMEM((1, 128), F32)],
        compiler_params=_cparams(("arbitrary",)),
        name="moe_route",
    )(x, g.reshape(1, d), rpad)


def _dispatch_body(pad_start_ref, pad_len_ref, dest_ref, x_ref, xs_ref, zrow_ref, sem, zsem, *, tm):
    @pl.when(pl.program_id(0) == 0)
    def _():
        zrow_ref[...] = jnp.zeros_like(zrow_ref)
        for e in range(N_EXPERTS):
            def zero_row(r):
                return pltpu.make_async_copy(zrow_ref.at[pl.ds(0, 1)], xs_ref.at[pl.ds(pad_start_ref[e] + r, 1)], zsem)

            def zissue(r, carry):
                zero_row(r).start()
                return carry

            def zwait(r, carry):
                zero_row(r).wait()
                return carry

            lax.fori_loop(0, pad_len_ref[e], zissue, 0)
            lax.fori_loop(0, pad_len_ref[e], zwait, 0)

    def issue(r, carry):
        for kk in range(TOP_K):
            dst = dest_ref[r * TOP_K + kk]
            pltpu.make_async_copy(x_ref.at[pl.ds(r, 1)], xs_ref.at[pl.ds(dst, 1)], sem).start()
        return carry

    lax.fori_loop(0, tm, issue, 0, unroll=8)
    for kk in range(TOP_K):
        pltpu.make_async_copy(x_ref, xs_ref.at[pl.ds(0, tm)], sem).wait()


def moe_dispatch(x, dest_flat, pad_start, pad_len, p):
    t, d = x.shape
    tm = _tile(t, 512)
    body = functools.partial(_dispatch_body, tm=tm)
    grid_spec = pltpu.PrefetchScalarGridSpec(
        num_scalar_prefetch=2,
        grid=(t // tm,),
        in_specs=[
            pl.BlockSpec((tm * TOP_K,), lambda i, ps, pn: (i,), memory_space=pltpu.SMEM),
            pl.BlockSpec((tm, d), lambda i, ps, pn: (i, 0)),
        ],
        out_specs=pl.BlockSpec(memory_space=pl.ANY),
        scratch_shapes=[pltpu.VMEM((8, d), F32), pltpu.SemaphoreType.DMA(()), pltpu.SemaphoreType.DMA(())],
    )
    return pl.pallas_call(
        body,
        grid_spec=grid_spec,
        out_shape=jax.ShapeDtypeStruct((p, d), F32),
        compiler_params=_cparams(("arbitrary",)),
        name="moe_dispatch",
    )(pad_start, pad_len, dest_flat, x)


def _expert_body(te_ref, nt_ref, xs_ref, g_ref, wgu_ref, wd_ref, ys_ref, a_ref, *, dff, tf):
    i = pl.program_id(0)

    @pl.when(i < nt_ref[0])
    def _():
        h = _rms(xs_ref[...], g_ref[...]).astype(BF16)
        ys_ref[...] = _swiglu(h, lambda lo, hi: wgu_ref[0, :, lo:hi], wd_ref[0], a_ref, dff, tf)

    @pl.when(i >= nt_ref[0])
    def _():
        ys_ref[...] = jnp.zeros_like(ys_ref)


def moe_experts(xs, g, tile_expert, num_tiles, w_gate_up, w_down, tr):
    p, d = xs.shape
    dff = w_down.shape[1]
    tf = _ff_chunk(dff, 512)
    body = functools.partial(_expert_body, dff=dff, tf=tf)

    def row(i, nt):
        return jnp.minimum(i, nt[0] - 1)

    grid_spec = pltpu.PrefetchScalarGridSpec(
        num_scalar_prefetch=2,
        grid=(p // tr,),
        in_specs=[
            pl.BlockSpec((tr, d), lambda i, te, nt: (row(i, nt), 0)),
            pl.BlockSpec((1, d), lambda i, te, nt: (0, 0)),
            pl.BlockSpec((1, d, 2 * dff), lambda i, te, nt: (te[row(i, nt)], 0, 0), pipeline_mode=pl.Buffered(1)),
            pl.BlockSpec((1, dff, d), lambda i, te, nt: (te[row(i, nt)], 0, 0), pipeline_mode=pl.Buffered(1)),
        ],
        out_specs=pl.BlockSpec((tr, d), lambda i, te, nt: (i, 0)),
        scratch_shapes=[pltpu.VMEM((tr, dff), BF16)],
    )
    return pl.pallas_call(
        body,
        grid_spec=grid_spec,
        out_shape=jax.ShapeDtypeStruct((p, d), F32),
        compiler_params=_cparams(("arbitrary",)),
        name="moe_experts",
    )(tile_expert, num_tiles, xs, g.reshape(1, d), w_gate_up.astype(BF16), w_down.astype(BF16))


def _combine_body(dest_ref, x_ref, w_ref, gf_ref, ys_ref, o_ref, buf_ref, sem, *, tm, final_norm):
    def issue(r, carry):
        for kk in range(TOP_K):
            src = dest_ref[r * TOP_K + kk]
            pltpu.make_async_copy(ys_ref.at[pl.ds(src, 1)], buf_ref.at[kk, pl.ds(r, 1)], sem).start()
        return carry

    lax.fori_loop(0, tm, issue, 0, unroll=8)
    for kk in range(TOP_K):
        pltpu.make_async_copy(ys_ref.at[pl.ds(0, tm)], buf_ref.at[kk], sem).wait()
    w = w_ref[...]
    out = x_ref[...] + (w[:, 0:1] * buf_ref[0] + w[:, 1:2] * buf_ref[1])
    if final_norm:
        out = _rms(out, gf_ref[...])
    o_ref[...] = out


def moe_combine(x, wgt, dest_flat, ys, g_final, final_norm):
    t, d = x.shape
    tm = _tile(t, 512)
    body = functools.partial(_combine_body, tm=tm, final_norm=final_norm)
    return pl.pallas_call(
        body,
        grid=(t // tm,),
        in_specs=[
            pl.BlockSpec((tm * TOP_K,), lambda i: (i,), memory_space=pltpu.SMEM),
            pl.BlockSpec((tm, d), lambda i: (i, 0)),
            pl.BlockSpec((tm, TOP_K), lambda i: (i, 0)),
            pl.BlockSpec((1, d), lambda i: (0, 0)),
            pl.BlockSpec(memory_space=pl.ANY),
        ],
        out_specs=pl.BlockSpec((tm, d), lambda i: (i, 0)),
        out_shape=jax.ShapeDtypeStruct((t, d), F32),
        scratch_shapes=[pltpu.VMEM((TOP_K, tm, d), F32), pltpu.SemaphoreType.DMA(())],
        compiler_params=_cparams(("arbitrary",)),
        name="moe_combine",
    )(dest_flat, x, wgt, g_final.reshape(1, d), ys)


def moe_layer(x, g, router, w_gate_up, w_down, g_final, final_norm):
    t, d = x.shape
    tr = min(512, t)
    eid, rank, wgt, cnt = moe_route(x, g, router)
    counts = cnt[0, :N_EXPERTS]
    padded = ((counts + tr - 1) // tr) * tr
    ends = jnp.cumsum(padded)
    base = ends - padded
    dest = (base[eid] + rank).reshape(-1).astype(jnp.int32)
    ntiles_max = (t * TOP_K) // tr + N_EXPERTS
    num_tiles = (ends[-1] // tr).astype(jnp.int32).reshape(1)
    tile_start = jnp.arange(ntiles_max, dtype=jnp.int32) * tr
    tile_expert = jnp.minimum(jnp.sum(tile_start[:, None] >= ends[None, :], axis=1), N_EXPERTS - 1).astype(jnp.int32)
    p = ntiles_max * tr
    pad_len = (padded - counts).at[N_EXPERTS - 1].add(p - ends[-1])
    xs = moe_dispatch(x, dest, (base + counts).astype(jnp.int32), pad_len.astype(jnp.int32), p)
    ys = moe_experts(xs, g, tile_expert, num_tiles, w_gate_up, w_down, tr)
    return moe_combine(x, wgt, dest, ys, g_final, final_norm)


def kernel(x, positions, norm_mix, norm_ffn, norm_final, conv_w_in, conv_w, conv_w_out, mla_w_down, mla_q_norm,
           mla_w_uq, mla_kv_norm, mla_w_ukv, mla_w_o, swa_w_qkv, swa_b_qkv, swa_sinks, swa_w_o, swa_b_o,
           ffn_w_gate_up, ffn_w_down, moe_router, moe_w_gate_up, moe_w_down):
    batch, seq, d = x.shape
    depth = norm_mix.shape[0]
    t = batch * seq
    xf = x.reshape(t, d)
    pos_col = positions.reshape(t, 1).astype(F32)
    pos_row = positions.reshape(t // WINDOW, 1, WINDOW).astype(F32)
    zeros_d = jnp.zeros((d,), F32)
    for i in range(depth):
        kind = i % N_MIXERS
        j = i // N_MIXERS
        if kind == 0:
            xf = conv_mixer(xf, norm_mix[i], conv_w_in[j], conv_w[j], conv_w_out[j], seq)
        elif kind == 1:
            q, k, v = mla_project(xf, pos_col, norm_mix[i], mla_w_down[j], mla_q_norm[j], mla_w_uq[j],
                                  mla_kv_norm[j], mla_w_ukv[j])
            o = mla_flash(q, k, v, batch, seq)
            xf = linear_residual(xf, o, mla_w_o[j], zeros_d)
        else:
            wq, bq = swa_qkv_weights(swa_w_qkv[j], swa_b_qkv[j])
            qkv = rms_linear(xf, norm_mix[i], wq, bq)
            o = swa_attention(qkv, swa_sinks[j], pos_col, pos_row, batch, seq)
            xf = linear_residual(xf, o, swa_w_o[j], swa_b_o[j])
        f_idx = i // 2
        last = i == depth - 1
        if i % 2 == 0:
            xf = dense_ffn(xf, norm_ffn[i], ffn_w_gate_up[f_idx], ffn_w_down[f_idx])
            if last:
                xf = final_rmsnorm(xf, norm_final)
        else:
            xf = moe_layer(xf, norm_ffn[i], moe_router[f_idx], moe_w_gate_up[f_idx], moe_w_down[f_idx],
                           norm_final, last)
    return xf.reshape(batch, seq, d)


def _final_norm_body(x_ref, g_ref, o_ref):
    o_ref[...] = _rms(x_ref[...], g_ref[...])


def final_rmsnorm(x, g):
    t, d = x.shape
    tm = _tile(t, 1024)
    return pl.pallas_call(
        _final_norm_body,
        grid=(t // tm,),
        in_specs=[pl.BlockSpec((tm, d), lambda i: (i, 0)), _const_spec((1, d))],
        out_specs=pl.BlockSpec((tm, d), lambda i: (i, 0)),
        out_shape=jax.ShapeDtypeStruct((t, d), F32),
        compiler_params=_cparams(("arbitrary",)),
        name="final_rmsnorm",
    )(x, g.reshape(1, d))
```

```python
import functools

import jax
import jax.numpy as jnp
from jax import lax
from jax.experimental import pallas as pl
from jax.experimental.pallas import tpu as pltpu

F32 = jnp.float32
BF16 = jnp.bfloat16

RMS_EPS = 1e-6
NEG_INF = -1e30
CONV_WIDTH = 3
MLA_HEADS = 8
MLA_Q_LORA = 384
MLA_KV_LORA = 256
MLA_NOPE = 128
MLA_ROPE = 64
MLA_V = 128
MLA_QK_PAD = 256
LOG2E = 1.4426950408889634
MLA_Q_SCALE = float((MLA_NOPE + MLA_ROPE) ** -0.5 * LOG2E)
ROPE_THETA = 10000.0
SWA_Q_HEADS = 16
SWA_KV_HEADS = 2
SWA_GROUP = SWA_Q_HEADS // SWA_KV_HEADS
SWA_HEAD_DIM = 64
WINDOW = 128
SWA_Q_SCALE = float(SWA_HEAD_DIM ** -0.5 * LOG2E)
SWA_MASKED_DIST = 1e32
N_EXPERTS = 8
TOP_K = 2
N_MIXERS = 3

VMEM_LIMIT = 56 * 1024 * 1024


def _cparams(sem):
    return pltpu.CompilerParams(dimension_semantics=sem, vmem_limit_bytes=VMEM_LIMIT)


def _rms(xf, g):
    ms = jnp.mean(xf * xf, axis=-1, keepdims=True)
    return xf * lax.rsqrt(ms + RMS_EPS) * g


def _const_spec(shape):
    nd = len(shape)
    return pl.BlockSpec(shape, lambda *_: (0,) * nd)


def _resident_spec(shape):
    nd = len(shape)
    return pl.BlockSpec(shape, lambda *_: (0,) * nd, pipeline_mode=pl.Buffered(1))


def _tile(n, pref):
    t = min(n, pref)
    assert n % t == 0, (n, t)
    return t


def _conv_body(x_ref, g_ref, win_ref, wc_ref, wout_ref, o_ref, vpad_ref, *, tm, d, tiles_per_seq):
    i = pl.program_id(0)
    x = x_ref[...]
    h = _rms(x, g_ref[...]).astype(BF16)
    bg = jnp.dot(h, win_ref[:, 0:d], preferred_element_type=F32)
    cg = jnp.dot(h, win_ref[:, d:2 * d], preferred_element_type=F32)
    u = jnp.dot(h, win_ref[:, 2 * d:3 * d], preferred_element_type=F32)
    v = cg * u

    @pl.when(i % tiles_per_seq == 0)
    def _():
        vpad_ref[0:8, :] = jnp.zeros((8, d), F32)

    vpad_ref[8:8 + tm, :] = v
    conv = vpad_ref[6:6 + tm, :] * wc_ref[0:1, :]
    conv = conv + vpad_ref[7:7 + tm, :] * wc_ref[1:2, :]
    conv = conv + v * wc_ref[2:3, :]
    vpad_ref[0:8, :] = vpad_ref[tm:tm + 8, :]
    y = (bg * conv).astype(BF16)
    o_ref[...] = x + jnp.dot(y, wout_ref[...], preferred_element_type=F32)


def conv_mixer(x, g, w_in, w_conv, w_out, seq):
    t, d = x.shape
    tm = _tile(seq, 512)
    body = functools.partial(_conv_body, tm=tm, d=d, tiles_per_seq=seq // tm)
    return pl.pallas_call(
        body,
        grid=(t // tm,),
        in_specs=[
            pl.BlockSpec((tm, d), lambda i: (i, 0)),
            _const_spec((1, d)),
            _resident_spec((d, 3 * d)),
            _const_spec((CONV_WIDTH, d)),
            _resident_spec((d, d)),
        ],
        out_specs=pl.BlockSpec((tm, d), lambda i: (i, 0)),
        out_shape=jax.ShapeDtypeStruct((t, d), F32),
        scratch_shapes=[pltpu.VMEM((tm + 8, d), F32)],
        compiler_params=_cparams(("arbitrary",)),
        name="conv_mixer",
    )(x, g.reshape(1, d), w_in.astype(BF16), w_conv, w_out.astype(BF16))


def _swiglu(h, wgu_cols, wd, a_ref, dff, tf):
    def gate_up(c):
        gate = jnp.dot(h, wgu_cols(c * tf, (c + 1) * tf), preferred_element_type=F32)
        up = jnp.dot(h, wgu_cols(dff + c * tf, dff + (c + 1) * tf), preferred_element_type=F32)
        return gate, up

    nxt = gate_up(0)
    for c in range(dff // tf):
        gate, up = nxt
        if c + 1 < dff // tf:
            nxt = gate_up(c + 1)
        a_ref[:, c * tf:(c + 1) * tf] = (gate * jax.nn.sigmoid(gate) * up).astype(BF16)
    return jnp.dot(a_ref[...], wd, preferred_element_type=F32)


def _mixer_out(x_ref, proj_refs):
    if not proj_refs:
        return x_ref[...]
    a_ref, w_ref, b_ref = proj_refs
    return (x_ref[...] + b_ref[...]) + jnp.dot(a_ref[...], w_ref[...], preferred_element_type=F32)


def _proj_specs(proj, tm, d):
    if proj is None:
        return [], []
    a, w, b = proj
    k = a.shape[1]
    specs = [pl.BlockSpec((tm, k), lambda i: (i, 0)), _resident_spec((k, d)), _const_spec((1, d))]
    return [a, w.astype(BF16), b.reshape(1, d).astype(F32)], specs


def _ffn_body(*refs, dff, tf, fused_proj):
    x_ref, *proj_refs = refs[:4] if fused_proj else refs[:1]
    g_ref, wgu_ref, wd_ref, o_ref, a_ref = refs[4:] if fused_proj else refs[1:]
    x = _mixer_out(x_ref, proj_refs)
    h = _rms(x, g_ref[...]).astype(BF16)
    o_ref[...] = x + _swiglu(h, lambda lo, hi: wgu_ref[:, lo:hi], wd_ref[...], a_ref, dff, tf)


def _ff_chunk(dff, pref):
    best = None
    for c in range(128, dff + 1, 128):
        if dff % c == 0 and c <= pref:
            best = c
    return best if best is not None else dff


def dense_ffn(x, g, w_gate_up, w_down, proj=None):
    t, d = x.shape
    dff = w_down.shape[0]
    tm = _tile(t, 512)
    tf = _ff_chunk(dff, 768)
    body = functools.partial(_ffn_body, dff=dff, tf=tf, fused_proj=proj is not None)
    proj_args, proj_specs = _proj_specs(proj, tm, d)
    return pl.pallas_call(
        body,
        grid=(t // tm,),
        in_specs=[pl.BlockSpec((tm, d), lambda i: (i, 0))] + proj_specs + [
            _const_spec((1, d)),
            _resident_spec((d, 2 * dff)),
            _resident_spec((dff, d)),
        ],
        out_specs=pl.BlockSpec((tm, d), lambda i: (i, 0)),
        out_shape=jax.ShapeDtypeStruct((t, d), F32),
        scratch_shapes=[pltpu.VMEM((tm, dff), BF16)],
        compiler_params=_cparams(("arbitrary",)),
        name="dense_ffn",
    )(x, *proj_args, g.reshape(1, d), w_gate_up.astype(BF16), w_down.astype(BF16))


def _rmslin_body(x_ref, g_ref, w_ref, b_ref, o_ref):
    h = _rms(x_ref[...], g_ref[...]).astype(BF16)
    o_ref[...] = (jnp.dot(h, w_ref[...], preferred_element_type=F32) + b_ref[...]).astype(o_ref.dtype)


def rms_linear(x, g, w, b):
    t, d = x.shape
    n = w.shape[1]
    tm = _tile(t, 512)
    return pl.pallas_call(
        _rmslin_body,
        grid=(t // tm,),
        in_specs=[
            pl.BlockSpec((tm, d), lambda i: (i, 0)),
            _const_spec((1, d)),
            _const_spec((d, n)),
            _const_spec((1, n)),
        ],
        out_specs=pl.BlockSpec((tm, n), lambda i: (i, 0)),
        out_shape=jax.ShapeDtypeStruct((t, n), BF16),
        compiler_params=_cparams(("arbitrary",)),
        name="rms_linear",
    )(x, g.reshape(1, d), w.astype(BF16), b.reshape(1, n).astype(F32))


def _mla_proj_body(x_ref, pos_ref, g_ref, wd_ref, qn_ref, kvn_ref, wqa_ref, wqb_ref, wkv_ref, frq_ref,
                   q_ref, k_ref, v_ref):
    h = _rms(x_ref[...], g_ref[...]).astype(BF16)
    down = jnp.dot(h, wd_ref[...], preferred_element_type=F32)
    c_q = down[:, 0:MLA_Q_LORA]
    c_kv = down[:, MLA_Q_LORA:MLA_Q_LORA + MLA_KV_LORA]
    o = MLA_Q_LORA + MLA_KV_LORA
    kpe = down[:, o:o + 128]
    kpe_sw = down[:, o + 128:o + 256]
    hq = _rms(c_q, qn_ref[...]).astype(BF16)
    hkv = _rms(c_kv, kvn_ref[...]).astype(BF16)
    qa = jnp.dot(hq, wqa_ref[...], preferred_element_type=F32)
    qb = jnp.dot(hq, wqb_ref[...], preferred_element_type=F32)
    kv = jnp.dot(hkv, wkv_ref[...], preferred_element_type=F32)
    ang = pos_ref[...] * frq_ref[0:1, :]
    cos = jnp.cos(ang)
    sin = jnp.sin(ang) * frq_ref[1:2, :]
    kpe_r = (kpe * cos + kpe_sw * sin).astype(BF16)
    ones_col = jnp.where(lax.broadcasted_iota(jnp.int32, cos.shape, 1) == 0, 1.0, 0.0).astype(BF16)
    for hd in range(MLA_HEADS):
        qo = hd * MLA_QK_PAD
        q_ref[hd, :, 0:MLA_NOPE] = qa[:, qo:qo + MLA_NOPE].astype(BF16)
        qpe = qa[:, qo + MLA_NOPE:qo + MLA_QK_PAD] * cos + qb[:, hd * 128:(hd + 1) * 128] * sin
        q_ref[hd, :, MLA_NOPE:MLA_QK_PAD] = qpe.astype(BF16)
        ko = hd * (MLA_NOPE + MLA_V)
        k_ref[hd, :, 0:MLA_NOPE] = kv[:, ko:ko + MLA_NOPE].astype(BF16)
        k_ref[hd, :, MLA_NOPE:MLA_QK_PAD] = kpe_r
        v_ref[hd, :, 0:MLA_V] = kv[:, ko + MLA_NOPE:ko + MLA_NOPE + MLA_V].astype(BF16)
        v_ref[hd, :, MLA_V:2 * MLA_V] = ones_col


def _rope_swap(w):
    half = MLA_ROPE // 2
    return jnp.concatenate([w[..., half:], w[..., :half]], axis=-1)


def mla_project(x, pos_col, g, w_down, q_norm, w_uq, kv_norm, w_ukv):
    t, d = x.shape
    tm = _tile(t, 512)
    hn = MLA_HEADS
    z64 = jnp.zeros((d, 64), F32)
    o = MLA_Q_LORA + MLA_KV_LORA
    kpe_w = w_down[:, o:o + MLA_ROPE]
    wd = jnp.concatenate([w_down[:, :o], kpe_w, z64, _rope_swap(kpe_w), z64], axis=1).astype(BF16)
    wq = (w_uq * MLA_Q_SCALE).reshape(MLA_Q_LORA, hn, MLA_NOPE + MLA_ROPE)
    zq = jnp.zeros((MLA_Q_LORA, hn, 64), F32)
    wqa = jnp.concatenate([wq, zq], axis=-1).reshape(MLA_Q_LORA, hn * MLA_QK_PAD).astype(BF16)
    wqb = jnp.concatenate([_rope_swap(wq[..., MLA_NOPE:]), zq], axis=-1).reshape(MLA_Q_LORA, hn * 128).astype(BF16)
    inv = ROPE_THETA ** (-jnp.arange(0, MLA_ROPE, 2, dtype=F32) / MLA_ROPE)
    half = MLA_ROPE // 2
    frq = jnp.stack([
        jnp.concatenate([inv, inv, jnp.zeros((64,), F32)]),
        jnp.concatenate([-jnp.ones((half,), F32), jnp.ones((half,), F32), jnp.zeros((64,), F32)]),
    ])
    nd = wd.shape[1]
    outs = pl.pallas_call(
        _mla_proj_body,
        grid=(t // tm,),
        in_specs=[
            pl.BlockSpec((tm, d), lambda i: (i, 0)),
            pl.BlockSpec((tm, 1), lambda i: (i, 0)),
            _const_spec((1, d)),
            _resident_spec((d, nd)),
            _const_spec((1, MLA_Q_LORA)),
            _const_spec((1, MLA_KV_LORA)),
            _resident_spec((MLA_Q_LORA, hn * MLA_QK_PAD)),
            _resident_spec((MLA_Q_LORA, hn * 128)),
            _resident_spec((MLA_KV_LORA, hn * (MLA_NOPE + MLA_V))),
            _const_spec((2, 128)),
        ],
        out_specs=[
            pl.BlockSpec((hn, tm, MLA_QK_PAD), lambda i: (0, i, 0)),
            pl.BlockSpec((hn, tm, MLA_QK_PAD), lambda i: (0, i, 0)),
            pl.BlockSpec((hn, tm, 2 * MLA_V), lambda i: (0, i, 0)),
        ],
        out_shape=[
            jax.ShapeDtypeStruct((hn, t, MLA_QK_PAD), BF16),
            jax.ShapeDtypeStruct((hn, t, MLA_QK_PAD), BF16),
            jax.ShapeDtypeStruct((hn, t, 2 * MLA_V), BF16),
        ],
        compiler_params=_cparams(("arbitrary",)),
        name="mla_project",
    )(x, pos_col, g.reshape(1, d), wd, q_norm.reshape(1, -1), kv_norm.reshape(1, -1), wqa, wqb,
      w_ukv.astype(BF16), frq)
    return outs


def _flash_body(qi_ref, kj_ref, q_ref, k_ref, v_ref, o_ref, m_ref, acc_ref, *, tq, tqs, hp, dv):
    pair = pl.program_id(2)
    i = qi_ref[pair]
    j = kj_ref[pair]

    @pl.when(j == 0)
    def _():
        m_ref[...] = jnp.full_like(m_ref, NEG_INF)
        acc_ref[...] = jnp.zeros_like(acc_ref)

    def step(diagonal):
        nsub = tq // tqs
        units = [(h, r) for h in range(hp) for r in range(nsub)]
        causal = (lax.broadcasted_iota(jnp.int32, (tqs, tqs), 1) <= lax.broadcasted_iota(jnp.int32, (tqs, tqs), 0))

        def scores(u):
            h, r = u
            nk = (r + 1) * tqs if diagonal else tq
            return lax.dot_general(q_ref[h, r * tqs:(r + 1) * tqs, :], k_ref[h, 0:nk, :], (((1,), (1,)), ((), ())),
                                   preferred_element_type=F32)

        s_next = scores(units[0])
        for n, (h, r) in enumerate(units):
            rows = slice(r * tqs, (r + 1) * tqs)
            nk = (r + 1) * tqs if diagonal else tq
            s = s_next
            if n + 1 < len(units):
                s_next = scores(units[n + 1])
            if diagonal:
                s_diag = jnp.where(causal, s[:, nk - tqs:nk], NEG_INF)
                s = s_diag if r == 0 else jnp.concatenate([s[:, 0:nk - tqs], s_diag], axis=1)
            m_prev = m_ref[h, rows, :]
            m_new = jnp.maximum(m_prev, jnp.max(s, axis=-1, keepdims=True))
            alpha = jnp.exp2(m_prev - m_new)
            p = jnp.exp2(s - m_new)
            pv = jnp.dot(p.astype(BF16), v_ref[h, 0:nk, :], preferred_element_type=F32)
            acc_ref[h, rows, :] = alpha * acc_ref[h, rows, :] + pv
            m_ref[h, rows, :] = m_new

    @pl.when(j < i)
    def _():
        step(False)

    @pl.when(j == i)
    def _():
        step(True)
        for h in range(hp):
            o_ref[:, h * dv:(h + 1) * dv] = (acc_ref[h, :, 0:dv] / acc_ref[h, :, dv:dv + 1]).astype(o_ref.dtype)


def mla_flash(q, k, v, batch, seq):
    hn, t, dq = q.shape
    dvp = v.shape[2]
    dv = MLA_V
    tq = tk = _tile(seq, 1024)
    nq = nk = seq // tq
    hp = 4
    body = functools.partial(_flash_body, tq=tq, tqs=_tile(tq, 256), hp=hp, dv=dv)
    pairs = [(i, j) for i in range(nq) for j in range(i + 1)]
    qi = jnp.asarray([p[0] for p in pairs], jnp.int32)
    kj = jnp.asarray([p[1] for p in pairs], jnp.int32)
    grid_spec = pltpu.PrefetchScalarGridSpec(
        num_scalar_prefetch=2,
        grid=(batch, hn // hp, len(pairs)),
        in_specs=[
            pl.BlockSpec((hp, tq, dq), lambda b, h, p, qi, kj: (h, b * nq + qi[p], 0)),
            pl.BlockSpec((hp, tk, dq), lambda b, h, p, qi, kj: (h, b * nk + kj[p], 0)),
            pl.BlockSpec((hp, tk, dvp), lambda b, h, p, qi, kj: (h, b * nk + kj[p], 0)),
        ],
        out_specs=pl.BlockSpec((tq, hp * dv), lambda b, h, p, qi, kj: (b * nq + qi[p], h)),
        scratch_shapes=[pltpu.VMEM((hp, tq, 1), F32), pltpu.VMEM((hp, tq, dvp), F32)],
    )
    return pl.pallas_call(
        body,
        grid_spec=grid_spec,
        out_shape=jax.ShapeDtypeStruct((t, hn * dv), BF16),
        compiler_params=_cparams(("arbitrary", "arbitrary", "arbitrary")),
        name="mla_flash",
    )(qi, kj, q, k, v)


def _alibi_slopes():
    return [2.0 ** (-8.0 * (h + 1) / SWA_Q_HEADS) for h in range(SWA_Q_HEADS)]


def _swa_body(sink_ref, q_ref, kc_ref, kp_ref, vc_ref, vp_ref, pq_ref, pkc_ref, pkp_ref, o_ref, *, steps_per_seq, nblk):
    w = WINDOW
    seq_start = pl.program_id(0) % steps_per_seq == 0
    for r in range(nblk):
        cur = slice(r * w, (r + 1) * w)
        prv = slice((r - 1) * w, r * w)
        _swa_block(
            sink_ref, q_ref.at[cur], kc_ref.at[cur], kp_ref if r == 0 else kc_ref.at[prv],
            vc_ref.at[cur], vp_ref if r == 0 else vc_ref.at[prv], pq_ref[cur, :], pkc_ref[r],
            pkp_ref[0] if r == 0 else pkc_ref[r - 1], o_ref.at[cur],
            has_prev=jnp.logical_not(seq_start) if r == 0 else True)


def _swa_block(sink_ref, q_ref, kc_ref, kp_ref, vc_ref, vp_ref, pq, pkc, pkp, o_ref, *, has_prev):
    w = WINDOW
    dist_c = jnp.abs(pq - pkc)
    dist_p = jnp.abs(pq - pkp)
    iq = lax.broadcasted_iota(jnp.int32, (w, w), 0)
    ik = lax.broadcasted_iota(jnp.int32, (w, w), 1)
    dist_c = jnp.where(ik <= iq, dist_c, SWA_MASKED_DIST)
    dist_p = jnp.where(jnp.logical_and(ik > iq, has_prev), dist_p, SWA_MASKED_DIST)
    lane = lax.broadcasted_iota(jnp.int32, (w, 2 * SWA_HEAD_DIM), 1)
    left = lane < SWA_HEAD_DIM
    slopes = _alibi_slopes()
    pairs = SWA_GROUP // 2
    for kh in range(SWA_KV_HEADS):
        kc = kc_ref[:, kh * 128:(kh + 1) * 128]
        kp = kp_ref[:, kh * 128:(kh + 1) * 128]
        vc_l = vc_ref[:, kh * 256:kh * 256 + 128]
        vc_r = vc_ref[:, kh * 256 + 128:kh * 256 + 256]
        vp_l = vp_ref[:, kh * 256:kh * 256 + 128]
        vp_r = vp_ref[:, kh * 256 + 128:kh * 256 + 256]
        zero = jnp.zeros((w, 128), BF16)
        rows = []
        for half in range(2):
            for p in range(pairs):
                qp = q_ref[:, (kh * pairs + p) * 128:(kh * pairs + p + 1) * 128]
                rows.append(jnp.where(left if half == 0 else jnp.logical_not(left), qp, zero))
        qs = jnp.concatenate(rows, axis=0)
        dn = (((1,), (1,)), ((), ()))
        s_c = lax.dot_general(qs, kc, dn, preferred_element_type=F32)
        s_p = lax.dot_general(qs, kp, dn, preferred_element_type=F32)
        pc_rows, pp_rows = [], []
        for half in range(2):
            for p in range(pairs):
                hd = kh * SWA_GROUP + 2 * p + half
                r0 = (half * pairs + p) * w
                sink = sink_ref[hd] * LOG2E
                a_c = s_c[r0:r0 + w] - (slopes[hd] * LOG2E) * dist_c
                a_p = s_p[r0:r0 + w] - (slopes[hd] * LOG2E) * dist_p
                m = jnp.maximum(jnp.max(jnp.maximum(a_c, a_p), axis=-1, keepdims=True), sink)
                e_c = jnp.exp2(a_c - m)
                e_p = jnp.exp2(a_p - m)
                den = jnp.sum(e_c + e_p, axis=-1, keepdims=True) + jnp.exp2(sink - m)
                inv = 1.0 / den
                pc_rows.append((e_c * inv).astype(BF16))
                pp_rows.append((e_p * inv).astype(BF16))
        hw = pairs * w
        pc = jnp.concatenate(pc_rows, axis=0)
        pp = jnp.concatenate(pp_rows, axis=0)
        o_pair = (jnp.dot(pc[0:hw], vc_l, preferred_element_type=F32)
                  + jnp.dot(pp[0:hw], vp_l, preferred_element_type=F32)
                  + jnp.dot(pc[hw:2 * hw], vc_r, preferred_element_type=F32)
                  + jnp.dot(pp[hw:2 * hw], vp_r, preferred_element_type=F32))
        for p in range(pairs):
            c0 = (kh * pairs + p) * 128
            o_ref[:, c0:c0 + 128] = o_pair[p * w:(p + 1) * w].astype(o_ref.dtype)


def swa_attention(qkv, sinks, pos_col, pos_row, batch, seq):
    t = qkv.shape[0]
    w = WINDOW
    nblk = 4 if (seq // w) % 4 == 0 else 1
    tq = nblk * w
    steps_per_seq = seq // tq
    nq = SWA_Q_HEADS * SWA_HEAD_DIM
    body = functools.partial(_swa_body, steps_per_seq=steps_per_seq, nblk=nblk)

    def prev(i):
        return jnp.where(i % steps_per_seq == 0, i * nblk, i * nblk - 1)

    kcol = (nq + 4 * 128) // 256
    return pl.pallas_call(
        body,
        grid=(t // tq,),
        in_specs=[
            pl.BlockSpec(memory_space=pltpu.SMEM),
            pl.BlockSpec((tq, nq), lambda i: (i, 0)),
            pl.BlockSpec((tq, 256), lambda i: (i, kcol)),
            pl.BlockSpec((w, 256), lambda i: (prev(i), kcol)),
            pl.BlockSpec((tq, 512), lambda i: (i, nq // 512)),
            pl.BlockSpec((w, 512), lambda i: (prev(i), nq // 512)),
            pl.BlockSpec((tq, 1), lambda i: (i, 0)),
            pl.BlockSpec((nblk, 1, w), lambda i: (i, 0, 0)),
            pl.BlockSpec((1, 1, w), lambda i: (prev(i), 0, 0)),
        ],
        out_specs=pl.BlockSpec((tq, nq), lambda i: (i, 0)),
        out_shape=jax.ShapeDtypeStruct((t, nq), BF16),
        compiler_params=_cparams(("arbitrary",)),
        name="swa_attention",
    )(sinks.astype(F32), qkv, qkv, qkv, qkv, qkv, pos_col, pos_row, pos_row)


def swa_qkv_weights(w_qkv, b_qkv):
    nq = SWA_Q_HEADS * SWA_HEAD_DIM
    nk = SWA_KV_HEADS * SWA_HEAD_DIM
    hd = SWA_HEAD_DIM

    def arrange(m):
        q = m[..., :nq] * SWA_Q_SCALE
        k = m[..., nq:nq + nk]
        v = m[..., nq + nk:]
        z = jnp.zeros(m.shape[:-1] + (hd,), m.dtype)
        parts = [q]
        for kh in range(SWA_KV_HEADS):
            vh = v[..., kh * hd:(kh + 1) * hd]
            parts += [vh, z, z, vh]
        for kh in range(SWA_KV_HEADS):
            kk = k[..., kh * hd:(kh + 1) * hd]
            parts += [kk, kk]
        return jnp.concatenate(parts, axis=-1)

    return arrange(w_qkv), arrange(b_qkv)


def _router_body(*refs, tm, fused_proj):
    x_ref, *proj_refs = refs[:4] if fused_proj else refs[:1]
    rest = refs[4:] if fused_proj else refs[1:]
    if fused_proj:
        g_ref, r_ref, x1_ref, eid_ref, rank_ref, wgt_ref, cnt_ref, run_ref = rest
    else:
        g_ref, r_ref, eid_ref, rank_ref, wgt_ref, cnt_ref, run_ref = rest
    i = pl.program_id(0)

    @pl.when(i == 0)
    def _():
        run_ref[...] = jnp.zeros_like(run_ref)

    x = _mixer_out(x_ref, proj_refs)
    if fused_proj:
        x1_ref[...] = x
    h = _rms(x, g_ref[...]).astype(BF16)
    logits = jnp.dot(h, r_ref[...], preferred_element_type=F32)
    lane = lax.broadcasted_iota(jnp.int32, logits.shape, 1)
    logits = jnp.where(lane < N_EXPERTS, logits, -jnp.inf)
    m1 = jnp.max(logits, axis=-1, keepdims=True)
    i1 = jnp.min(jnp.where(logits == m1, lane, 128), axis=-1, keepdims=True)
    rest = jnp.where(lane == i1, -jnp.inf, logits)
    m2 = jnp.max(rest, axis=-1, keepdims=True)
    i2 = jnp.min(jnp.where(rest == m2, lane, 128), axis=-1, keepdims=True)
    e2 = jnp.exp(m2 - m1)
    w1 = 1.0 / (1.0 + e2)
    w2 = e2 / (1.0 + e2)
    oh1 = (lane == i1)
    oh2 = (lane == i2)
    sel = jnp.where(jnp.logical_or(oh1, oh2), 1.0, 0.0).astype(BF16)
    r = lax.broadcasted_iota(jnp.int32, (tm, tm), 0)
    c = lax.broadcasted_iota(jnp.int32, (tm, tm), 1)
    tri = jnp.where(c < r, 1.0, 0.0).astype(BF16)
    before = jnp.dot(tri, sel, preferred_element_type=F32) + run_ref[...]
    rank1 = jnp.sum(jnp.where(oh1, before, 0.0), axis=-1, keepdims=True)
    rank2 = jnp.sum(jnp.where(oh2, before, 0.0), axis=-1, keepdims=True)
    run_ref[...] = run_ref[...] + jnp.sum(sel.astype(F32), axis=0, keepdims=True)
    eid_ref[...] = jnp.where(lane == 0, i1, jnp.where(lane == 1, i2, 0))[:, 0:TOP_K]
    rank_ref[...] = jnp.where(lane == 0, rank1, jnp.where(lane == 1, rank2, 0.0))[:, 0:TOP_K].astype(jnp.int32)
    wgt_ref[...] = jnp.where(lane == 0, w1, jnp.where(lane == 1, w2, 0.0))[:, 0:TOP_K]
    cnt_ref[...] = run_ref[...].astype(jnp.int32)


def moe_route(x, g, router, proj=None):
    t, d = x.shape
    tm = _tile(t, 512)
    rpad = jnp.zeros((d, 128), F32).at[:, :N_EXPERTS].set(router).astype(BF16)
    body = functools.partial(_router_body, tm=tm, fused_proj=proj is not None)
    proj_args, proj_specs = _proj_specs(proj, tm, d)
    x1_spec = [pl.BlockSpec((tm, d), lambda i: (i, 0))] if proj is not None else []
    x1_shape = [jax.ShapeDtypeStruct((t, d), F32)] if proj is not None else []
    outs = pl.pallas_call(
        body,
        grid=(t // tm,),
        in_specs=[pl.BlockSpec((tm, d), lambda i: (i, 0))] + proj_specs + [
            _const_spec((1, d)),
            _const_spec((d, 128)),
        ],
        out_specs=x1_spec + [
            pl.BlockSpec((tm, TOP_K), lambda i: (i, 0)),
            pl.BlockSpec((tm, TOP_K), lambda i: (i, 0)),
            pl.BlockSpec((tm, TOP_K), lambda i: (i, 0)),
            _const_spec((1, 128)),
        ],
        out_shape=x1_shape + [
            jax.ShapeDtypeStruct((t, TOP_K), jnp.int32),
            jax.ShapeDtypeStruct((t, TOP_K), jnp.int32),
            jax.ShapeDtypeStruct((t, TOP_K), F32),
            jax.ShapeDtypeStruct((1, 128), jnp.int32),
        ],
        scratch_shapes=[pltpu.VMEM((1, 128), F32)],
        compiler_params=_cparams(("arbitrary",)),
        name="moe_route",
    )(x, *proj_args, g.reshape(1, d), rpad)
    return outs if proj is not None else [x] + list(outs)


def _dispatch_body(pad_start_ref, pad_len_ref, dest_ref, x_ref, xs_ref, zrow_ref, sem, zsem, *, tm):
    @pl.when(pl.program_id(0) == 0)
    def _():
        zrow_ref[...] = jnp.zeros_like(zrow_ref)
        for e in range(N_EXPERTS):
            def zero_row(r):
                return pltpu.make_async_copy(zrow_ref.at[pl.ds(0, 1)], xs_ref.at[pl.ds(pad_start_ref[e] + r, 1)], zsem)

            def zissue(r, carry):
                zero_row(r).start()
                return carry

            def zwait(r, carry):
                zero_row(r).wait()
                return carry

            lax.fori_loop(0, pad_len_ref[e], zissue, 0)
            lax.fori_loop(0, pad_len_ref[e], zwait, 0)

    def issue(r, carry):
        for kk in range(TOP_K):
            dst = dest_ref[r * TOP_K + kk]
            pltpu.make_async_copy(x_ref.at[pl.ds(r, 1)], xs_ref.at[pl.ds(dst, 1)], sem).start()
        return carry

    lax.fori_loop(0, tm, issue, 0, unroll=8)
    for kk in range(TOP_K):
        pltpu.make_async_copy(x_ref, xs_ref.at[pl.ds(0, tm)], sem).wait()


def moe_dispatch(x, dest_flat, pad_start, pad_len, p):
    t, d = x.shape
    tm = _tile(t, 512)
    body = functools.partial(_dispatch_body, tm=tm)
    grid_spec = pltpu.PrefetchScalarGridSpec(
        num_scalar_prefetch=2,
        grid=(t // tm,),
        in_specs=[
            pl.BlockSpec((tm * TOP_K,), lambda i, ps, pn: (i,), memory_space=pltpu.SMEM),
            pl.BlockSpec((tm, d), lambda i, ps, pn: (i, 0)),
        ],
        out_specs=pl.BlockSpec(memory_space=pl.ANY),
        scratch_shapes=[pltpu.VMEM((8, d), F32), pltpu.SemaphoreType.DMA(()), pltpu.SemaphoreType.DMA(())],
    )
    return pl.pallas_call(
        body,
        grid_spec=grid_spec,
        out_shape=jax.ShapeDtypeStruct((p, d), F32),
        compiler_params=_cparams(("arbitrary",)),
        name="moe_dispatch",
    )(pad_start, pad_len, dest_flat, x)


def _expert_body(te_ref, nt_ref, xs_ref, g_ref, wgu_ref, wd_ref, ys_ref, a_ref, *, dff, tf):
    i = pl.program_id(0)

    @pl.when(i < nt_ref[0])
    def _():
        h = _rms(xs_ref[...], g_ref[...]).astype(BF16)
        ys_ref[...] = _swiglu(h, lambda lo, hi: wgu_ref[0, :, lo:hi], wd_ref[0], a_ref, dff, tf)

    @pl.when(i >= nt_ref[0])
    def _():
        ys_ref[...] = jnp.zeros_like(ys_ref)


def moe_experts(xs, g, tile_expert, num_tiles, w_gate_up, w_down, tr):
    p, d = xs.shape
    dff = w_down.shape[1]
    tf = _ff_chunk(dff, 512)
    body = functools.partial(_expert_body, dff=dff, tf=tf)

    def row(i, nt):
        return jnp.minimum(i, nt[0] - 1)

    grid_spec = pltpu.PrefetchScalarGridSpec(
        num_scalar_prefetch=2,
        grid=(p // tr,),
        in_specs=[
            pl.BlockSpec((tr, d), lambda i, te, nt: (row(i, nt), 0)),
            pl.BlockSpec((1, d), lambda i, te, nt: (0, 0)),
            pl.BlockSpec((1, d, 2 * dff), lambda i, te, nt: (te[row(i, nt)], 0, 0), pipeline_mode=pl.Buffered(1)),
            pl.BlockSpec((1, dff, d), lambda i, te, nt: (te[row(i, nt)], 0, 0), pipeline_mode=pl.Buffered(1)),
        ],
        out_specs=pl.BlockSpec((tr, d), lambda i, te, nt: (i, 0)),
        scratch_shapes=[pltpu.VMEM((tr, dff), BF16)],
    )
    return pl.pallas_call(
        body,
        grid_spec=grid_spec,
        out_shape=jax.ShapeDtypeStruct((p, d), F32),
        compiler_params=_cparams(("arbitrary",)),
        name="moe_experts",
    )(tile_expert, num_tiles, xs, g.reshape(1, d), w_gate_up.astype(BF16), w_down.astype(BF16))


def _combine_body(dest_ref, dest_next_ref, x_ref, w_ref, gf_ref, ys_ref, o_ref, buf_ref, sems, *, tm, final_norm):
    i = pl.program_id(0)
    slot = i % 2

    def gather(d_ref, s):
        def issue(r, carry):
            for kk in range(TOP_K):
                src = d_ref[r * TOP_K + kk]
                pltpu.make_async_copy(ys_ref.at[pl.ds(src, 1)], buf_ref.at[s, kk, pl.ds(r, 1)], sems.at[s]).start()
            return carry

        lax.fori_loop(0, tm, issue, 0, unroll=8)

    @pl.when(i == 0)
    def _():
        gather(dest_ref, 0)

    @pl.when(i + 1 < pl.num_programs(0))
    def _():
        gather(dest_next_ref, 1 - slot)

    for kk in range(TOP_K):
        pltpu.make_async_copy(ys_ref.at[pl.ds(0, tm)], buf_ref.at[slot, kk], sems.at[slot]).wait()
    w = w_ref[...]
    out = x_ref[...] + (w[:, 0:1] * buf_ref[slot, 0] + w[:, 1:2] * buf_ref[slot, 1])
    if final_norm:
        out = _rms(out, gf_ref[...])
    o_ref[...] = out


def moe_combine(x, wgt, dest_flat, ys, g_final, final_norm):
    t, d = x.shape
    tm = _tile(t, 512)
    body = functools.partial(_combine_body, tm=tm, final_norm=final_norm)
    nsteps = t // tm
    return pl.pallas_call(
        body,
        grid=(nsteps,),
        in_specs=[
            pl.BlockSpec((tm * TOP_K,), lambda i: (i,), memory_space=pltpu.SMEM),
            pl.BlockSpec((tm * TOP_K,), lambda i: (jnp.minimum(i + 1, nsteps - 1),), memory_space=pltpu.SMEM),
            pl.BlockSpec((tm, d), lambda i: (i, 0)),
            pl.BlockSpec((tm, TOP_K), lambda i: (i, 0)),
            pl.BlockSpec((1, d), lambda i: (0, 0)),
            pl.BlockSpec(memory_space=pl.ANY),
        ],
        out_specs=pl.BlockSpec((tm, d), lambda i: (i, 0)),
        out_shape=jax.ShapeDtypeStruct((t, d), F32),
        scratch_shapes=[pltpu.VMEM((2, TOP_K, tm, d), F32), pltpu.SemaphoreType.DMA((2,))],
        compiler_params=_cparams(("arbitrary",)),
        name="moe_combine",
    )(dest_flat, dest_flat, x, wgt, g_final.reshape(1, d), ys)


def moe_layer(x, g, router, w_gate_up, w_down, g_final, final_norm, proj=None):
    t, d = x.shape
    tr = min(512, t)
    x, eid, rank, wgt, cnt = moe_route(x, g, router, proj)
    counts = cnt[0, :N_EXPERTS]
    padded = ((counts + tr - 1) // tr) * tr
    ends = jnp.cumsum(padded)
    base = ends - padded
    dest = (base[eid] + rank).reshape(-1).astype(jnp.int32)
    ntiles_max = (t * TOP_K) // tr + N_EXPERTS
    num_tiles = (ends[-1] // tr).astype(jnp.int32).reshape(1)
    tile_start = jnp.arange(ntiles_max, dtype=jnp.int32) * tr
    tile_expert = jnp.minimum(jnp.sum(tile_start[:, None] >= ends[None, :], axis=1), N_EXPERTS - 1).astype(jnp.int32)
    p = ntiles_max * tr
    pad_len = (padded - counts).at[N_EXPERTS - 1].add(p - ends[-1])
    xs = moe_dispatch(x, dest, (base + counts).astype(jnp.int32), pad_len.astype(jnp.int32), p)
    ys = moe_experts(xs, g, tile_expert, num_tiles, w_gate_up, w_down, tr)
    return moe_combine(x, wgt, dest, ys, g_final, final_norm)


def kernel(x, positions, norm_mix, norm_ffn, norm_final, conv_w_in, conv_w, conv_w_out, mla_w_down, mla_q_norm,
           mla_w_uq, mla_kv_norm, mla_w_ukv, mla_w_o, swa_w_qkv, swa_b_qkv, swa_sinks, swa_w_o, swa_b_o,
           ffn_w_gate_up, ffn_w_down, moe_router, moe_w_gate_up, moe_w_down):
    batch, seq, d = x.shape
    depth = norm_mix.shape[0]
    t = batch * seq
    xf = x.reshape(t, d)
    pos_col = positions.reshape(t, 1).astype(F32)
    pos_row = positions.reshape(t // WINDOW, 1, WINDOW).astype(F32)
    zeros_d = jnp.zeros((d,), F32)
    for i in range(depth):
        kind = i % N_MIXERS
        j = i // N_MIXERS
        proj = None
        if kind == 0:
            xf = conv_mixer(xf, norm_mix[i], conv_w_in[j], conv_w[j], conv_w_out[j], seq)
        elif kind == 1:
            q, k, v = mla_project(xf, pos_col, norm_mix[i], mla_w_down[j], mla_q_norm[j], mla_w_uq[j],
                                  mla_kv_norm[j], mla_w_ukv[j])
            proj = (mla_flash(q, k, v, batch, seq), mla_w_o[j], zeros_d)
        else:
            wq, bq = swa_qkv_weights(swa_w_qkv[j], swa_b_qkv[j])
            qkv = rms_linear(xf, norm_mix[i], wq, bq)
            proj = (swa_attention(qkv, swa_sinks[j], pos_col, pos_row, batch, seq), swa_w_o[j], swa_b_o[j])
        f_idx = i // 2
        last = i == depth - 1
        if i % 2 == 0:
            xf = dense_ffn(xf, norm_ffn[i], ffn_w_gate_up[f_idx], ffn_w_down[f_idx], proj)
            if last:
                xf = final_rmsnorm(xf, norm_final)
        else:
            xf = moe_layer(xf, norm_ffn[i], moe_router[f_idx], moe_w_gate_up[f_idx], moe_w_down[f_idx],
                           norm_final, last, proj)
    return xf.reshape(batch, seq, d)


def _final_norm_body(x_ref, g_ref, o_ref):
    o_ref[...] = _rms(x_ref[...], g_ref[...])


def final_rmsnorm(x, g):
    t, d = x.shape
    tm = _tile(t, 1024)
    return pl.pallas_call(
        _final_norm_body,
        grid=(t // tm,),
        in_specs=[pl.BlockSpec((tm, d), lambda i: (i, 0)), _const_spec((1, d))],
        out_specs=pl.BlockSpec((tm, d), lambda i: (i, 0)),
        out_shape=jax.ShapeDtypeStruct((t, d), F32),
        compiler_params=_cparams(("arbitrary",)),
        name="final_rmsnorm",
    )(x, g.reshape(1, d))
```

```python
import functools
from typing import NamedTuple

import jax
import jax.numpy as jnp
from jax import lax
from jax.experimental import pallas as pl
from jax.experimental.pallas import tpu as pltpu

F32 = jnp.float32
BF16 = jnp.bfloat16

RMS_EPS = 1e-6
NEG_INF = -1e30
CONV_WIDTH = 3
MLA_HEADS = 8
MLA_Q_LORA = 384
MLA_KV_LORA = 256
MLA_NOPE = 128
MLA_ROPE = 64
MLA_V = 128
MLA_QK_PAD = 256
LOG2E = 1.4426950408889634
MLA_Q_SCALE = float((MLA_NOPE + MLA_ROPE) ** -0.5 * LOG2E)
ROPE_THETA = 10000.0
SWA_Q_HEADS = 16
SWA_KV_HEADS = 2
SWA_GROUP = SWA_Q_HEADS // SWA_KV_HEADS
SWA_HEAD_DIM = 64
WINDOW = 128
SWA_Q_SCALE = float(SWA_HEAD_DIM ** -0.5 * LOG2E)
SWA_MASKED_DIST = 1e32
N_EXPERTS = 8
TOP_K = 2
N_MIXERS = 3

VMEM_LIMIT = 56 * 1024 * 1024


def _cparams(sem):
    return pltpu.CompilerParams(dimension_semantics=sem, vmem_limit_bytes=VMEM_LIMIT)


def _rms(xf, g):
    ms = jnp.mean(xf * xf, axis=-1, keepdims=True)
    return xf * lax.rsqrt(ms + RMS_EPS) * g


def _const_spec(shape):
    nd = len(shape)
    return pl.BlockSpec(shape, lambda *_: (0,) * nd)


def _resident_spec(shape):
    nd = len(shape)
    return pl.BlockSpec(shape, lambda *_: (0,) * nd, pipeline_mode=pl.Buffered(1))


def _tile(n, pref):
    t = min(n, pref)
    assert n % t == 0, (n, t)
    return t


def _conv_body(x_ref, g_ref, win_ref, wc_ref, wout_ref, o_ref, vpad_ref, *, tm, d, tiles_per_seq):
    i = pl.program_id(0)
    x = x_ref[...]
    h = _rms(x, g_ref[...]).astype(BF16)
    bg = jnp.dot(h, win_ref[:, 0:d], preferred_element_type=F32)
    cg = jnp.dot(h, win_ref[:, d:2 * d], preferred_element_type=F32)
    u = jnp.dot(h, win_ref[:, 2 * d:3 * d], preferred_element_type=F32)
    v = cg * u

    @pl.when(i % tiles_per_seq == 0)
    def _():
        vpad_ref[0:8, :] = jnp.zeros((8, d), F32)

    vpad_ref[8:8 + tm, :] = v
    conv = vpad_ref[6:6 + tm, :] * wc_ref[0:1, :]
    conv = conv + vpad_ref[7:7 + tm, :] * wc_ref[1:2, :]
    conv = conv + v * wc_ref[2:3, :]
    vpad_ref[0:8, :] = vpad_ref[tm:tm + 8, :]
    y = (bg * conv).astype(BF16)
    o_ref[...] = x + jnp.dot(y, wout_ref[...], preferred_element_type=F32)


def conv_mixer(x, g, w_in, w_conv, w_out, seq):
    t, d = x.shape
    tm = _tile(seq, 512)
    body = functools.partial(_conv_body, tm=tm, d=d, tiles_per_seq=seq // tm)
    return pl.pallas_call(
        body,
        grid=(t // tm,),
        in_specs=[
            pl.BlockSpec((tm, d), lambda i: (i, 0)),
            _const_spec((1, d)),
            _resident_spec((d, 3 * d)),
            _const_spec((CONV_WIDTH, d)),
            _resident_spec((d, d)),
        ],
        out_specs=pl.BlockSpec((tm, d), lambda i: (i, 0)),
        out_shape=jax.ShapeDtypeStruct((t, d), F32),
        scratch_shapes=[pltpu.VMEM((tm + 8, d), F32)],
        compiler_params=_cparams(("arbitrary",)),
        name="conv_mixer",
    )(x, g.reshape(1, d), w_in.astype(BF16), w_conv, w_out.astype(BF16))


def _swiglu(h, wgu_cols, wd, a_ref, dff, tf):
    def gate_up(c):
        gate = jnp.dot(h, wgu_cols(c * tf, (c + 1) * tf), preferred_element_type=F32)
        up = jnp.dot(h, wgu_cols(dff + c * tf, dff + (c + 1) * tf), preferred_element_type=F32)
        return gate, up

    nxt = gate_up(0)
    for c in range(dff // tf):
        gate, up = nxt
        if c + 1 < dff // tf:
            nxt = gate_up(c + 1)
        a_ref[:, c * tf:(c + 1) * tf] = (gate * jax.nn.sigmoid(gate) * up).astype(BF16)
    return jnp.dot(a_ref[...], wd, preferred_element_type=F32)


def _mixer_out(x_ref, proj_refs):
    if not proj_refs:
        return x_ref[...]
    a_ref, w_ref, b_ref = proj_refs
    return (x_ref[...] + b_ref[...]) + jnp.dot(a_ref[...], w_ref[...], preferred_element_type=F32)


def _proj_specs(proj, tm, d):
    if proj is None:
        return [], []
    a, w, b = proj
    k = a.shape[1]
    specs = [pl.BlockSpec((tm, k), lambda i: (i, 0)), _resident_spec((k, d)), _const_spec((1, d))]
    return [a, w.astype(BF16), b.reshape(1, d).astype(F32)], specs


def _ffn_body(*refs, dff, tf, fused_proj):
    x_ref, *proj_refs = refs[:4] if fused_proj else refs[:1]
    g_ref, wgu_ref, wd_ref, o_ref, a_ref = refs[4:] if fused_proj else refs[1:]
    x = _mixer_out(x_ref, proj_refs)
    h = _rms(x, g_ref[...]).astype(BF16)
    o_ref[...] = x + _swiglu(h, lambda lo, hi: wgu_ref[:, lo:hi], wd_ref[...], a_ref, dff, tf)


def _ff_chunk(dff, pref):
    best = None
    for c in range(128, dff + 1, 128):
        if dff % c == 0 and c <= pref:
            best = c
    return best if best is not None else dff


def dense_ffn(x, g, w_gate_up, w_down, proj=None):
    t, d = x.shape
    dff = w_down.shape[0]
    tm = _tile(t, 512)
    tf = _ff_chunk(dff, 768)
    body = functools.partial(_ffn_body, dff=dff, tf=tf, fused_proj=proj is not None)
    proj_args, proj_specs = _proj_specs(proj, tm, d)
    return pl.pallas_call(
        body,
        grid=(t // tm,),
        in_specs=[pl.BlockSpec((tm, d), lambda i: (i, 0))] + proj_specs + [
            _const_spec((1, d)),
            _resident_spec((d, 2 * dff)),
            _resident_spec((dff, d)),
        ],
        out_specs=pl.BlockSpec((tm, d), lambda i: (i, 0)),
        out_shape=jax.ShapeDtypeStruct((t, d), F32),
        scratch_shapes=[pltpu.VMEM((tm, dff), BF16)],
        compiler_params=_cparams(("arbitrary",)),
        name="dense_ffn",
    )(x, *proj_args, g.reshape(1, d), w_gate_up.astype(BF16), w_down.astype(BF16))


def _rmslin_body(x_ref, g_ref, w_ref, b_ref, o_ref):
    h = _rms(x_ref[...], g_ref[...]).astype(BF16)
    o_ref[...] = (jnp.dot(h, w_ref[...], preferred_element_type=F32) + b_ref[...]).astype(o_ref.dtype)


def rms_linear(x, g, w, b):
    t, d = x.shape
    n = w.shape[1]
    tm = _tile(t, 512)
    return pl.pallas_call(
        _rmslin_body,
        grid=(t // tm,),
        in_specs=[
            pl.BlockSpec((tm, d), lambda i: (i, 0)),
            _const_spec((1, d)),
            _const_spec((d, n)),
            _const_spec((1, n)),
        ],
        out_specs=pl.BlockSpec((tm, n), lambda i: (i, 0)),
        out_shape=jax.ShapeDtypeStruct((t, n), BF16),
        compiler_params=_cparams(("arbitrary",)),
        name="rms_linear",
    )(x, g.reshape(1, d), w.astype(BF16), b.reshape(1, n).astype(F32))


def _mla_proj_body(x_ref, pos_ref, g_ref, wd_ref, qn_ref, kvn_ref, wqa_ref, wqb_ref, wkv_ref, frq_ref,
                   q_ref, k_ref, v_ref):
    h = _rms(x_ref[...], g_ref[...]).astype(BF16)
    down = jnp.dot(h, wd_ref[...], preferred_element_type=F32)
    c_q = down[:, 0:MLA_Q_LORA]
    c_kv = down[:, MLA_Q_LORA:MLA_Q_LORA + MLA_KV_LORA]
    o = MLA_Q_LORA + MLA_KV_LORA
    kpe = down[:, o:o + 128]
    kpe_sw = down[:, o + 128:o + 256]
    hq = _rms(c_q, qn_ref[...]).astype(BF16)
    hkv = _rms(c_kv, kvn_ref[...]).astype(BF16)
    qa = jnp.dot(hq, wqa_ref[...], preferred_element_type=F32)
    qb = jnp.dot(hq, wqb_ref[...], preferred_element_type=F32)
    kv = jnp.dot(hkv, wkv_ref[...], preferred_element_type=F32)
    ang = pos_ref[...] * frq_ref[0:1, :]
    cos = jnp.cos(ang)
    sin = jnp.sin(ang) * frq_ref[1:2, :]
    kpe_r = (kpe * cos + kpe_sw * sin).astype(BF16)
    ones_col = jnp.where(lax.broadcasted_iota(jnp.int32, cos.shape, 1) == 0, 1.0, 0.0).astype(BF16)
    for hd in range(MLA_HEADS):
        qo = hd * MLA_QK_PAD
        q_ref[hd, :, 0:MLA_NOPE] = qa[:, qo:qo + MLA_NOPE].astype(BF16)
        qpe = qa[:, qo + MLA_NOPE:qo + MLA_QK_PAD] * cos + qb[:, hd * 128:(hd + 1) * 128] * sin
        q_ref[hd, :, MLA_NOPE:MLA_QK_PAD] = qpe.astype(BF16)
        ko = hd * (MLA_NOPE + MLA_V)
        k_ref[hd, :, 0:MLA_NOPE] = kv[:, ko:ko + MLA_NOPE].astype(BF16)
        k_ref[hd, :, MLA_NOPE:MLA_QK_PAD] = kpe_r
        v_ref[hd, :, 0:MLA_V] = kv[:, ko + MLA_NOPE:ko + MLA_NOPE + MLA_V].astype(BF16)
        v_ref[hd, :, MLA_V:2 * MLA_V] = ones_col


def _rope_swap(w):
    half = MLA_ROPE // 2
    return jnp.concatenate([w[..., half:], w[..., :half]], axis=-1)


def mla_project(x, pos_col, g, w_down, q_norm, w_uq, kv_norm, w_ukv):
    t, d = x.shape
    tm = _tile(t, 512)
    hn = MLA_HEADS
    z64 = jnp.zeros((d, 64), F32)
    o = MLA_Q_LORA + MLA_KV_LORA
    kpe_w = w_down[:, o:o + MLA_ROPE]
    wd = jnp.concatenate([w_down[:, :o], kpe_w, z64, _rope_swap(kpe_w), z64], axis=1).astype(BF16)
    wq = (w_uq * MLA_Q_SCALE).reshape(MLA_Q_LORA, hn, MLA_NOPE + MLA_ROPE)
    zq = jnp.zeros((MLA_Q_LORA, hn, 64), F32)
    wqa = jnp.concatenate([wq, zq], axis=-1).reshape(MLA_Q_LORA, hn * MLA_QK_PAD).astype(BF16)
    wqb = jnp.concatenate([_rope_swap(wq[..., MLA_NOPE:]), zq], axis=-1).reshape(MLA_Q_LORA, hn * 128).astype(BF16)
    inv = ROPE_THETA ** (-jnp.arange(0, MLA_ROPE, 2, dtype=F32) / MLA_ROPE)
    half = MLA_ROPE // 2
    frq = jnp.stack([
        jnp.concatenate([inv, inv, jnp.zeros((64,), F32)]),
        jnp.concatenate([-jnp.ones((half,), F32), jnp.ones((half,), F32), jnp.zeros((64,), F32)]),
    ])
    nd = wd.shape[1]
    outs = pl.pallas_call(
        _mla_proj_body,
        grid=(t // tm,),
        in_specs=[
            pl.BlockSpec((tm, d), lambda i: (i, 0)),
            pl.BlockSpec((tm, 1), lambda i: (i, 0)),
            _const_spec((1, d)),
            _resident_spec((d, nd)),
            _const_spec((1, MLA_Q_LORA)),
            _const_spec((1, MLA_KV_LORA)),
            _resident_spec((MLA_Q_LORA, hn * MLA_QK_PAD)),
            _resident_spec((MLA_Q_LORA, hn * 128)),
            _resident_spec((MLA_KV_LORA, hn * (MLA_NOPE + MLA_V))),
            _const_spec((2, 128)),
        ],
        out_specs=[
            pl.BlockSpec((hn, tm, MLA_QK_PAD), lambda i: (0, i, 0)),
            pl.BlockSpec((hn, tm, MLA_QK_PAD), lambda i: (0, i, 0)),
            pl.BlockSpec((hn, tm, 2 * MLA_V), lambda i: (0, i, 0)),
        ],
        out_shape=[
            jax.ShapeDtypeStruct((hn, t, MLA_QK_PAD), BF16),
            jax.ShapeDtypeStruct((hn, t, MLA_QK_PAD), BF16),
            jax.ShapeDtypeStruct((hn, t, 2 * MLA_V), BF16),
        ],
        compiler_params=_cparams(("arbitrary",)),
        name="mla_project",
    )(x, pos_col, g.reshape(1, d), wd, q_norm.reshape(1, -1), kv_norm.reshape(1, -1), wqa, wqb,
      w_ukv.astype(BF16), frq)
    return outs


def _flash_body(qi_ref, kj_ref, q_ref, k_ref, v_ref, o_ref, m_ref, acc_ref, *, tq, tqs, hp, dv):
    pair = pl.program_id(2)
    i = qi_ref[pair]
    j = kj_ref[pair]

    @pl.when(j == 0)
    def _():
        m_ref[...] = jnp.full_like(m_ref, NEG_INF)
        acc_ref[...] = jnp.zeros_like(acc_ref)

    def step(diagonal):
        nsub = tq // tqs
        units = [(h, r) for h in range(hp) for r in range(nsub)]
        causal = (lax.broadcasted_iota(jnp.int32, (tqs, tqs), 1) <= lax.broadcasted_iota(jnp.int32, (tqs, tqs), 0))

        def scores(u):
            h, r = u
            nk = (r + 1) * tqs if diagonal else tq
            return lax.dot_general(q_ref[h, r * tqs:(r + 1) * tqs, :], k_ref[h, 0:nk, :], (((1,), (1,)), ((), ())),
                                   preferred_element_type=F32)

        s_next = scores(units[0])
        for n, (h, r) in enumerate(units):
            rows = slice(r * tqs, (r + 1) * tqs)
            nk = (r + 1) * tqs if diagonal else tq
            s = s_next
            if n + 1 < len(units):
                s_next = scores(units[n + 1])
            if diagonal:
                s_diag = jnp.where(causal, s[:, nk - tqs:nk], NEG_INF)
                s = s_diag if r == 0 else jnp.concatenate([s[:, 0:nk - tqs], s_diag], axis=1)
            m_prev = m_ref[h, rows, :]
            m_new = jnp.maximum(m_prev, jnp.max(s, axis=-1, keepdims=True))
            alpha = jnp.exp2(m_prev - m_new)
            p = jnp.exp2(s - m_new)
            pv = jnp.dot(p.astype(BF16), v_ref[h, 0:nk, :], preferred_element_type=F32)
            acc_ref[h, rows, :] = alpha * acc_ref[h, rows, :] + pv
            m_ref[h, rows, :] = m_new

    @pl.when(j < i)
    def _():
        step(False)

    @pl.when(j == i)
    def _():
        step(True)
        for h in range(hp):
            o_ref[:, h * dv:(h + 1) * dv] = (acc_ref[h, :, 0:dv] / acc_ref[h, :, dv:dv + 1]).astype(o_ref.dtype)


def mla_flash(q, k, v, batch, seq):
    hn, t, dq = q.shape
    dvp = v.shape[2]
    dv = MLA_V
    tq = tk = _tile(seq, 1024)
    nq = nk = seq // tq
    hp = 4
    body = functools.partial(_flash_body, tq=tq, tqs=_tile(tq, 256), hp=hp, dv=dv)
    pairs = [(i, j) for i in range(nq) for j in range(i + 1)]
    qi = jnp.asarray([p[0] for p in pairs], jnp.int32)
    kj = jnp.asarray([p[1] for p in pairs], jnp.int32)
    grid_spec = pltpu.PrefetchScalarGridSpec(
        num_scalar_prefetch=2,
        grid=(batch, hn // hp, len(pairs)),
        in_specs=[
            pl.BlockSpec((hp, tq, dq), lambda b, h, p, qi, kj: (h, b * nq + qi[p], 0)),
            pl.BlockSpec((hp, tk, dq), lambda b, h, p, qi, kj: (h, b * nk + kj[p], 0)),
            pl.BlockSpec((hp, tk, dvp), lambda b, h, p, qi, kj: (h, b * nk + kj[p], 0)),
        ],
        out_specs=pl.BlockSpec((tq, hp * dv), lambda b, h, p, qi, kj: (b * nq + qi[p], h)),
        scratch_shapes=[pltpu.VMEM((hp, tq, 1), F32), pltpu.VMEM((hp, tq, dvp), F32)],
    )
    return pl.pallas_call(
        body,
        grid_spec=grid_spec,
        out_shape=jax.ShapeDtypeStruct((t, hn * dv), BF16),
        compiler_params=_cparams(("arbitrary", "arbitrary", "arbitrary")),
        name="mla_flash",
    )(qi, kj, q, k, v)


def _alibi_slopes():
    return [2.0 ** (-8.0 * (h + 1) / SWA_Q_HEADS) for h in range(SWA_Q_HEADS)]


def _swa_body(sink_ref, q_ref, kc_ref, kp_ref, vc_ref, vp_ref, pq_ref, pkc_ref, pkp_ref, o_ref, *, steps_per_seq, nblk):
    w = WINDOW
    seq_start = pl.program_id(0) % steps_per_seq == 0
    for r in range(nblk):
        cur = slice(r * w, (r + 1) * w)
        prv = slice((r - 1) * w, r * w)
        _swa_block(
            sink_ref, q_ref.at[cur], kc_ref.at[cur], kp_ref if r == 0 else kc_ref.at[prv],
            vc_ref.at[cur], vp_ref if r == 0 else vc_ref.at[prv], pq_ref[cur, :], pkc_ref[r],
            pkp_ref[0] if r == 0 else pkc_ref[r - 1], o_ref.at[cur],
            has_prev=jnp.logical_not(seq_start) if r == 0 else True)


def _swa_block(sink_ref, q_ref, kc_ref, kp_ref, vc_ref, vp_ref, pq, pkc, pkp, o_ref, *, has_prev):
    w = WINDOW
    dist_c = jnp.abs(pq - pkc)
    dist_p = jnp.abs(pq - pkp)
    iq = lax.broadcasted_iota(jnp.int32, (w, w), 0)
    ik = lax.broadcasted_iota(jnp.int32, (w, w), 1)
    dist_c = jnp.where(ik <= iq, dist_c, SWA_MASKED_DIST)
    dist_p = jnp.where(jnp.logical_and(ik > iq, has_prev), dist_p, SWA_MASKED_DIST)
    lane = lax.broadcasted_iota(jnp.int32, (w, 2 * SWA_HEAD_DIM), 1)
    left = lane < SWA_HEAD_DIM
    slopes = _alibi_slopes()
    pairs = SWA_GROUP // 2
    for kh in range(SWA_KV_HEADS):
        kc = kc_ref[:, kh * 128:(kh + 1) * 128]
        kp = kp_ref[:, kh * 128:(kh + 1) * 128]
        vc_l = vc_ref[:, kh * 256:kh * 256 + 128]
        vc_r = vc_ref[:, kh * 256 + 128:kh * 256 + 256]
        vp_l = vp_ref[:, kh * 256:kh * 256 + 128]
        vp_r = vp_ref[:, kh * 256 + 128:kh * 256 + 256]
        zero = jnp.zeros((w, 128), BF16)
        rows = []
        for half in range(2):
            for p in range(pairs):
                qp = q_ref[:, (kh * pairs + p) * 128:(kh * pairs + p + 1) * 128]
                rows.append(jnp.where(left if half == 0 else jnp.logical_not(left), qp, zero))
        qs = jnp.concatenate(rows, axis=0)
        dn = (((1,), (1,)), ((), ()))
        s_c = lax.dot_general(qs, kc, dn, preferred_element_type=F32)
        s_p = lax.dot_general(qs, kp, dn, preferred_element_type=F32)
        pc_rows, pp_rows = [], []
        for half in range(2):
            for p in range(pairs):
                hd = kh * SWA_GROUP + 2 * p + half
                r0 = (half * pairs + p) * w
                sink = sink_ref[hd] * LOG2E
                a_c = s_c[r0:r0 + w] - (slopes[hd] * LOG2E) * dist_c
                a_p = s_p[r0:r0 + w] - (slopes[hd] * LOG2E) * dist_p
                m = jnp.maximum(jnp.max(jnp.maximum(a_c, a_p), axis=-1, keepdims=True), sink)
                e_c = jnp.exp2(a_c - m)
                e_p = jnp.exp2(a_p - m)
                den = jnp.sum(e_c + e_p, axis=-1, keepdims=True) + jnp.exp2(sink - m)
                inv = 1.0 / den
                pc_rows.append((e_c * inv).astype(BF16))
                pp_rows.append((e_p * inv).astype(BF16))
        hw = pairs * w
        pc = jnp.concatenate(pc_rows, axis=0)
        pp = jnp.concatenate(pp_rows, axis=0)
        o_pair = (jnp.dot(pc[0:hw], vc_l, preferred_element_type=F32)
                  + jnp.dot(pp[0:hw], vp_l, preferred_element_type=F32)
                  + jnp.dot(pc[hw:2 * hw], vc_r, preferred_element_type=F32)
                  + jnp.dot(pp[hw:2 * hw], vp_r, preferred_element_type=F32))
        for p in range(pairs):
            c0 = (kh * pairs + p) * 128
            o_ref[:, c0:c0 + 128] = o_pair[p * w:(p + 1) * w].astype(o_ref.dtype)


def swa_attention(qkv, sinks, pos_col, pos_row, batch, seq):
    t = qkv.shape[0]
    w = WINDOW
    nblk = 4 if (seq // w) % 4 == 0 else 1
    tq = nblk * w
    steps_per_seq = seq // tq
    nq = SWA_Q_HEADS * SWA_HEAD_DIM
    body = functools.partial(_swa_body, steps_per_seq=steps_per_seq, nblk=nblk)

    def prev(i):
        return jnp.where(i % steps_per_seq == 0, i * nblk, i * nblk - 1)

    kcol = (nq + 4 * 128) // 256
    return pl.pallas_call(
        body,
        grid=(t // tq,),
        in_specs=[
            pl.BlockSpec(memory_space=pltpu.SMEM),
            pl.BlockSpec((tq, nq), lambda i: (i, 0)),
            pl.BlockSpec((tq, 256), lambda i: (i, kcol)),
            pl.BlockSpec((w, 256), lambda i: (prev(i), kcol)),
            pl.BlockSpec((tq, 512), lambda i: (i, nq // 512)),
            pl.BlockSpec((w, 512), lambda i: (prev(i), nq // 512)),
            pl.BlockSpec((tq, 1), lambda i: (i, 0)),
            pl.BlockSpec((nblk, 1, w), lambda i: (i, 0, 0)),
            pl.BlockSpec((1, 1, w), lambda i: (prev(i), 0, 0)),
        ],
        out_specs=pl.BlockSpec((tq, nq), lambda i: (i, 0)),
        out_shape=jax.ShapeDtypeStruct((t, nq), BF16),
        compiler_params=_cparams(("arbitrary",)),
        name="swa_attention",
    )(sinks.astype(F32), qkv, qkv, qkv, qkv, qkv, pos_col, pos_row, pos_row)


def swa_qkv_weights(w_qkv, b_qkv):
    nq = SWA_Q_HEADS * SWA_HEAD_DIM
    nk = SWA_KV_HEADS * SWA_HEAD_DIM
    hd = SWA_HEAD_DIM

    def arrange(m):
        q = m[..., :nq] * SWA_Q_SCALE
        k = m[..., nq:nq + nk]
        v = m[..., nq + nk:]
        z = jnp.zeros(m.shape[:-1] + (hd,), m.dtype)
        parts = [q]
        for kh in range(SWA_KV_HEADS):
            vh = v[..., kh * hd:(kh + 1) * hd]
            parts += [vh, z, z, vh]
        for kh in range(SWA_KV_HEADS):
            kk = k[..., kh * hd:(kh + 1) * hd]
            parts += [kk, kk]
        return jnp.concatenate(parts, axis=-1)

    return arrange(w_qkv), arrange(b_qkv)


def _router_body(*refs, tm, fused_proj, steps_per_group):
    x_ref, *proj_refs = refs[:4] if fused_proj else refs[:1]
    rest = refs[4:] if fused_proj else refs[1:]
    if fused_proj:
        g_ref, r_ref, x1_ref, eid_ref, rank_ref, wgt_ref, cnt_ref, run_ref = rest
    else:
        g_ref, r_ref, eid_ref, rank_ref, wgt_ref, cnt_ref, run_ref = rest
    i = pl.program_id(0)

    @pl.when(i % steps_per_group == 0)
    def _():
        run_ref[...] = jnp.zeros_like(run_ref)

    x = _mixer_out(x_ref, proj_refs)
    if fused_proj:
        x1_ref[...] = x
    h = _rms(x, g_ref[...]).astype(BF16)
    logits = jnp.dot(h, r_ref[...], preferred_element_type=F32)
    lane = lax.broadcasted_iota(jnp.int32, logits.shape, 1)
    logits = jnp.where(lane < N_EXPERTS, logits, -jnp.inf)
    m1 = jnp.max(logits, axis=-1, keepdims=True)
    i1 = jnp.min(jnp.where(logits == m1, lane, 128), axis=-1, keepdims=True)
    rest = jnp.where(lane == i1, -jnp.inf, logits)
    m2 = jnp.max(rest, axis=-1, keepdims=True)
    i2 = jnp.min(jnp.where(rest == m2, lane, 128), axis=-1, keepdims=True)
    e2 = jnp.exp(m2 - m1)
    w1 = 1.0 / (1.0 + e2)
    w2 = e2 / (1.0 + e2)
    oh1 = (lane == i1)
    oh2 = (lane == i2)
    sel = jnp.where(jnp.logical_or(oh1, oh2), 1.0, 0.0).astype(BF16)
    r = lax.broadcasted_iota(jnp.int32, (tm, tm), 0)
    c = lax.broadcasted_iota(jnp.int32, (tm, tm), 1)
    tri = jnp.where(c < r, 1.0, 0.0).astype(BF16)
    before = jnp.dot(tri, sel, preferred_element_type=F32) + run_ref[...]
    rank1 = jnp.sum(jnp.where(oh1, before, 0.0), axis=-1, keepdims=True)
    rank2 = jnp.sum(jnp.where(oh2, before, 0.0), axis=-1, keepdims=True)
    run_ref[...] = run_ref[...] + jnp.sum(sel.astype(F32), axis=0, keepdims=True)
    eid_ref[...] = jnp.where(lane == 0, i1, jnp.where(lane == 1, i2, 0))[:, 0:TOP_K]
    rank_ref[...] = jnp.where(lane == 0, rank1, jnp.where(lane == 1, rank2, 0.0))[:, 0:TOP_K].astype(jnp.int32)
    wgt_ref[...] = jnp.where(lane == 0, w1, jnp.where(lane == 1, w2, 0.0))[:, 0:TOP_K]
    cnt_ref[0] = jnp.broadcast_to(run_ref[...], (8, 128)).astype(jnp.int32)


def moe_route(x, g, router, groups, proj=None):
    t, d = x.shape
    tm = _tile(t // groups, 512)
    steps_per_group = t // groups // tm
    rpad = jnp.zeros((d, 128), F32).at[:, :N_EXPERTS].set(router).astype(BF16)
    body = functools.partial(_router_body, tm=tm, fused_proj=proj is not None, steps_per_group=steps_per_group)
    proj_args, proj_specs = _proj_specs(proj, tm, d)
    x1_spec = [pl.BlockSpec((tm, d), lambda i: (i, 0))] if proj is not None else []
    x1_shape = [jax.ShapeDtypeStruct((t, d), F32)] if proj is not None else []
    outs = pl.pallas_call(
        body,
        grid=(t // tm,),
        in_specs=[pl.BlockSpec((tm, d), lambda i: (i, 0))] + proj_specs + [
            _const_spec((1, d)),
            _const_spec((d, 128)),
        ],
        out_specs=x1_spec + [
            pl.BlockSpec((tm, TOP_K), lambda i: (i, 0)),
            pl.BlockSpec((tm, TOP_K), lambda i: (i, 0)),
            pl.BlockSpec((tm, TOP_K), lambda i: (i, 0)),
            pl.BlockSpec((1, 8, 128), lambda i: (i // steps_per_group, 0, 0)),
        ],
        out_shape=x1_shape + [
            jax.ShapeDtypeStruct((t, TOP_K), jnp.int32),
            jax.ShapeDtypeStruct((t, TOP_K), jnp.int32),
            jax.ShapeDtypeStruct((t, TOP_K), F32),
            jax.ShapeDtypeStruct((groups, 8, 128), jnp.int32),
        ],
        scratch_shapes=[pltpu.VMEM((1, 128), F32)],
        compiler_params=_cparams(("arbitrary",)),
        name="moe_route",
    )(x, *proj_args, g.reshape(1, d), rpad)
    return outs if proj is not None else [x] + list(outs)


def _zero_pad_rows(pad_start_ref, pad_len_ref, xs_ref, zrow_ref, zsem):
    zrow_ref[...] = jnp.zeros_like(zrow_ref)
    for e in range(N_EXPERTS):
        def zero_row(r):
            return pltpu.make_async_copy(zrow_ref.at[pl.ds(0, 1)], xs_ref.at[pl.ds(pad_start_ref[e] + r, 1)], zsem)

        def zissue(r, carry):
            zero_row(r).start()
            return carry

        def zwait(r, carry):
            zero_row(r).wait()
            return carry

        lax.fori_loop(0, pad_len_ref[e], zissue, 0)
        lax.fori_loop(0, pad_len_ref[e], zwait, 0)


def _for_rows(n, fn, inline):
    if inline:
        for r in range(n):
            fn(r)
    else:
        lax.fori_loop(0, n, lambda r, c: (fn(r), c)[1], 0, unroll=8)


def _scatter_rows(dest_ref, x_ref, xs_ref, sem, n, inline):
    def issue(r):
        for kk in range(TOP_K):
            pltpu.make_async_copy(x_ref.at[pl.ds(r, 1)], xs_ref.at[pl.ds(dest_ref[r * TOP_K + kk], 1)], sem).start()

    _for_rows(n, issue, inline)


def _scatter_wait(x_ref, xs_ref, sem, n):
    for kk in range(TOP_K):
        pltpu.make_async_copy(x_ref, xs_ref.at[pl.ds(0, n)], sem).wait()


def _gather_rows(dest_ref, ys_ref, buf_ref, sem, n, inline):
    def issue(r):
        for kk in range(TOP_K):
            pltpu.make_async_copy(ys_ref.at[pl.ds(dest_ref[r * TOP_K + kk], 1)], buf_ref.at[kk, pl.ds(r, 1)], sem).start()

    _for_rows(n, issue, inline)


def _gather_wait(ys_ref, buf_ref, sem, n):
    for kk in range(TOP_K):
        pltpu.make_async_copy(ys_ref.at[pl.ds(0, n)], buf_ref.at[kk], sem).wait()


def _dispatch_body(pad_start_ref, pad_len_ref, dest_ref, x_ref, xs_ref, zrow_ref, sem, zsem, *, tm):
    @pl.when(pl.program_id(0) == 0)
    def _():
        _zero_pad_rows(pad_start_ref, pad_len_ref, xs_ref, zrow_ref, zsem)

    _scatter_rows(dest_ref, x_ref, xs_ref, sem, tm, inline=False)
    _scatter_wait(x_ref, xs_ref, sem, tm)


def moe_dispatch(x, row0, ntok, dest_flat, pad_start, pad_len, p):
    d = x.shape[1]
    tm = _tile(ntok, 512)
    tile0 = row0 // tm
    body = functools.partial(_dispatch_body, tm=tm)
    grid_spec = pltpu.PrefetchScalarGridSpec(
        num_scalar_prefetch=2,
        grid=(ntok // tm,),
        in_specs=[
            pl.BlockSpec((tm * TOP_K,), lambda i, ps, pn: (i,), memory_space=pltpu.SMEM),
            pl.BlockSpec((tm, d), lambda i, ps, pn: (tile0 + i, 0)),
        ],
        out_specs=pl.BlockSpec(memory_space=pl.ANY),
        scratch_shapes=[pltpu.VMEM((8, d), F32), pltpu.SemaphoreType.DMA(()), pltpu.SemaphoreType.DMA(())],
    )
    return pl.pallas_call(
        body,
        grid_spec=grid_spec,
        out_shape=jax.ShapeDtypeStruct((p, d), F32),
        compiler_params=_cparams(("arbitrary",)),
        name="moe_dispatch",
    )(pad_start, pad_len, dest_flat, x)


def _expert_body(*refs, dff, tf, side, ts, nside, final_norm):
    if side == "dispatch":
        (te_ref, nt_ref, pad_start_ref, pad_len_ref, xs_ref, g_ref, wgu_ref, wd_ref, dest_o_ref, x_o_ref,
         ys_ref, xs_o_ref, a_ref, zrow_ref, sem, zsem) = refs
    elif side == "combine":
        (te_ref, nt_ref, xs_ref, g_ref, wgu_ref, wd_ref, dest_o_ref, x_o_ref, w_o_ref, gf_ref, ys_o_ref,
         ys_ref, out_ref, a_ref, buf_ref, sem) = refs
    else:
        te_ref, nt_ref, xs_ref, g_ref, wgu_ref, wd_ref, ys_ref, a_ref = refs
    i = pl.program_id(0)

    def experts():
        h = _rms(xs_ref[...], g_ref[...]).astype(BF16)
        ys_ref[...] = _swiglu(h, lambda lo, hi: wgu_ref[0, :, lo:hi], wd_ref[0], a_ref, dff, tf)

    if side == "dispatch":
        @pl.when(i == 0)
        def _():
            _zero_pad_rows(pad_start_ref, pad_len_ref, xs_o_ref, zrow_ref, zsem)

    if side is None:
        pl.when(i < nt_ref[0])(experts)
    else:
        @pl.when(i < nside)
        def _():
            if side == "dispatch":
                _scatter_rows(dest_o_ref, x_o_ref, xs_o_ref, sem, ts, inline=True)
                experts()
                _scatter_wait(x_o_ref, xs_o_ref, sem, ts)
            else:
                _gather_rows(dest_o_ref, ys_o_ref, buf_ref, sem, ts, inline=True)
                experts()
                _gather_wait(ys_o_ref, buf_ref, sem, ts)
                out_ref[...] = _combine_rows(x_o_ref, w_o_ref, buf_ref, gf_ref, final_norm)

        pl.when(jnp.logical_and(i >= nside, i < nt_ref[0]))(experts)

    @pl.when(i >= nt_ref[0])
    def _():
        ys_ref[...] = jnp.zeros_like(ys_ref)


def _combine_rows(x_ref, w_ref, buf_ref, gf_ref, final_norm):
    w = w_ref[...]
    out = x_ref[...] + (w[:, 0:1] * buf_ref[0] + w[:, 1:2] * buf_ref[1])
    return _rms(out, gf_ref[...]) if final_norm else out


def moe_experts(xs, g, plan, w_gate_up, w_down, side=None, other=None, x=None, wgt=None, ys_other=None,
                g_final=None, final_norm=False):
    p, d = xs.shape
    tr = plan.tr
    dff = w_down.shape[1]
    tf = _ff_chunk(dff, 512)
    nside = plan.min_tiles
    ts = other.ntok // nside if side else 0
    body = functools.partial(_expert_body, dff=dff, tf=tf, side=side, ts=ts, nside=nside, final_norm=final_norm)
    nprefetch = 4 if side == "dispatch" else 2

    def row(i, nt):
        return jnp.minimum(i, nt[0] - 1)

    def imap(fn):
        return lambda i, *pf: fn(i, pf[0], pf[1])

    def side_tile(i, te, nt):
        return jnp.minimum(i, nside - 1)

    in_specs = [
        pl.BlockSpec((tr, d), imap(lambda i, te, nt: (row(i, nt), 0))),
        pl.BlockSpec((1, d), imap(lambda i, te, nt: (0, 0))),
        pl.BlockSpec((1, d, 2 * dff), imap(lambda i, te, nt: (te[row(i, nt)], 0, 0)), pipeline_mode=pl.Buffered(1)),
        pl.BlockSpec((1, dff, d), imap(lambda i, te, nt: (te[row(i, nt)], 0, 0)), pipeline_mode=pl.Buffered(1)),
    ]
    out_specs = [pl.BlockSpec((tr, d), imap(lambda i, te, nt: (i, 0)))]
    out_shape = [jax.ShapeDtypeStruct((p, d), F32)]
    scratch = [pltpu.VMEM((tr, dff), BF16)]
    args = [plan.tile_expert, plan.num_tiles]
    operands = [xs, g.reshape(1, d), w_gate_up.astype(BF16), w_down.astype(BF16)]
    aliases = {}
    if side:
        tile0 = other.row0 // ts
        in_specs += [
            pl.BlockSpec((ts * TOP_K,), imap(lambda i, te, nt: (side_tile(i, te, nt),)), memory_space=pltpu.SMEM),
            pl.BlockSpec((ts, d), imap(lambda i, te, nt: (tile0 + side_tile(i, te, nt), 0))),
        ]
        operands += [other.dest, x]
    if side == "dispatch":
        args += [other.pad_start, other.pad_len]
        out_specs.append(pl.BlockSpec(memory_space=pl.ANY))
        out_shape.append(jax.ShapeDtypeStruct((other.p, d), F32))
        scratch += [pltpu.VMEM((8, d), F32), pltpu.SemaphoreType.DMA(()), pltpu.SemaphoreType.DMA(())]
    elif side == "combine":
        in_specs += [
            pl.BlockSpec((ts, TOP_K), imap(lambda i, te, nt: (tile0 + side_tile(i, te, nt), 0))),
            pl.BlockSpec((1, d), imap(lambda i, te, nt: (0, 0))),
            pl.BlockSpec(memory_space=pl.ANY),
        ]
        operands += [wgt, g_final.reshape(1, d), ys_other]
        out_specs.append(pl.BlockSpec((ts, d), imap(lambda i, te, nt: (tile0 + side_tile(i, te, nt), 0))))
        out_shape.append(jax.ShapeDtypeStruct(x.shape, F32))
        scratch += [pltpu.VMEM((TOP_K, ts, d), F32), pltpu.SemaphoreType.DMA(())]
        aliases = {len(args) + 5: 1}
    grid_spec = pltpu.PrefetchScalarGridSpec(
        num_scalar_prefetch=nprefetch,
        grid=(p // tr,),
        in_specs=in_specs,
        out_specs=out_specs,
        scratch_shapes=scratch,
    )
    outs = pl.pallas_call(
        body,
        grid_spec=grid_spec,
        out_shape=out_shape,
        input_output_aliases=aliases,
        compiler_params=_cparams(("arbitrary",)),
        name="moe_experts" + ("_" + side if side else ""),
    )(*args, *operands)
    return outs if side else outs[0]


def _combine_body(dest_ref, dest_next_ref, x_ref, w_ref, gf_ref, ys_ref, o_ref, buf_ref, sems, *, tm, final_norm):
    i = pl.program_id(0)
    slot = i % 2

    def gather(d_ref, s):
        def issue(r, carry):
            for kk in range(TOP_K):
                src = d_ref[r * TOP_K + kk]
                pltpu.make_async_copy(ys_ref.at[pl.ds(src, 1)], buf_ref.at[s, kk, pl.ds(r, 1)], sems.at[s]).start()
            return carry

        lax.fori_loop(0, tm, issue, 0, unroll=8)

    @pl.when(i == 0)
    def _():
        gather(dest_ref, 0)

    @pl.when(i + 1 < pl.num_programs(0))
    def _():
        gather(dest_next_ref, 1 - slot)

    for kk in range(TOP_K):
        pltpu.make_async_copy(ys_ref.at[pl.ds(0, tm)], buf_ref.at[slot, kk], sems.at[slot]).wait()
    o_ref[...] = _combine_rows(x_ref, w_ref, buf_ref.at[slot], gf_ref, final_norm)


def moe_combine(x, wgt, plan, ys, g_final, final_norm):
    t, d = x.shape
    tm = _tile(plan.ntok, 512)
    body = functools.partial(_combine_body, tm=tm, final_norm=final_norm)
    nsteps = plan.ntok // tm
    tile0 = plan.row0 // tm
    return pl.pallas_call(
        body,
        grid=(nsteps,),
        in_specs=[
            pl.BlockSpec((tm * TOP_K,), lambda i: (i,), memory_space=pltpu.SMEM),
            pl.BlockSpec((tm * TOP_K,), lambda i: (jnp.minimum(i + 1, nsteps - 1),), memory_space=pltpu.SMEM),
            pl.BlockSpec((tm, d), lambda i: (tile0 + i, 0)),
            pl.BlockSpec((tm, TOP_K), lambda i: (tile0 + i, 0)),
            pl.BlockSpec((1, d), lambda i: (0, 0)),
            pl.BlockSpec(memory_space=pl.ANY),
        ],
        out_specs=pl.BlockSpec((tm, d), lambda i: (tile0 + i, 0)),
        out_shape=jax.ShapeDtypeStruct((t, d), F32),
        scratch_shapes=[pltpu.VMEM((2, TOP_K, tm, d), F32), pltpu.SemaphoreType.DMA((2,))],
        input_output_aliases={2: 0},
        compiler_params=_cparams(("arbitrary",)),
        name="moe_combine",
    )(plan.dest, plan.dest, x, wgt, g_final.reshape(1, d), ys)


class _GroupPlan(NamedTuple):
    row0: int
    ntok: int
    tr: int
    p: int
    min_tiles: int
    dest: jax.Array
    tile_expert: jax.Array
    num_tiles: jax.Array
    pad_start: jax.Array
    pad_len: jax.Array


def _plan_group(eid, rank, counts, row0, ntok, tr):
    padded = ((counts + tr - 1) // tr) * tr
    ends = jnp.cumsum(padded)
    base = ends - padded
    dest = (base[eid] + rank).reshape(-1).astype(jnp.int32)
    min_tiles = (ntok * TOP_K) // tr
    ntiles_max = min_tiles + N_EXPERTS
    p = ntiles_max * tr
    tile_start = jnp.arange(ntiles_max, dtype=jnp.int32) * tr
    tile_expert = jnp.minimum(jnp.sum(tile_start[:, None] >= ends[None, :], axis=1), N_EXPERTS - 1)
    pad_len = (padded - counts).at[N_EXPERTS - 1].add(p - ends[-1])
    return _GroupPlan(row0, ntok, tr, p, min_tiles, dest, tile_expert.astype(jnp.int32),
                      (ends[-1] // tr).astype(jnp.int32).reshape(1), (base + counts).astype(jnp.int32),
                      pad_len.astype(jnp.int32))


def moe_layer(x, g, router, w_gate_up, w_down, g_final, final_norm, proj=None):
    t, d = x.shape
    groups = 2 if t % 2048 == 0 else 1
    ntok = t // groups
    tr = min(512, ntok)
    x, eid, rank, wgt, cnt = moe_route(x, g, router, groups, proj)
    plans = [
        _plan_group(eid[k * ntok:(k + 1) * ntok], rank[k * ntok:(k + 1) * ntok], cnt[k, 0, :N_EXPERTS], k * ntok,
                    ntok, tr)
        for k in range(groups)
    ]
    pa = plans[0]
    xs_a = moe_dispatch(x, pa.row0, pa.ntok, pa.dest, pa.pad_start, pa.pad_len, pa.p)
    if groups == 1:
        ys_a = moe_experts(xs_a, g, pa, w_gate_up, w_down)
        return moe_combine(x, wgt, pa, ys_a, g_final, final_norm)
    pb = plans[1]
    ys_a, xs_b = moe_experts(xs_a, g, pa, w_gate_up, w_down, side="dispatch", other=pb, x=x)
    ys_b, x = moe_experts(xs_b, g, pb, w_gate_up, w_down, side="combine", other=pa, x=x, wgt=wgt, ys_other=ys_a,
                          g_final=g_final, final_norm=final_norm)
    return moe_combine(x, wgt, pb, ys_b, g_final, final_norm)


def kernel(x, positions, norm_mix, norm_ffn, norm_final, conv_w_in, conv_w, conv_w_out, mla_w_down, mla_q_norm,
           mla_w_uq, mla_kv_norm, mla_w_ukv, mla_w_o, swa_w_qkv, swa_b_qkv, swa_sinks, swa_w_o, swa_b_o,
           ffn_w_gate_up, ffn_w_down, moe_router, moe_w_gate_up, moe_w_down):
    batch, seq, d = x.shape
    depth = norm_mix.shape[0]
    t = batch * seq
    xf = x.reshape(t, d)
    pos_col = positions.reshape(t, 1).astype(F32)
    pos_row = positions.reshape(t // WINDOW, 1, WINDOW).astype(F32)
    zeros_d = jnp.zeros((d,), F32)
    for i in range(depth):
        kind = i % N_MIXERS
        j = i // N_MIXERS
        proj = None
        if kind == 0:
            xf = conv_mixer(xf, norm_mix[i], conv_w_in[j], conv_w[j], conv_w_out[j], seq)
        elif kind == 1:
            q, k, v = mla_project(xf, pos_col, norm_mix[i], mla_w_down[j], mla_q_norm[j], mla_w_uq[j],
                                  mla_kv_norm[j], mla_w_ukv[j])
            proj = (mla_flash(q, k, v, batch, seq), mla_w_o[j], zeros_d)
        else:
            wq, bq = swa_qkv_weights(swa_w_qkv[j], swa_b_qkv[j])
            qkv = rms_linear(xf, norm_mix[i], wq, bq)
            proj = (swa_attention(qkv, swa_sinks[j], pos_col, pos_row, batch, seq), swa_w_o[j], swa_b_o[j])
        f_idx = i // 2
        last = i == depth - 1
        if i % 2 == 0:
            xf = dense_ffn(xf, norm_ffn[i], ffn_w_gate_up[f_idx], ffn_w_down[f_idx], proj)
            if last:
                xf = final_rmsnorm(xf, norm_final)
        else:
            xf = moe_layer(xf, norm_ffn[i], moe_router[f_idx], moe_w_gate_up[f_idx], moe_w_down[f_idx],
                           norm_final, last, proj)
    return xf.reshape(batch, seq, d)


def _final_norm_body(x_ref, g_ref, o_ref):
    o_ref[...] = _rms(x_ref[...], g_ref[...])


def final_rmsnorm(x, g):
    t, d = x.shape
    tm = _tile(t, 1024)
    return pl.pallas_call(
        _final_norm_body,
        grid=(t // tm,),
        in_specs=[pl.BlockSpec((tm, d), lambda i: (i, 0)), _const_spec((1, d))],
        out_specs=pl.BlockSpec((tm, d), lambda i: (i, 0)),
        out_shape=jax.ShapeDtypeStruct((t, d), F32),
        compiler_params=_cparams(("arbitrary",)),
        name="final_rmsnorm",
    )(x, g.reshape(1, d))
```

```python
import functools
from typing import NamedTuple

import jax
import jax.numpy as jnp
from jax import lax
from jax.experimental import pallas as pl
from jax.experimental.pallas import tpu as pltpu

F32 = jnp.float32
BF16 = jnp.bfloat16

RMS_EPS = 1e-6
NEG_INF = -1e30
CONV_WIDTH = 3
MLA_HEADS = 8
MLA_Q_LORA = 384
MLA_KV_LORA = 256
MLA_NOPE = 128
MLA_ROPE = 64
MLA_V = 128
MLA_QK_PAD = 256
LOG2E = 1.4426950408889634
MLA_Q_SCALE = float((MLA_NOPE + MLA_ROPE) ** -0.5 * LOG2E)
ROPE_THETA = 10000.0
SWA_Q_HEADS = 16
SWA_KV_HEADS = 2
SWA_GROUP = SWA_Q_HEADS // SWA_KV_HEADS
SWA_HEAD_DIM = 64
WINDOW = 128
SWA_Q_SCALE = float(SWA_HEAD_DIM ** -0.5 * LOG2E)
SWA_MASKED_DIST = 1e32
N_EXPERTS = 8
TOP_K = 2
N_MIXERS = 3

VMEM_LIMIT = 56 * 1024 * 1024


def _cparams(sem):
    return pltpu.CompilerParams(dimension_semantics=sem, vmem_limit_bytes=VMEM_LIMIT)


def _rms(xf, g):
    ms = jnp.mean(xf * xf, axis=-1, keepdims=True)
    return xf * lax.rsqrt(ms + RMS_EPS) * g


def _const_spec(shape):
    nd = len(shape)
    return pl.BlockSpec(shape, lambda *_: (0,) * nd)


def _resident_spec(shape):
    nd = len(shape)
    return pl.BlockSpec(shape, lambda *_: (0,) * nd, pipeline_mode=pl.Buffered(1))


def _tile(n, pref):
    t = min(n, pref)
    assert n % t == 0, (n, t)
    return t


def _conv_body(x_ref, g_ref, win_ref, wc_ref, wout_ref, o_ref, vpad_ref, *, tm, d, tiles_per_seq, nsub):
    i = pl.program_id(0)
    ts = tm // nsub

    @pl.when(i % tiles_per_seq == 0)
    def _():
        vpad_ref[0:8, :] = jnp.zeros((8, d), F32)

    def in_proj(r):
        h = _rms(x_ref[r * ts:(r + 1) * ts, :], g_ref[...]).astype(BF16)
        return tuple(jnp.dot(h, win_ref[:, c * d:(c + 1) * d], preferred_element_type=F32) for c in range(3))

    nxt = in_proj(0)
    for r in range(nsub):
        bg, cg, u = nxt
        if r + 1 < nsub:
            nxt = in_proj(r + 1)
        v = cg * u
        lo = r * ts
        vpad_ref[8 + lo:8 + lo + ts, :] = v
        conv = vpad_ref[6 + lo:6 + lo + ts, :] * wc_ref[0:1, :]
        conv = conv + vpad_ref[7 + lo:7 + lo + ts, :] * wc_ref[1:2, :]
        conv = conv + v * wc_ref[2:3, :]
        y = (bg * conv).astype(BF16)
        o_ref[lo:lo + ts, :] = x_ref[lo:lo + ts, :] + jnp.dot(y, wout_ref[...], preferred_element_type=F32)
    vpad_ref[0:8, :] = vpad_ref[tm:tm + 8, :]


def conv_mixer(x, g, w_in, w_conv, w_out, seq):
    t, d = x.shape
    tm = _tile(seq, 512)
    body = functools.partial(_conv_body, tm=tm, d=d, tiles_per_seq=seq // tm, nsub=2 if tm % 256 == 0 else 1)
    return pl.pallas_call(
        body,
        grid=(t // tm,),
        in_specs=[
            pl.BlockSpec((tm, d), lambda i: (i, 0)),
            _const_spec((1, d)),
            _resident_spec((d, 3 * d)),
            _const_spec((CONV_WIDTH, d)),
            _resident_spec((d, d)),
        ],
        out_specs=pl.BlockSpec((tm, d), lambda i: (i, 0)),
        out_shape=jax.ShapeDtypeStruct((t, d), F32),
        scratch_shapes=[pltpu.VMEM((tm + 8, d), F32)],
        compiler_params=_cparams(("arbitrary",)),
        name="conv_mixer",
    )(x, g.reshape(1, d), w_in.astype(BF16), w_conv, w_out.astype(BF16))


def _swiglu(h, wgu_cols, wd, a_ref, dff, tf):
    def gate_up(c):
        gate = jnp.dot(h, wgu_cols(c * tf, (c + 1) * tf), preferred_element_type=F32)
        up = jnp.dot(h, wgu_cols(dff + c * tf, dff + (c + 1) * tf), preferred_element_type=F32)
        return gate, up

    nxt = gate_up(0)
    for c in range(dff // tf):
        gate, up = nxt
        if c + 1 < dff // tf:
            nxt = gate_up(c + 1)
        a_ref[:, c * tf:(c + 1) * tf] = (gate * jax.nn.sigmoid(gate) * up).astype(BF16)
    return jnp.dot(a_ref[...], wd, preferred_element_type=F32)


def _mixer_out(x_ref, proj_refs):
    if not proj_refs:
        return x_ref[...]
    a_ref, w_ref, b_ref = proj_refs
    return (x_ref[...] + b_ref[...]) + jnp.dot(a_ref[...], w_ref[...], preferred_element_type=F32)


def _proj_specs(proj, tm, d):
    if proj is None:
        return [], []
    a, w, b = proj
    k = a.shape[1]
    specs = [pl.BlockSpec((tm, k), lambda i: (i, 0)), _resident_spec((k, d)), _const_spec((1, d))]
    return [a, w.astype(BF16), b.reshape(1, d).astype(F32)], specs


def _ffn_body(*refs, dff, tf, fused_proj):
    x_ref, *proj_refs = refs[:4] if fused_proj else refs[:1]
    g_ref, wgu_ref, wd_ref, o_ref, a_ref = refs[4:] if fused_proj else refs[1:]
    x = _mixer_out(x_ref, proj_refs)
    h = _rms(x, g_ref[...]).astype(BF16)
    o_ref[...] = x + _swiglu(h, lambda lo, hi: wgu_ref[:, lo:hi], wd_ref[...], a_ref, dff, tf)


def _ff_chunk(dff, pref):
    best = None
    for c in range(128, dff + 1, 128):
        if dff % c == 0 and c <= pref:
            best = c
    return best if best is not None else dff


def dense_ffn(x, g, w_gate_up, w_down, proj=None):
    t, d = x.shape
    dff = w_down.shape[0]
    tm = _tile(t, 512)
    tf = _ff_chunk(dff, 768)
    body = functools.partial(_ffn_body, dff=dff, tf=tf, fused_proj=proj is not None)
    proj_args, proj_specs = _proj_specs(proj, tm, d)
    return pl.pallas_call(
        body,
        grid=(t // tm,),
        in_specs=[pl.BlockSpec((tm, d), lambda i: (i, 0))] + proj_specs + [
            _const_spec((1, d)),
            _resident_spec((d, 2 * dff)),
            _resident_spec((dff, d)),
        ],
        out_specs=pl.BlockSpec((tm, d), lambda i: (i, 0)),
        out_shape=jax.ShapeDtypeStruct((t, d), F32),
        scratch_shapes=[pltpu.VMEM((tm, dff), BF16)],
        compiler_params=_cparams(("arbitrary",)),
        name="dense_ffn",
    )(x, *proj_args, g.reshape(1, d), w_gate_up.astype(BF16), w_down.astype(BF16))


def _rmslin_body(x_ref, g_ref, w_ref, b_ref, o_ref):
    h = _rms(x_ref[...], g_ref[...]).astype(BF16)
    o_ref[...] = (jnp.dot(h, w_ref[...], preferred_element_type=F32) + b_ref[...]).astype(o_ref.dtype)


def rms_linear(x, g, w, b):
    t, d = x.shape
    n = w.shape[1]
    tm = _tile(t, 512)
    return pl.pallas_call(
        _rmslin_body,
        grid=(t // tm,),
        in_specs=[
            pl.BlockSpec((tm, d), lambda i: (i, 0)),
            _const_spec((1, d)),
            _const_spec((d, n)),
            _const_spec((1, n)),
        ],
        out_specs=pl.BlockSpec((tm, n), lambda i: (i, 0)),
        out_shape=jax.ShapeDtypeStruct((t, n), BF16),
        compiler_params=_cparams(("arbitrary",)),
        name="rms_linear",
    )(x, g.reshape(1, d), w.astype(BF16), b.reshape(1, n).astype(F32))


def _mla_proj_body(x_ref, pos_ref, g_ref, wd_ref, qn_ref, kvn_ref, wqa_ref, wqb_ref, wkv_ref, frq_ref,
                   q_ref, k_ref, v_ref):
    h = _rms(x_ref[...], g_ref[...]).astype(BF16)
    down = jnp.dot(h, wd_ref[...], preferred_element_type=F32)
    c_q = down[:, 0:MLA_Q_LORA]
    c_kv = down[:, MLA_Q_LORA:MLA_Q_LORA + MLA_KV_LORA]
    o = MLA_Q_LORA + MLA_KV_LORA
    kpe = down[:, o:o + 128]
    kpe_sw = down[:, o + 128:o + 256]
    hq = _rms(c_q, qn_ref[...]).astype(BF16)
    hkv = _rms(c_kv, kvn_ref[...]).astype(BF16)
    qa = jnp.dot(hq, wqa_ref[...], preferred_element_type=F32)
    qb = jnp.dot(hq, wqb_ref[...], preferred_element_type=F32)
    kv = jnp.dot(hkv, wkv_ref[...], preferred_element_type=F32)
    ang = pos_ref[...] * frq_ref[0:1, :]
    cos = jnp.cos(ang)
    sin = jnp.sin(ang) * frq_ref[1:2, :]
    kpe_r = (kpe * cos + kpe_sw * sin).astype(BF16)
    ones_col = jnp.where(lax.broadcasted_iota(jnp.int32, cos.shape, 1) == 0, 1.0, 0.0).astype(BF16)
    for hd in range(MLA_HEADS):
        qo = hd * MLA_QK_PAD
        q_ref[hd, :, 0:MLA_NOPE] = qa[:, qo:qo + MLA_NOPE].astype(BF16)
        qpe = qa[:, qo + MLA_NOPE:qo + MLA_QK_PAD] * cos + qb[:, hd * 128:(hd + 1) * 128] * sin
        q_ref[hd, :, MLA_NOPE:MLA_QK_PAD] = qpe.astype(BF16)
        ko = hd * (MLA_NOPE + MLA_V)
        k_ref[hd, :, 0:MLA_NOPE] = kv[:, ko:ko + MLA_NOPE].astype(BF16)
        k_ref[hd, :, MLA_NOPE:MLA_QK_PAD] = kpe_r
        v_ref[hd, :, 0:MLA_V] = kv[:, ko + MLA_NOPE:ko + MLA_NOPE + MLA_V].astype(BF16)
        v_ref[hd, :, MLA_V:2 * MLA_V] = ones_col


def _rope_swap(w):
    half = MLA_ROPE // 2
    return jnp.concatenate([w[..., half:], w[..., :half]], axis=-1)


def mla_project(x, pos_col, g, w_down, q_norm, w_uq, kv_norm, w_ukv):
    t, d = x.shape
    tm = _tile(t, 512)
    hn = MLA_HEADS
    z64 = jnp.zeros((d, 64), F32)
    o = MLA_Q_LORA + MLA_KV_LORA
    kpe_w = w_down[:, o:o + MLA_ROPE]
    wd = jnp.concatenate([w_down[:, :o], kpe_w, z64, _rope_swap(kpe_w), z64], axis=1).astype(BF16)
    wq = (w_uq * MLA_Q_SCALE).reshape(MLA_Q_LORA, hn, MLA_NOPE + MLA_ROPE)
    zq = jnp.zeros((MLA_Q_LORA, hn, 64), F32)
    wqa = jnp.concatenate([wq, zq], axis=-1).reshape(MLA_Q_LORA, hn * MLA_QK_PAD).astype(BF16)
    wqb = jnp.concatenate([_rope_swap(wq[..., MLA_NOPE:]), zq], axis=-1).reshape(MLA_Q_LORA, hn * 128).astype(BF16)
    inv = ROPE_THETA ** (-jnp.arange(0, MLA_ROPE, 2, dtype=F32) / MLA_ROPE)
    half = MLA_ROPE // 2
    frq = jnp.stack([
        jnp.concatenate([inv, inv, jnp.zeros((64,), F32)]),
        jnp.concatenate([-jnp.ones((half,), F32), jnp.ones((half,), F32), jnp.zeros((64,), F32)]),
    ])
    nd = wd.shape[1]
    outs = pl.pallas_call(
        _mla_proj_body,
        grid=(t // tm,),
        in_specs=[
            pl.BlockSpec((tm, d), lambda i: (i, 0)),
            pl.BlockSpec((tm, 1), lambda i: (i, 0)),
            _const_spec((1, d)),
            _resident_spec((d, nd)),
            _const_spec((1, MLA_Q_LORA)),
            _const_spec((1, MLA_KV_LORA)),
            _resident_spec((MLA_Q_LORA, hn * MLA_QK_PAD)),
            _resident_spec((MLA_Q_LORA, hn * 128)),
            _resident_spec((MLA_KV_LORA, hn * (MLA_NOPE + MLA_V))),
            _const_spec((2, 128)),
        ],
        out_specs=[
            pl.BlockSpec((hn, tm, MLA_QK_PAD), lambda i: (0, i, 0)),
            pl.BlockSpec((hn, tm, MLA_QK_PAD), lambda i: (0, i, 0)),
            pl.BlockSpec((hn, tm, 2 * MLA_V), lambda i: (0, i, 0)),
        ],
        out_shape=[
            jax.ShapeDtypeStruct((hn, t, MLA_QK_PAD), BF16),
            jax.ShapeDtypeStruct((hn, t, MLA_QK_PAD), BF16),
            jax.ShapeDtypeStruct((hn, t, 2 * MLA_V), BF16),
        ],
        compiler_params=_cparams(("arbitrary",)),
        name="mla_project",
    )(x, pos_col, g.reshape(1, d), wd, q_norm.reshape(1, -1), kv_norm.reshape(1, -1), wqa, wqb,
      w_ukv.astype(BF16), frq)
    return outs


def _flash_body(qi_ref, kj_ref, q_ref, k_ref, v_ref, o_ref, m_ref, acc_ref, *, tq, tqs, hp, dv):
    pair = pl.program_id(2)
    i = qi_ref[pair]
    j = kj_ref[pair]

    @pl.when(j == 0)
    def _():
        m_ref[...] = jnp.full_like(m_ref, NEG_INF)
        acc_ref[...] = jnp.zeros_like(acc_ref)

    def step(diagonal):
        nsub = tq // tqs
        units = [(h, r) for h in range(hp) for r in range(nsub)]
        causal = (lax.broadcasted_iota(jnp.int32, (tqs, tqs), 1) <= lax.broadcasted_iota(jnp.int32, (tqs, tqs), 0))

        def scores(u):
            h, r = u
            nk = (r + 1) * tqs if diagonal else tq
            return lax.dot_general(q_ref[h, r * tqs:(r + 1) * tqs, :], k_ref[h, 0:nk, :], (((1,), (1,)), ((), ())),
                                   preferred_element_type=F32)

        s_next = scores(units[0])
        for n, (h, r) in enumerate(units):
            rows = slice(r * tqs, (r + 1) * tqs)
            nk = (r + 1) * tqs if diagonal else tq
            s = s_next
            if n + 1 < len(units):
                s_next = scores(units[n + 1])
            if diagonal:
                s_diag = jnp.where(causal, s[:, nk - tqs:nk], NEG_INF)
                s = s_diag if r == 0 else jnp.concatenate([s[:, 0:nk - tqs], s_diag], axis=1)
            m_prev = m_ref[h, rows, :]
            m_new = jnp.maximum(m_prev, jnp.max(s, axis=-1, keepdims=True))
            alpha = jnp.exp2(m_prev - m_new)
            p = jnp.exp2(s - m_new)
            pv = jnp.dot(p.astype(BF16), v_ref[h, 0:nk, :], preferred_element_type=F32)
            acc_ref[h, rows, :] = alpha * acc_ref[h, rows, :] + pv
            m_ref[h, rows, :] = m_new

    @pl.when(j < i)
    def _():
        step(False)

    @pl.when(j == i)
    def _():
        step(True)
        for h in range(hp):
            o_ref[:, h * dv:(h + 1) * dv] = (acc_ref[h, :, 0:dv] / acc_ref[h, :, dv:dv + 1]).astype(o_ref.dtype)


def mla_flash(q, k, v, batch, seq):
    hn, t, dq = q.shape
    dvp = v.shape[2]
    dv = MLA_V
    tq = tk = _tile(seq, 1024)
    nq = nk = seq // tq
    hp = 4
    body = functools.partial(_flash_body, tq=tq, tqs=_tile(tq, 256), hp=hp, dv=dv)
    pairs = [(i, j) for i in range(nq) for j in range(i + 1)]
    qi = jnp.asarray([p[0] for p in pairs], jnp.int32)
    kj = jnp.asarray([p[1] for p in pairs], jnp.int32)
    grid_spec = pltpu.PrefetchScalarGridSpec(
        num_scalar_prefetch=2,
        grid=(batch, hn // hp, len(pairs)),
        in_specs=[
            pl.BlockSpec((hp, tq, dq), lambda b, h, p, qi, kj: (h, b * nq + qi[p], 0)),
            pl.BlockSpec((hp, tk, dq), lambda b, h, p, qi, kj: (h, b * nk + kj[p], 0)),
            pl.BlockSpec((hp, tk, dvp), lambda b, h, p, qi, kj: (h, b * nk + kj[p], 0)),
        ],
        out_specs=pl.BlockSpec((tq, hp * dv), lambda b, h, p, qi, kj: (b * nq + qi[p], h)),
        scratch_shapes=[pltpu.VMEM((hp, tq, 1), F32), pltpu.VMEM((hp, tq, dvp), F32)],
    )
    return pl.pallas_call(
        body,
        grid_spec=grid_spec,
        out_shape=jax.ShapeDtypeStruct((t, hn * dv), BF16),
        compiler_params=_cparams(("arbitrary", "arbitrary", "arbitrary")),
        name="mla_flash",
    )(qi, kj, q, k, v)


def _alibi_slopes():
    return [2.0 ** (-8.0 * (h + 1) / SWA_Q_HEADS) for h in range(SWA_Q_HEADS)]


def _swa_body(sink_ref, q_ref, kc_ref, kp_ref, vc_ref, vp_ref, pq_ref, pkc_ref, pkp_ref, o_ref, *, steps_per_seq, nblk):
    w = WINDOW
    seq_start = pl.program_id(0) % steps_per_seq == 0
    for r in range(nblk):
        cur = slice(r * w, (r + 1) * w)
        prv = slice((r - 1) * w, r * w)
        _swa_block(
            sink_ref, q_ref.at[cur], kc_ref.at[cur], kp_ref if r == 0 else kc_ref.at[prv],
            vc_ref.at[cur], vp_ref if r == 0 else vc_ref.at[prv], pq_ref[cur, :], pkc_ref[r],
            pkp_ref[0] if r == 0 else pkc_ref[r - 1], o_ref.at[cur],
            has_prev=jnp.logical_not(seq_start) if r == 0 else True)


def _swa_block(sink_ref, q_ref, kc_ref, kp_ref, vc_ref, vp_ref, pq, pkc, pkp, o_ref, *, has_prev):
    w = WINDOW
    dist_c = jnp.abs(pq - pkc)
    dist_p = jnp.abs(pq - pkp)
    iq = lax.broadcasted_iota(jnp.int32, (w, w), 0)
    ik = lax.broadcasted_iota(jnp.int32, (w, w), 1)
    dist_c = jnp.where(ik <= iq, dist_c, SWA_MASKED_DIST)
    dist_p = jnp.where(jnp.logical_and(ik > iq, has_prev), dist_p, SWA_MASKED_DIST)
    lane = lax.broadcasted_iota(jnp.int32, (w, 2 * SWA_HEAD_DIM), 1)
    left = lane < SWA_HEAD_DIM
    slopes = _alibi_slopes()
    pairs = SWA_GROUP // 2
    for kh in range(SWA_KV_HEADS):
        kc = kc_ref[:, kh * 128:(kh + 1) * 128]
        kp = kp_ref[:, kh * 128:(kh + 1) * 128]
        vc_l = vc_ref[:, kh * 256:kh * 256 + 128]
        vc_r = vc_ref[:, kh * 256 + 128:kh * 256 + 256]
        vp_l = vp_ref[:, kh * 256:kh * 256 + 128]
        vp_r = vp_ref[:, kh * 256 + 128:kh * 256 + 256]
        zero = jnp.zeros((w, 128), BF16)
        rows = []
        for half in range(2):
            for p in range(pairs):
                qp = q_ref[:, (kh * pairs + p) * 128:(kh * pairs + p + 1) * 128]
                rows.append(jnp.where(left if half == 0 else jnp.logical_not(left), qp, zero))
        qs = jnp.concatenate(rows, axis=0)
        dn = (((1,), (1,)), ((), ()))
        s_c = lax.dot_general(qs, kc, dn, preferred_element_type=F32)
        s_p = lax.dot_general(qs, kp, dn, preferred_element_type=F32)
        pc_rows, pp_rows = [], []
        for half in range(2):
            for p in range(pairs):
                hd = kh * SWA_GROUP + 2 * p + half
                r0 = (half * pairs + p) * w
                sink = sink_ref[hd] * LOG2E
                a_c = s_c[r0:r0 + w] - (slopes[hd] * LOG2E) * dist_c
                a_p = s_p[r0:r0 + w] - (slopes[hd] * LOG2E) * dist_p
                m = jnp.maximum(jnp.max(jnp.maximum(a_c, a_p), axis=-1, keepdims=True), sink)
                e_c = jnp.exp2(a_c - m)
                e_p = jnp.exp2(a_p - m)
                den = jnp.sum(e_c + e_p, axis=-1, keepdims=True) + jnp.exp2(sink - m)
                inv = 1.0 / den
                pc_rows.append((e_c * inv).astype(BF16))
                pp_rows.append((e_p * inv).astype(BF16))
        hw = pairs * w
        pc = jnp.concatenate(pc_rows, axis=0)
        pp = jnp.concatenate(pp_rows, axis=0)
        o_pair = (jnp.dot(pc[0:hw], vc_l, preferred_element_type=F32)
                  + jnp.dot(pp[0:hw], vp_l, preferred_element_type=F32)
                  + jnp.dot(pc[hw:2 * hw], vc_r, preferred_element_type=F32)
                  + jnp.dot(pp[hw:2 * hw], vp_r, preferred_element_type=F32))
        for p in range(pairs):
            c0 = (kh * pairs + p) * 128
            o_ref[:, c0:c0 + 128] = o_pair[p * w:(p + 1) * w].astype(o_ref.dtype)


def swa_attention(qkv, sinks, pos_col, pos_row, batch, seq):
    t = qkv.shape[0]
    w = WINDOW
    nblk = 4 if (seq // w) % 4 == 0 else 1
    tq = nblk * w
    steps_per_seq = seq // tq
    nq = SWA_Q_HEADS * SWA_HEAD_DIM
    body = functools.partial(_swa_body, steps_per_seq=steps_per_seq, nblk=nblk)

    def prev(i):
        return jnp.where(i % steps_per_seq == 0, i * nblk, i * nblk - 1)

    kcol = (nq + 4 * 128) // 256
    return pl.pallas_call(
        body,
        grid=(t // tq,),
        in_specs=[
            pl.BlockSpec(memory_space=pltpu.SMEM),
            pl.BlockSpec((tq, nq), lambda i: (i, 0)),
            pl.BlockSpec((tq, 256), lambda i: (i, kcol)),
            pl.BlockSpec((w, 256), lambda i: (prev(i), kcol)),
            pl.BlockSpec((tq, 512), lambda i: (i, nq // 512)),
            pl.BlockSpec((w, 512), lambda i: (prev(i), nq // 512)),
            pl.BlockSpec((tq, 1), lambda i: (i, 0)),
            pl.BlockSpec((nblk, 1, w), lambda i: (i, 0, 0)),
            pl.BlockSpec((1, 1, w), lambda i: (prev(i), 0, 0)),
        ],
        out_specs=pl.BlockSpec((tq, nq), lambda i: (i, 0)),
        out_shape=jax.ShapeDtypeStruct((t, nq), BF16),
        compiler_params=_cparams(("arbitrary",)),
        name="swa_attention",
    )(sinks.astype(F32), qkv, qkv, qkv, qkv, qkv, pos_col, pos_row, pos_row)


def swa_qkv_weights(w_qkv, b_qkv):
    nq = SWA_Q_HEADS * SWA_HEAD_DIM
    nk = SWA_KV_HEADS * SWA_HEAD_DIM
    hd = SWA_HEAD_DIM

    def arrange(m):
        q = m[..., :nq] * SWA_Q_SCALE
        k = m[..., nq:nq + nk]
        v = m[..., nq + nk:]
        z = jnp.zeros(m.shape[:-1] + (hd,), m.dtype)
        parts = [q]
        for kh in range(SWA_KV_HEADS):
            vh = v[..., kh * hd:(kh + 1) * hd]
            parts += [vh, z, z, vh]
        for kh in range(SWA_KV_HEADS):
            kk = k[..., kh * hd:(kh + 1) * hd]
            parts += [kk, kk]
        return jnp.concatenate(parts, axis=-1)

    return arrange(w_qkv), arrange(b_qkv)


def _router_body(*refs, tm, fused_proj, steps_per_group):
    x_ref, *proj_refs = refs[:4] if fused_proj else refs[:1]
    rest = refs[4:] if fused_proj else refs[1:]
    if fused_proj:
        g_ref, r_ref, x1_ref, eid_ref, rank_ref, wgt_ref, cnt_ref, run_ref = rest
    else:
        g_ref, r_ref, eid_ref, rank_ref, wgt_ref, cnt_ref, run_ref = rest
    i = pl.program_id(0)

    @pl.when(i % steps_per_group == 0)
    def _():
        run_ref[...] = jnp.zeros_like(run_ref)

    x = _mixer_out(x_ref, proj_refs)
    if fused_proj:
        x1_ref[...] = x
    h = _rms(x, g_ref[...]).astype(BF16)
    logits = jnp.dot(h, r_ref[...], preferred_element_type=F32)
    lane = lax.broadcasted_iota(jnp.int32, logits.shape, 1)
    logits = jnp.where(lane < N_EXPERTS, logits, -jnp.inf)
    m1 = jnp.max(logits, axis=-1, keepdims=True)
    i1 = jnp.min(jnp.where(logits == m1, lane, 128), axis=-1, keepdims=True)
    rest = jnp.where(lane == i1, -jnp.inf, logits)
    m2 = jnp.max(rest, axis=-1, keepdims=True)
    i2 = jnp.min(jnp.where(rest == m2, lane, 128), axis=-1, keepdims=True)
    e2 = jnp.exp(m2 - m1)
    w1 = 1.0 / (1.0 + e2)
    w2 = e2 / (1.0 + e2)
    oh1 = (lane == i1)
    oh2 = (lane == i2)
    sel = jnp.where(jnp.logical_or(oh1, oh2), 1.0, 0.0).astype(BF16)
    r = lax.broadcasted_iota(jnp.int32, (tm, tm), 0)
    c = lax.broadcasted_iota(jnp.int32, (tm, tm), 1)
    tri = jnp.where(c < r, 1.0, 0.0).astype(BF16)
    before = jnp.dot(tri, sel, preferred_element_type=F32) + run_ref[...]
    rank1 = jnp.sum(jnp.where(oh1, before, 0.0), axis=-1, keepdims=True)
    rank2 = jnp.sum(jnp.where(oh2, before, 0.0), axis=-1, keepdims=True)
    run_ref[...] = run_ref[...] + jnp.sum(sel.astype(F32), axis=0, keepdims=True)
    eid_ref[...] = jnp.where(lane == 0, i1, jnp.where(lane == 1, i2, 0))[:, 0:TOP_K]
    rank_ref[...] = jnp.where(lane == 0, rank1, jnp.where(lane == 1, rank2, 0.0))[:, 0:TOP_K].astype(jnp.int32)
    wgt_ref[...] = jnp.where(lane == 0, w1, jnp.where(lane == 1, w2, 0.0))[:, 0:TOP_K]
    cnt_ref[0] = jnp.broadcast_to(run_ref[...], (8, 128)).astype(jnp.int32)


def moe_route(x, g, router, groups, proj=None):
    t, d = x.shape
    tm = _tile(t // groups, 512)
    steps_per_group = t // groups // tm
    rpad = jnp.zeros((d, 128), F32).at[:, :N_EXPERTS].set(router).astype(BF16)
    body = functools.partial(_router_body, tm=tm, fused_proj=proj is not None, steps_per_group=steps_per_group)
    proj_args, proj_specs = _proj_specs(proj, tm, d)
    x1_spec = [pl.BlockSpec((tm, d), lambda i: (i, 0))] if proj is not None else []
    x1_shape = [jax.ShapeDtypeStruct((t, d), F32)] if proj is not None else []
    outs = pl.pallas_call(
        body,
        grid=(t // tm,),
        in_specs=[pl.BlockSpec((tm, d), lambda i: (i, 0))] + proj_specs + [
            _const_spec((1, d)),
            _const_spec((d, 128)),
        ],
        out_specs=x1_spec + [
            pl.BlockSpec((tm, TOP_K), lambda i: (i, 0)),
            pl.BlockSpec((tm, TOP_K), lambda i: (i, 0)),
            pl.BlockSpec((tm, TOP_K), lambda i: (i, 0)),
            pl.BlockSpec((1, 8, 128), lambda i: (i // steps_per_group, 0, 0)),
        ],
        out_shape=x1_shape + [
            jax.ShapeDtypeStruct((t, TOP_K), jnp.int32),
            jax.ShapeDtypeStruct((t, TOP_K), jnp.int32),
            jax.ShapeDtypeStruct((t, TOP_K), F32),
            jax.ShapeDtypeStruct((groups, 8, 128), jnp.int32),
        ],
        scratch_shapes=[pltpu.VMEM((1, 128), F32)],
        compiler_params=_cparams(("arbitrary",)),
        name="moe_route",
    )(x, *proj_args, g.reshape(1, d), rpad)
    return outs if proj is not None else [x] + list(outs)


def _zero_pad_rows(pad_start_ref, pad_len_ref, xs_ref, zrow_ref, zsem):
    zrow_ref[...] = jnp.zeros_like(zrow_ref)
    for e in range(N_EXPERTS):
        def zero_row(r):
            return pltpu.make_async_copy(zrow_ref.at[pl.ds(0, 1)], xs_ref.at[pl.ds(pad_start_ref[e] + r, 1)], zsem)

        def zissue(r, carry):
            zero_row(r).start()
            return carry

        def zwait(r, carry):
            zero_row(r).wait()
            return carry

        lax.fori_loop(0, pad_len_ref[e], zissue, 0)
        lax.fori_loop(0, pad_len_ref[e], zwait, 0)


def _for_rows(n, fn, inline):
    if inline:
        for r in range(n):
            fn(r)
    else:
        lax.fori_loop(0, n, lambda r, c: (fn(r), c)[1], 0, unroll=8)


def _scatter_rows(dest_ref, x_ref, xs_ref, sem, n, inline):
    def issue(r):
        for kk in range(TOP_K):
            pltpu.make_async_copy(x_ref.at[pl.ds(r, 1)], xs_ref.at[pl.ds(dest_ref[r * TOP_K + kk], 1)], sem).start()

    _for_rows(n, issue, inline)


def _scatter_wait(x_ref, xs_ref, sem, n):
    for kk in range(TOP_K):
        pltpu.make_async_copy(x_ref, xs_ref.at[pl.ds(0, n)], sem).wait()


def _gather_rows(dest_ref, ys_ref, buf_ref, sem, n, inline):
    def issue(r):
        for kk in range(TOP_K):
            pltpu.make_async_copy(ys_ref.at[pl.ds(dest_ref[r * TOP_K + kk], 1)], buf_ref.at[kk, pl.ds(r, 1)], sem).start()

    _for_rows(n, issue, inline)


def _gather_wait(ys_ref, buf_ref, sem, n):
    for kk in range(TOP_K):
        pltpu.make_async_copy(ys_ref.at[pl.ds(0, n)], buf_ref.at[kk], sem).wait()


def _dispatch_body(pad_start_ref, pad_len_ref, dest_ref, x_ref, xs_ref, zrow_ref, sem, zsem, *, tm):
    @pl.when(pl.program_id(0) == 0)
    def _():
        _zero_pad_rows(pad_start_ref, pad_len_ref, xs_ref, zrow_ref, zsem)

    _scatter_rows(dest_ref, x_ref, xs_ref, sem, tm, inline=False)
    _scatter_wait(x_ref, xs_ref, sem, tm)


def moe_dispatch(x, row0, ntok, dest_flat, pad_start, pad_len, p):
    d = x.shape[1]
    tm = _tile(ntok, 512)
    tile0 = row0 // tm
    body = functools.partial(_dispatch_body, tm=tm)
    grid_spec = pltpu.PrefetchScalarGridSpec(
        num_scalar_prefetch=2,
        grid=(ntok // tm,),
        in_specs=[
            pl.BlockSpec((tm * TOP_K,), lambda i, ps, pn: (i,), memory_space=pltpu.SMEM),
            pl.BlockSpec((tm, d), lambda i, ps, pn: (tile0 + i, 0)),
        ],
        out_specs=pl.BlockSpec(memory_space=pl.ANY),
        scratch_shapes=[pltpu.VMEM((8, d), F32), pltpu.SemaphoreType.DMA(()), pltpu.SemaphoreType.DMA(())],
    )
    return pl.pallas_call(
        body,
        grid_spec=grid_spec,
        out_shape=jax.ShapeDtypeStruct((p, d), F32),
        compiler_params=_cparams(("arbitrary",)),
        name="moe_dispatch",
    )(pad_start, pad_len, dest_flat, x)


def _expert_body(*refs, dff, tf, side, ts, nside, final_norm):
    if side == "dispatch":
        (te_ref, nt_ref, pad_start_ref, pad_len_ref, xs_ref, g_ref, wgu_ref, wd_ref, dest_o_ref, x_o_ref,
         ys_ref, xs_o_ref, a_ref, zrow_ref, sem, zsem) = refs
    elif side == "combine":
        (te_ref, nt_ref, xs_ref, g_ref, wgu_ref, wd_ref, dest_o_ref, x_o_ref, w_o_ref, gf_ref, ys_o_ref,
         ys_ref, out_ref, a_ref, buf_ref, sem) = refs
    else:
        te_ref, nt_ref, xs_ref, g_ref, wgu_ref, wd_ref, ys_ref, a_ref = refs
    i = pl.program_id(0)

    def experts():
        h = _rms(xs_ref[...], g_ref[...]).astype(BF16)
        ys_ref[...] = _swiglu(h, lambda lo, hi: wgu_ref[0, 0, :, lo:hi], wd_ref[0, 0], a_ref, dff, tf)

    if side == "dispatch":
        @pl.when(i == 0)
        def _():
            _zero_pad_rows(pad_start_ref, pad_len_ref, xs_o_ref, zrow_ref, zsem)

    if side is None:
        pl.when(i < nt_ref[0])(experts)
    else:
        @pl.when(i < nside)
        def _():
            if side == "dispatch":
                _scatter_rows(dest_o_ref, x_o_ref, xs_o_ref, sem, ts, inline=True)
                experts()
                _scatter_wait(x_o_ref, xs_o_ref, sem, ts)
            else:
                _gather_rows(dest_o_ref, ys_o_ref, buf_ref, sem, ts, inline=True)
                experts()
                _gather_wait(ys_o_ref, buf_ref, sem, ts)
                out_ref[...] = _combine_rows(x_o_ref, w_o_ref, buf_ref, gf_ref, final_norm)

        pl.when(jnp.logical_and(i >= nside, i < nt_ref[0]))(experts)

    @pl.when(i >= nt_ref[0])
    def _():
        ys_ref[...] = jnp.zeros_like(ys_ref)


def _combine_rows(x_ref, w_ref, buf_ref, gf_ref, final_norm):
    w = w_ref[...]
    out = x_ref[...] + (w[:, 0:1] * buf_ref[0] + w[:, 1:2] * buf_ref[1])
    return _rms(out, gf_ref[...]) if final_norm else out


def moe_experts(xs, g, plan, w_gate_up, w_down, layer, side=None, other=None, x=None, wgt=None, ys_other=None,
                g_final=None, final_norm=False):
    p, d = xs.shape
    tr = plan.tr
    dff = w_down.shape[2]
    tf = _ff_chunk(dff, 512)
    nside = plan.min_tiles
    ts = other.ntok // nside if side else 0
    body = functools.partial(_expert_body, dff=dff, tf=tf, side=side, ts=ts, nside=nside, final_norm=final_norm)
    nprefetch = 4 if side == "dispatch" else 2

    def row(i, nt):
        return jnp.minimum(i, nt[0] - 1)

    def imap(fn):
        return lambda i, *pf: fn(i, pf[0], pf[1])

    def side_tile(i, te, nt):
        return jnp.minimum(i, nside - 1)

    in_specs = [
        pl.BlockSpec((tr, d), imap(lambda i, te, nt: (row(i, nt), 0))),
        pl.BlockSpec((1, d), imap(lambda i, te, nt: (0, 0))),
        pl.BlockSpec((1, 1, d, 2 * dff), imap(lambda i, te, nt: (layer, te[row(i, nt)], 0, 0)),
                     pipeline_mode=pl.Buffered(1)),
        pl.BlockSpec((1, 1, dff, d), imap(lambda i, te, nt: (layer, te[row(i, nt)], 0, 0)),
                     pipeline_mode=pl.Buffered(1)),
    ]
    out_specs = [pl.BlockSpec((tr, d), imap(lambda i, te, nt: (i, 0)))]
    out_shape = [jax.ShapeDtypeStruct((p, d), F32)]
    scratch = [pltpu.VMEM((tr, dff), BF16)]
    args = [plan.tile_expert, plan.num_tiles]
    operands = [xs, g.reshape(1, d), w_gate_up, w_down]
    aliases = {}
    if side:
        tile0 = other.row0 // ts
        in_specs += [
            pl.BlockSpec((ts * TOP_K,), imap(lambda i, te, nt: (side_tile(i, te, nt),)), memory_space=pltpu.SMEM),
            pl.BlockSpec((ts, d), imap(lambda i, te, nt: (tile0 + side_tile(i, te, nt), 0))),
        ]
        operands += [other.dest, x]
    if side == "dispatch":
        args += [other.pad_start, other.pad_len]
        out_specs.append(pl.BlockSpec(memory_space=pl.ANY))
        out_shape.append(jax.ShapeDtypeStruct((other.p, d), F32))
        scratch += [pltpu.VMEM((8, d), F32), pltpu.SemaphoreType.DMA(()), pltpu.SemaphoreType.DMA(())]
    elif side == "combine":
        in_specs += [
            pl.BlockSpec((ts, TOP_K), imap(lambda i, te, nt: (tile0 + side_tile(i, te, nt), 0))),
            pl.BlockSpec((1, d), imap(lambda i, te, nt: (0, 0))),
            pl.BlockSpec(memory_space=pl.ANY),
        ]
        operands += [wgt, g_final.reshape(1, d), ys_other]
        out_specs.append(pl.BlockSpec((ts, d), imap(lambda i, te, nt: (tile0 + side_tile(i, te, nt), 0))))
        out_shape.append(jax.ShapeDtypeStruct(x.shape, F32))
        scratch += [pltpu.VMEM((TOP_K, ts, d), F32), pltpu.SemaphoreType.DMA(())]
        aliases = {len(args) + 5: 1}
    grid_spec = pltpu.PrefetchScalarGridSpec(
        num_scalar_prefetch=nprefetch,
        grid=(p // tr,),
        in_specs=in_specs,
        out_specs=out_specs,
        scratch_shapes=scratch,
    )
    outs = pl.pallas_call(
        body,
        grid_spec=grid_spec,
        out_shape=out_shape,
        input_output_aliases=aliases,
        compiler_params=_cparams(("arbitrary",)),
        name="moe_experts" + ("_" + side if side else ""),
    )(*args, *operands)
    return outs if side else outs[0]


def _combine_body(dest_ref, dest_next_ref, x_ref, w_ref, gf_ref, ys_ref, o_ref, buf_ref, sems, *, tm, final_norm):
    i = pl.program_id(0)
    slot = i % 2

    def gather(d_ref, s):
        def issue(r, carry):
            for kk in range(TOP_K):
                src = d_ref[r * TOP_K + kk]
                pltpu.make_async_copy(ys_ref.at[pl.ds(src, 1)], buf_ref.at[s, kk, pl.ds(r, 1)], sems.at[s]).start()
            return carry

        lax.fori_loop(0, tm, issue, 0, unroll=8)

    @pl.when(i == 0)
    def _():
        gather(dest_ref, 0)

    @pl.when(i + 1 < pl.num_programs(0))
    def _():
        gather(dest_next_ref, 1 - slot)

    for kk in range(TOP_K):
        pltpu.make_async_copy(ys_ref.at[pl.ds(0, tm)], buf_ref.at[slot, kk], sems.at[slot]).wait()
    o_ref[...] = _combine_rows(x_ref, w_ref, buf_ref.at[slot], gf_ref, final_norm)


def moe_combine(x, wgt, plan, ys, g_final, final_norm):
    t, d = x.shape
    tm = _tile(plan.ntok, 512)
    body = functools.partial(_combine_body, tm=tm, final_norm=final_norm)
    nsteps = plan.ntok // tm
    tile0 = plan.row0 // tm
    return pl.pallas_call(
        body,
        grid=(nsteps,),
        in_specs=[
            pl.BlockSpec((tm * TOP_K,), lambda i: (i,), memory_space=pltpu.SMEM),
            pl.BlockSpec((tm * TOP_K,), lambda i: (jnp.minimum(i + 1, nsteps - 1),), memory_space=pltpu.SMEM),
            pl.BlockSpec((tm, d), lambda i: (tile0 + i, 0)),
            pl.BlockSpec((tm, TOP_K), lambda i: (tile0 + i, 0)),
            pl.BlockSpec((1, d), lambda i: (0, 0)),
            pl.BlockSpec(memory_space=pl.ANY),
        ],
        out_specs=pl.BlockSpec((tm, d), lambda i: (tile0 + i, 0)),
        out_shape=jax.ShapeDtypeStruct((t, d), F32),
        scratch_shapes=[pltpu.VMEM((2, TOP_K, tm, d), F32), pltpu.SemaphoreType.DMA((2,))],
        input_output_aliases={2: 0},
        compiler_params=_cparams(("arbitrary",)),
        name="moe_combine",
    )(plan.dest, plan.dest, x, wgt, g_final.reshape(1, d), ys)


class _GroupPlan(NamedTuple):
    row0: int
    ntok: int
    tr: int
    p: int
    min_tiles: int
    dest: jax.Array
    tile_expert: jax.Array
    num_tiles: jax.Array
    pad_start: jax.Array
    pad_len: jax.Array


def _plan_group(eid, rank, counts, row0, ntok, tr):
    padded = ((counts + tr - 1) // tr) * tr
    ends = jnp.cumsum(padded)
    base = ends - padded
    dest = (base[eid] + rank).reshape(-1).astype(jnp.int32)
    min_tiles = (ntok * TOP_K) // tr
    ntiles_max = min_tiles + N_EXPERTS
    p = ntiles_max * tr
    tile_start = jnp.arange(ntiles_max, dtype=jnp.int32) * tr
    tile_expert = jnp.minimum(jnp.sum(tile_start[:, None] >= ends[None, :], axis=1), N_EXPERTS - 1)
    pad_len = (padded - counts).at[N_EXPERTS - 1].add(p - ends[-1])
    return _GroupPlan(row0, ntok, tr, p, min_tiles, dest, tile_expert.astype(jnp.int32),
                      (ends[-1] // tr).astype(jnp.int32).reshape(1), (base + counts).astype(jnp.int32),
                      pad_len.astype(jnp.int32))


def moe_layer(x, g, router, w_gate_up, w_down, layer, g_final, final_norm, proj=None):
    t, d = x.shape
    groups = 2 if t % 2048 == 0 else 1
    ntok = t // groups
    tr = min(512, ntok)
    x, eid, rank, wgt, cnt = moe_route(x, g, router, groups, proj)
    plans = [
        _plan_group(eid[k * ntok:(k + 1) * ntok], rank[k * ntok:(k + 1) * ntok], cnt[k, 0, :N_EXPERTS], k * ntok,
                    ntok, tr)
        for k in range(groups)
    ]
    pa = plans[0]
    xs_a = moe_dispatch(x, pa.row0, pa.ntok, pa.dest, pa.pad_start, pa.pad_len, pa.p)
    if groups == 1:
        ys_a = moe_experts(xs_a, g, pa, w_gate_up, w_down, layer)
        return moe_combine(x, wgt, pa, ys_a, g_final, final_norm)
    pb = plans[1]
    ys_a, xs_b = moe_experts(xs_a, g, pa, w_gate_up, w_down, layer, side="dispatch", other=pb, x=x)
    ys_b, x = moe_experts(xs_b, g, pb, w_gate_up, w_down, layer, side="combine", other=pa, x=x, wgt=wgt,
                          ys_other=ys_a, g_final=g_final, final_norm=final_norm)
    return moe_combine(x, wgt, pb, ys_b, g_final, final_norm)


def kernel(x, positions, norm_mix, norm_ffn, norm_final, conv_w_in, conv_w, conv_w_out, mla_w_down, mla_q_norm,
           mla_w_uq, mla_kv_norm, mla_w_ukv, mla_w_o, swa_w_qkv, swa_b_qkv, swa_sinks, swa_w_o, swa_b_o,
           ffn_w_gate_up, ffn_w_down, moe_router, moe_w_gate_up, moe_w_down):
    batch, seq, d = x.shape
    depth = norm_mix.shape[0]
    t = batch * seq
    xf = x.reshape(t, d)
    pos_col = positions.reshape(t, 1).astype(F32)
    pos_row = positions.reshape(t // WINDOW, 1, WINDOW).astype(F32)
    zeros_d = jnp.zeros((d,), F32)
    moe_wgu, moe_wd = moe_w_gate_up.astype(BF16), moe_w_down.astype(BF16)
    for i in range(depth):
        kind = i % N_MIXERS
        j = i // N_MIXERS
        proj = None
        if kind == 0:
            xf = conv_mixer(xf, norm_mix[i], conv_w_in[j], conv_w[j], conv_w_out[j], seq)
        elif kind == 1:
            q, k, v = mla_project(xf, pos_col, norm_mix[i], mla_w_down[j], mla_q_norm[j], mla_w_uq[j],
                                  mla_kv_norm[j], mla_w_ukv[j])
            proj = (mla_flash(q, k, v, batch, seq), mla_w_o[j], zeros_d)
        else:
            wq, bq = swa_qkv_weights(swa_w_qkv[j], swa_b_qkv[j])
            qkv = rms_linear(xf, norm_mix[i], wq, bq)
            proj = (swa_attention(qkv, swa_sinks[j], pos_col, pos_row, batch, seq), swa_w_o[j], swa_b_o[j])
        f_idx = i // 2
        last = i == depth - 1
        if i % 2 == 0:
            xf = dense_ffn(xf, norm_ffn[i], ffn_w_gate_up[f_idx], ffn_w_down[f_idx], proj)
            if last:
                xf = final_rmsnorm(xf, norm_final)
        else:
            xf = moe_layer(xf, norm_ffn[i], moe_router[f_idx], moe_wgu, moe_wd, f_idx, norm_final, last, proj)
    return xf.reshape(batch, seq, d)


def _final_norm_body(x_ref, g_ref, o_ref):
    o_ref[...] = _rms(x_ref[...], g_ref[...])


def final_rmsnorm(x, g):
    t, d = x.shape
    tm = _tile(t, 1024)
    return pl.pallas_call(
        _final_norm_body,
        grid=(t // tm,),
        in_specs=[pl.BlockSpec((tm, d), lambda i: (i, 0)), _const_spec((1, d))],
        out_specs=pl.BlockSpec((tm, d), lambda i: (i, 0)),
        out_shape=jax.ShapeDtypeStruct((t, d), F32),
        compiler_params=_cparams(("arbitrary",)),
        name="final_rmsnorm",
    )(x, g.reshape(1, d))
```

```python
import functools
from typing import NamedTuple

import jax
import jax.numpy as jnp
from jax import lax
from jax.experimental import pallas as pl
from jax.experimental.pallas import tpu as pltpu

F32 = jnp.float32
BF16 = jnp.bfloat16

RMS_EPS = 1e-6
NEG_INF = -1e30
CONV_WIDTH = 3
MLA_HEADS = 8
MLA_Q_LORA = 384
MLA_KV_LORA = 256
MLA_NOPE = 128
MLA_ROPE = 64
MLA_V = 128
MLA_QK_PAD = 256
LOG2E = 1.4426950408889634
MLA_Q_SCALE = float((MLA_NOPE + MLA_ROPE) ** -0.5 * LOG2E)
ROPE_THETA = 10000.0
SWA_Q_HEADS = 16
SWA_KV_HEADS = 2
SWA_GROUP = SWA_Q_HEADS // SWA_KV_HEADS
SWA_HEAD_DIM = 64
WINDOW = 128
SWA_Q_SCALE = float(SWA_HEAD_DIM ** -0.5 * LOG2E)
SWA_MASKED_DIST = 1e32
N_EXPERTS = 8
TOP_K = 2
N_MIXERS = 3

VMEM_LIMIT = 56 * 1024 * 1024


def _cparams(sem):
    return pltpu.CompilerParams(dimension_semantics=sem, vmem_limit_bytes=VMEM_LIMIT)


def _rms(xf, g):
    ms = jnp.mean(xf * xf, axis=-1, keepdims=True)
    return xf * lax.rsqrt(ms + RMS_EPS) * g


def _const_spec(shape):
    nd = len(shape)
    return pl.BlockSpec(shape, lambda *_: (0,) * nd)


def _resident_spec(shape):
    nd = len(shape)
    return pl.BlockSpec(shape, lambda *_: (0,) * nd, pipeline_mode=pl.Buffered(1))


def _tile(n, pref):
    t = min(n, pref)
    assert n % t == 0, (n, t)
    return t


def _conv_body(x_ref, g_ref, win_ref, wc_ref, wout_ref, o_ref, vpad_ref, *, tm, d, tiles_per_seq, nsub):
    i = pl.program_id(0)
    ts = tm // nsub

    @pl.when(i % tiles_per_seq == 0)
    def _():
        vpad_ref[0:8, :] = jnp.zeros((8, d), F32)

    def in_proj(r):
        h = _rms(x_ref[r * ts:(r + 1) * ts, :], g_ref[...]).astype(BF16)
        return tuple(jnp.dot(h, win_ref[:, c * d:(c + 1) * d], preferred_element_type=F32) for c in range(3))

    nxt = in_proj(0)
    for r in range(nsub):
        bg, cg, u = nxt
        if r + 1 < nsub:
            nxt = in_proj(r + 1)
        v = cg * u
        lo = r * ts
        vpad_ref[8 + lo:8 + lo + ts, :] = v
        conv = vpad_ref[6 + lo:6 + lo + ts, :] * wc_ref[0:1, :]
        conv = conv + vpad_ref[7 + lo:7 + lo + ts, :] * wc_ref[1:2, :]
        conv = conv + v * wc_ref[2:3, :]
        y = (bg * conv).astype(BF16)
        o_ref[lo:lo + ts, :] = x_ref[lo:lo + ts, :] + jnp.dot(y, wout_ref[...], preferred_element_type=F32)
    vpad_ref[0:8, :] = vpad_ref[tm:tm + 8, :]


def conv_mixer(x, g, w_in, w_conv, w_out, seq):
    t, d = x.shape
    tm = _tile(seq, 512)
    body = functools.partial(_conv_body, tm=tm, d=d, tiles_per_seq=seq // tm, nsub=2 if tm % 256 == 0 else 1)
    return pl.pallas_call(
        body,
        grid=(t // tm,),
        in_specs=[
            pl.BlockSpec((tm, d), lambda i: (i, 0)),
            _const_spec((1, d)),
            _resident_spec((d, 3 * d)),
            _const_spec((CONV_WIDTH, d)),
            _resident_spec((d, d)),
        ],
        out_specs=pl.BlockSpec((tm, d), lambda i: (i, 0)),
        out_shape=jax.ShapeDtypeStruct((t, d), F32),
        scratch_shapes=[pltpu.VMEM((tm + 8, d), F32)],
        compiler_params=_cparams(("arbitrary",)),
        name="conv_mixer",
    )(x, g.reshape(1, d), w_in.astype(BF16), w_conv, w_out.astype(BF16))


def _swiglu(h, wgu_cols, wd, a_ref, dff, tf):
    def gate_up(c):
        gate = jnp.dot(h, wgu_cols(c * tf, (c + 1) * tf), preferred_element_type=F32)
        up = jnp.dot(h, wgu_cols(dff + c * tf, dff + (c + 1) * tf), preferred_element_type=F32)
        return gate, up

    nxt = gate_up(0)
    for c in range(dff // tf):
        gate, up = nxt
        if c + 1 < dff // tf:
            nxt = gate_up(c + 1)
        a_ref[:, c * tf:(c + 1) * tf] = (gate * jax.nn.sigmoid(gate) * up).astype(BF16)
    return jnp.dot(a_ref[...], wd, preferred_element_type=F32)


def _mixer_out(x_ref, proj_refs):
    if not proj_refs:
        return x_ref[...]
    a_ref, w_ref, b_ref = proj_refs
    return (x_ref[...] + b_ref[...]) + jnp.dot(a_ref[...], w_ref[...], preferred_element_type=F32)


def _proj_specs(proj, tm, d):
    if proj is None:
        return [], []
    a, w, b = proj
    k = a.shape[1]
    specs = [pl.BlockSpec((tm, k), lambda i: (i, 0)), _resident_spec((k, d)), _const_spec((1, d))]
    return [a, w.astype(BF16), b.reshape(1, d).astype(F32)], specs


def _ffn_body(*refs, dff, tf, fused_proj):
    x_ref, *proj_refs = refs[:4] if fused_proj else refs[:1]
    g_ref, wgu_ref, wd_ref, o_ref, a_ref = refs[4:] if fused_proj else refs[1:]
    x = _mixer_out(x_ref, proj_refs)
    h = _rms(x, g_ref[...]).astype(BF16)
    o_ref[...] = x + _swiglu(h, lambda lo, hi: wgu_ref[:, lo:hi], wd_ref[...], a_ref, dff, tf)


def _ff_chunk(dff, pref):
    best = None
    for c in range(128, dff + 1, 128):
        if dff % c == 0 and c <= pref:
            best = c
    return best if best is not None else dff


def dense_ffn(x, g, w_gate_up, w_down, proj=None):
    t, d = x.shape
    dff = w_down.shape[0]
    tm = _tile(t, 512)
    tf = _ff_chunk(dff, 768)
    body = functools.partial(_ffn_body, dff=dff, tf=tf, fused_proj=proj is not None)
    proj_args, proj_specs = _proj_specs(proj, tm, d)
    return pl.pallas_call(
        body,
        grid=(t // tm,),
        in_specs=[pl.BlockSpec((tm, d), lambda i: (i, 0))] + proj_specs + [
            _const_spec((1, d)),
            _resident_spec((d, 2 * dff)),
            _resident_spec((dff, d)),
        ],
        out_specs=pl.BlockSpec((tm, d), lambda i: (i, 0)),
        out_shape=jax.ShapeDtypeStruct((t, d), F32),
        scratch_shapes=[pltpu.VMEM((tm, dff), BF16)],
        compiler_params=_cparams(("arbitrary",)),
        name="dense_ffn",
    )(x, *proj_args, g.reshape(1, d), w_gate_up.astype(BF16), w_down.astype(BF16))


def _rmslin_body(x_ref, g_ref, w_ref, b_ref, o_ref):
    h = _rms(x_ref[...], g_ref[...]).astype(BF16)
    o_ref[...] = (jnp.dot(h, w_ref[...], preferred_element_type=F32) + b_ref[...]).astype(o_ref.dtype)


def rms_linear(x, g, w, b):
    t, d = x.shape
    n = w.shape[1]
    tm = _tile(t, 512)
    return pl.pallas_call(
        _rmslin_body,
        grid=(t // tm,),
        in_specs=[
            pl.BlockSpec((tm, d), lambda i: (i, 0)),
            _const_spec((1, d)),
            _const_spec((d, n)),
            _const_spec((1, n)),
        ],
        out_specs=pl.BlockSpec((tm, n), lambda i: (i, 0)),
        out_shape=jax.ShapeDtypeStruct((t, n), BF16),
        compiler_params=_cparams(("arbitrary",)),
        name="rms_linear",
    )(x, g.reshape(1, d), w.astype(BF16), b.reshape(1, n).astype(F32))


def _mla_proj_body(x_ref, pos_ref, g_ref, wd_ref, qn_ref, kvn_ref, wqa_ref, wqb_ref, wkv_ref, frq_ref,
                   q_ref, k_ref, v_ref):
    h = _rms(x_ref[...], g_ref[...]).astype(BF16)
    down = jnp.dot(h, wd_ref[...], preferred_element_type=F32)
    c_q = down[:, 0:MLA_Q_LORA]
    c_kv = down[:, MLA_Q_LORA:MLA_Q_LORA + MLA_KV_LORA]
    o = MLA_Q_LORA + MLA_KV_LORA
    kpe = down[:, o:o + 128]
    kpe_sw = down[:, o + 128:o + 256]
    hq = _rms(c_q, qn_ref[...]).astype(BF16)
    hkv = _rms(c_kv, kvn_ref[...]).astype(BF16)
    qa = jnp.dot(hq, wqa_ref[...], preferred_element_type=F32)
    qb = jnp.dot(hq, wqb_ref[...], preferred_element_type=F32)
    kv = jnp.dot(hkv, wkv_ref[...], preferred_element_type=F32)
    ang = pos_ref[...] * frq_ref[0:1, :]
    cos = jnp.cos(ang)
    sin = jnp.sin(ang) * frq_ref[1:2, :]
    kpe_r = (kpe * cos + kpe_sw * sin).astype(BF16)
    ones_col = jnp.where(lax.broadcasted_iota(jnp.int32, cos.shape, 1) == 0, 1.0, 0.0).astype(BF16)
    for hd in range(MLA_HEADS):
        qo = hd * MLA_QK_PAD
        q_ref[hd, :, 0:MLA_NOPE] = qa[:, qo:qo + MLA_NOPE].astype(BF16)
        qpe = qa[:, qo + MLA_NOPE:qo + MLA_QK_PAD] * cos + qb[:, hd * 128:(hd + 1) * 128] * sin
        q_ref[hd, :, MLA_NOPE:MLA_QK_PAD] = qpe.astype(BF16)
        ko = hd * (MLA_NOPE + MLA_V)
        k_ref[hd, :, 0:MLA_NOPE] = kv[:, ko:ko + MLA_NOPE].astype(BF16)
        k_ref[hd, :, MLA_NOPE:MLA_QK_PAD] = kpe_r
        v_ref[hd, :, 0:MLA_V] = kv[:, ko + MLA_NOPE:ko + MLA_NOPE + MLA_V].astype(BF16)
        v_ref[hd, :, MLA_V:2 * MLA_V] = ones_col


def _rope_swap(w):
    half = MLA_ROPE // 2
    return jnp.concatenate([w[..., half:], w[..., :half]], axis=-1)


def mla_project(x, pos_col, g, w_down, q_norm, w_uq, kv_norm, w_ukv):
    t, d = x.shape
    tm = _tile(t, 512)
    hn = MLA_HEADS
    z64 = jnp.zeros((d, 64), F32)
    o = MLA_Q_LORA + MLA_KV_LORA
    kpe_w = w_down[:, o:o + MLA_ROPE]
    wd = jnp.concatenate([w_down[:, :o], kpe_w, z64, _rope_swap(kpe_w), z64], axis=1).astype(BF16)
    wq = (w_uq * MLA_Q_SCALE).reshape(MLA_Q_LORA, hn, MLA_NOPE + MLA_ROPE)
    zq = jnp.zeros((MLA_Q_LORA, hn, 64), F32)
    wqa = jnp.concatenate([wq, zq], axis=-1).reshape(MLA_Q_LORA, hn * MLA_QK_PAD).astype(BF16)
    wqb = jnp.concatenate([_rope_swap(wq[..., MLA_NOPE:]), zq], axis=-1).reshape(MLA_Q_LORA, hn * 128).astype(BF16)
    inv = ROPE_THETA ** (-jnp.arange(0, MLA_ROPE, 2, dtype=F32) / MLA_ROPE)
    half = MLA_ROPE // 2
    frq = jnp.stack([
        jnp.concatenate([inv, inv, jnp.zeros((64,), F32)]),
        jnp.concatenate([-jnp.ones((half,), F32), jnp.ones((half,), F32), jnp.zeros((64,), F32)]),
    ])
    nd = wd.shape[1]
    outs = pl.pallas_call(
        _mla_proj_body,
        grid=(t // tm,),
        in_specs=[
            pl.BlockSpec((tm, d), lambda i: (i, 0)),
            pl.BlockSpec((tm, 1), lambda i: (i, 0)),
            _const_spec((1, d)),
            _resident_spec((d, nd)),
            _const_spec((1, MLA_Q_LORA)),
            _const_spec((1, MLA_KV_LORA)),
            _resident_spec((MLA_Q_LORA, hn * MLA_QK_PAD)),
            _resident_spec((MLA_Q_LORA, hn * 128)),
            _resident_spec((MLA_KV_LORA, hn * (MLA_NOPE + MLA_V))),
            _const_spec((2, 128)),
        ],
        out_specs=[
            pl.BlockSpec((hn, tm, MLA_QK_PAD), lambda i: (0, i, 0)),
            pl.BlockSpec((hn, tm, MLA_QK_PAD), lambda i: (0, i, 0)),
            pl.BlockSpec((hn, tm, 2 * MLA_V), lambda i: (0, i, 0)),
        ],
        out_shape=[
            jax.ShapeDtypeStruct((hn, t, MLA_QK_PAD), BF16),
            jax.ShapeDtypeStruct((hn, t, MLA_QK_PAD), BF16),
            jax.ShapeDtypeStruct((hn, t, 2 * MLA_V), BF16),
        ],
        compiler_params=_cparams(("arbitrary",)),
        name="mla_project",
    )(x, pos_col, g.reshape(1, d), wd, q_norm.reshape(1, -1), kv_norm.reshape(1, -1), wqa, wqb,
      w_ukv.astype(BF16), frq)
    return outs


def _flash_body(qi_ref, kj_ref, q_ref, k_ref, v_ref, o_ref, m_ref, acc_ref, *, tq, tqs, hp, dv):
    pair = pl.program_id(2)
    i = qi_ref[pair]
    j = kj_ref[pair]

    @pl.when(j == 0)
    def _():
        m_ref[...] = jnp.full_like(m_ref, NEG_INF)
        acc_ref[...] = jnp.zeros_like(acc_ref)

    def step(diagonal):
        nsub = tq // tqs
        units = [(h, r) for h in range(hp) for r in range(nsub)]
        causal = (lax.broadcasted_iota(jnp.int32, (tqs, tqs), 1) <= lax.broadcasted_iota(jnp.int32, (tqs, tqs), 0))

        def scores(u):
            h, r = u
            nk = (r + 1) * tqs if diagonal else tq
            return lax.dot_general(q_ref[h, r * tqs:(r + 1) * tqs, :], k_ref[h, 0:nk, :], (((1,), (1,)), ((), ())),
                                   preferred_element_type=F32)

        s_next = scores(units[0])
        for n, (h, r) in enumerate(units):
            rows = slice(r * tqs, (r + 1) * tqs)
            nk = (r + 1) * tqs if diagonal else tq
            s = s_next
            if n + 1 < len(units):
                s_next = scores(units[n + 1])
            if diagonal:
                s_diag = jnp.where(causal, s[:, nk - tqs:nk], NEG_INF)
                s = s_diag if r == 0 else jnp.concatenate([s[:, 0:nk - tqs], s_diag], axis=1)
            m_prev = m_ref[h, rows, :]
            m_new = jnp.maximum(m_prev, jnp.max(s, axis=-1, keepdims=True))
            alpha = jnp.exp2(m_prev - m_new)
            p = jnp.exp2(s - m_new)
            pv = jnp.dot(p.astype(BF16), v_ref[h, 0:nk, :], preferred_element_type=F32)
            acc_ref[h, rows, :] = alpha * acc_ref[h, rows, :] + pv
            m_ref[h, rows, :] = m_new

    @pl.when(j < i)
    def _():
        step(False)

    @pl.when(j == i)
    def _():
        step(True)
        for h in range(hp):
            o_ref[:, h * dv:(h + 1) * dv] = (acc_ref[h, :, 0:dv] / acc_ref[h, :, dv:dv + 1]).astype(o_ref.dtype)


def mla_flash(q, k, v, batch, seq):
    hn, t, dq = q.shape
    dvp = v.shape[2]
    dv = MLA_V
    tq = tk = _tile(seq, 1024)
    nq = nk = seq // tq
    hp = 4
    body = functools.partial(_flash_body, tq=tq, tqs=_tile(tq, 256), hp=hp, dv=dv)
    pairs = [(i, j) for i in range(nq) for j in range(i + 1)]
    qi = jnp.asarray([p[0] for p in pairs], jnp.int32)
    kj = jnp.asarray([p[1] for p in pairs], jnp.int32)
    grid_spec = pltpu.PrefetchScalarGridSpec(
        num_scalar_prefetch=2,
        grid=(batch, hn // hp, len(pairs)),
        in_specs=[
            pl.BlockSpec((hp, tq, dq), lambda b, h, p, qi, kj: (h, b * nq + qi[p], 0)),
            pl.BlockSpec((hp, tk, dq), lambda b, h, p, qi, kj: (h, b * nk + kj[p], 0)),
            pl.BlockSpec((hp, tk, dvp), lambda b, h, p, qi, kj: (h, b * nk + kj[p], 0)),
        ],
        out_specs=pl.BlockSpec((tq, hp * dv), lambda b, h, p, qi, kj: (b * nq + qi[p], h)),
        scratch_shapes=[pltpu.VMEM((hp, tq, 1), F32), pltpu.VMEM((hp, tq, dvp), F32)],
    )
    return pl.pallas_call(
        body,
        grid_spec=grid_spec,
        out_shape=jax.ShapeDtypeStruct((t, hn * dv), BF16),
        compiler_params=_cparams(("arbitrary", "arbitrary", "arbitrary")),
        name="mla_flash",
    )(qi, kj, q, k, v)


def _alibi_slopes():
    return [2.0 ** (-8.0 * (h + 1) / SWA_Q_HEADS) for h in range(SWA_Q_HEADS)]


def _swa_body(sink_ref, q_ref, kc_ref, kp_ref, vc_ref, vp_ref, pq_ref, pkc_ref, pkp_ref, o_ref, *, steps_per_seq, nblk):
    w = WINDOW
    seq_start = pl.program_id(0) % steps_per_seq == 0
    for r in range(nblk):
        cur = slice(r * w, (r + 1) * w)
        prv = slice((r - 1) * w, r * w)
        _swa_block(
            sink_ref, q_ref.at[cur], kc_ref.at[cur], kp_ref if r == 0 else kc_ref.at[prv],
            vc_ref.at[cur], vp_ref if r == 0 else vc_ref.at[prv], pq_ref[cur, :], pkc_ref[r],
            pkp_ref[0] if r == 0 else pkc_ref[r - 1], o_ref.at[cur],
            has_prev=jnp.logical_not(seq_start) if r == 0 else True)


def _swa_block(sink_ref, q_ref, kc_ref, kp_ref, vc_ref, vp_ref, pq, pkc, pkp, o_ref, *, has_prev):
    w = WINDOW
    dist_c = jnp.abs(pq - pkc)
    dist_p = jnp.abs(pq - pkp)
    iq = lax.broadcasted_iota(jnp.int32, (w, w), 0)
    ik = lax.broadcasted_iota(jnp.int32, (w, w), 1)
    dist_c = jnp.where(ik <= iq, dist_c, SWA_MASKED_DIST)
    dist_p = jnp.where(jnp.logical_and(ik > iq, has_prev), dist_p, SWA_MASKED_DIST)
    lane = lax.broadcasted_iota(jnp.int32, (w, 2 * SWA_HEAD_DIM), 1)
    left = lane < SWA_HEAD_DIM
    slopes = _alibi_slopes()
    pairs = SWA_GROUP // 2
    for kh in range(SWA_KV_HEADS):
        kc = kc_ref[:, kh * 128:(kh + 1) * 128]
        kp = kp_ref[:, kh * 128:(kh + 1) * 128]
        vc_l = vc_ref[:, kh * 256:kh * 256 + 128]
        vc_r = vc_ref[:, kh * 256 + 128:kh * 256 + 256]
        vp_l = vp_ref[:, kh * 256:kh * 256 + 128]
        vp_r = vp_ref[:, kh * 256 + 128:kh * 256 + 256]
        zero = jnp.zeros((w, 128), BF16)
        rows = []
        for half in range(2):
            for p in range(pairs):
                qp = q_ref[:, (kh * pairs + p) * 128:(kh * pairs + p + 1) * 128]
                rows.append(jnp.where(left if half == 0 else jnp.logical_not(left), qp, zero))
        qs = jnp.concatenate(rows, axis=0)
        dn = (((1,), (1,)), ((), ()))
        s_c = lax.dot_general(qs, kc, dn, preferred_element_type=F32)
        s_p = lax.dot_general(qs, kp, dn, preferred_element_type=F32)
        pc_rows, pp_rows = [], []
        for half in range(2):
            for p in range(pairs):
                hd = kh * SWA_GROUP + 2 * p + half
                r0 = (half * pairs + p) * w
                sink = sink_ref[hd] * LOG2E
                a_c = s_c[r0:r0 + w] - (slopes[hd] * LOG2E) * dist_c
                a_p = s_p[r0:r0 + w] - (slopes[hd] * LOG2E) * dist_p
                m = jnp.maximum(jnp.max(jnp.maximum(a_c, a_p), axis=-1, keepdims=True), sink)
                e_c = jnp.exp2(a_c - m)
                e_p = jnp.exp2(a_p - m)
                den = jnp.sum(e_c + e_p, axis=-1, keepdims=True) + jnp.exp2(sink - m)
                inv = 1.0 / den
                pc_rows.append((e_c * inv).astype(BF16))
                pp_rows.append((e_p * inv).astype(BF16))
        hw = pairs * w
        pc = jnp.concatenate(pc_rows, axis=0)
        pp = jnp.concatenate(pp_rows, axis=0)
        o_pair = (jnp.dot(pc[0:hw], vc_l, preferred_element_type=F32)
                  + jnp.dot(pp[0:hw], vp_l, preferred_element_type=F32)
                  + jnp.dot(pc[hw:2 * hw], vc_r, preferred_element_type=F32)
                  + jnp.dot(pp[hw:2 * hw], vp_r, preferred_element_type=F32))
        for p in range(pairs):
            c0 = (kh * pairs + p) * 128
            o_ref[:, c0:c0 + 128] = o_pair[p * w:(p + 1) * w].astype(o_ref.dtype)


def swa_attention(qkv, sinks, pos_col, pos_row, batch, seq):
    t = qkv.shape[0]
    w = WINDOW
    nblk = 4 if (seq // w) % 4 == 0 else 1
    tq = nblk * w
    steps_per_seq = seq // tq
    nq = SWA_Q_HEADS * SWA_HEAD_DIM
    body = functools.partial(_swa_body, steps_per_seq=steps_per_seq, nblk=nblk)

    def prev(i):
        return jnp.where(i % steps_per_seq == 0, i * nblk, i * nblk - 1)

    kcol = (nq + 4 * 128) // 256
    return pl.pallas_call(
        body,
        grid=(t // tq,),
        in_specs=[
            pl.BlockSpec(memory_space=pltpu.SMEM),
            pl.BlockSpec((tq, nq), lambda i: (i, 0)),
            pl.BlockSpec((tq, 256), lambda i: (i, kcol)),
            pl.BlockSpec((w, 256), lambda i: (prev(i), kcol)),
            pl.BlockSpec((tq, 512), lambda i: (i, nq // 512)),
            pl.BlockSpec((w, 512), lambda i: (prev(i), nq // 512)),
            pl.BlockSpec((tq, 1), lambda i: (i, 0)),
            pl.BlockSpec((nblk, 1, w), lambda i: (i, 0, 0)),
            pl.BlockSpec((1, 1, w), lambda i: (prev(i), 0, 0)),
        ],
        out_specs=pl.BlockSpec((tq, nq), lambda i: (i, 0)),
        out_shape=jax.ShapeDtypeStruct((t, nq), BF16),
        compiler_params=_cparams(("arbitrary",)),
        name="swa_attention",
    )(sinks.astype(F32), qkv, qkv, qkv, qkv, qkv, pos_col, pos_row, pos_row)


def swa_qkv_weights(w_qkv, b_qkv):
    nq = SWA_Q_HEADS * SWA_HEAD_DIM
    nk = SWA_KV_HEADS * SWA_HEAD_DIM
    hd = SWA_HEAD_DIM

    def arrange(m):
        q = m[..., :nq] * SWA_Q_SCALE
        k = m[..., nq:nq + nk]
        v = m[..., nq + nk:]
        z = jnp.zeros(m.shape[:-1] + (hd,), m.dtype)
        parts = [q]
        for kh in range(SWA_KV_HEADS):
            vh = v[..., kh * hd:(kh + 1) * hd]
            parts += [vh, z, z, vh]
        for kh in range(SWA_KV_HEADS):
            kk = k[..., kh * hd:(kh + 1) * hd]
            parts += [kk, kk]
        return jnp.concatenate(parts, axis=-1)

    return arrange(w_qkv), arrange(b_qkv)


def _group_of_step(i, group_starts):
    gid = 0
    for s in group_starts[1:]:
        gid = gid + (i >= s).astype(jnp.int32)
    return gid


def _router_body(*refs, tm, fused_proj, group_starts):
    x_ref, *proj_refs = refs[:4] if fused_proj else refs[:1]
    rest = refs[4:] if fused_proj else refs[1:]
    if fused_proj:
        g_ref, r_ref, x1_ref, eid_ref, rank_ref, wgt_ref, cnt_ref, run_ref = rest
    else:
        g_ref, r_ref, eid_ref, rank_ref, wgt_ref, cnt_ref, run_ref = rest
    i = pl.program_id(0)

    is_start = functools.reduce(jnp.logical_or, [i == s for s in group_starts])

    @pl.when(is_start)
    def _():
        run_ref[...] = jnp.zeros_like(run_ref)

    x = _mixer_out(x_ref, proj_refs)
    if fused_proj:
        x1_ref[...] = x
    h = _rms(x, g_ref[...]).astype(BF16)
    logits = jnp.dot(h, r_ref[...], preferred_element_type=F32)
    lane = lax.broadcasted_iota(jnp.int32, logits.shape, 1)
    logits = jnp.where(lane < N_EXPERTS, logits, -jnp.inf)
    m1 = jnp.max(logits, axis=-1, keepdims=True)
    i1 = jnp.min(jnp.where(logits == m1, lane, 128), axis=-1, keepdims=True)
    rest = jnp.where(lane == i1, -jnp.inf, logits)
    m2 = jnp.max(rest, axis=-1, keepdims=True)
    i2 = jnp.min(jnp.where(rest == m2, lane, 128), axis=-1, keepdims=True)
    e2 = jnp.exp(m2 - m1)
    w1 = 1.0 / (1.0 + e2)
    w2 = e2 / (1.0 + e2)
    oh1 = (lane == i1)
    oh2 = (lane == i2)
    sel = jnp.where(jnp.logical_or(oh1, oh2), 1.0, 0.0).astype(BF16)
    r = lax.broadcasted_iota(jnp.int32, (tm, tm), 0)
    c = lax.broadcasted_iota(jnp.int32, (tm, tm), 1)
    tri = jnp.where(c < r, 1.0, 0.0).astype(BF16)
    before = jnp.dot(tri, sel, preferred_element_type=F32) + run_ref[...]
    rank1 = jnp.sum(jnp.where(oh1, before, 0.0), axis=-1, keepdims=True)
    rank2 = jnp.sum(jnp.where(oh2, before, 0.0), axis=-1, keepdims=True)
    run_ref[...] = run_ref[...] + jnp.sum(sel.astype(F32), axis=0, keepdims=True)
    eid_ref[...] = jnp.where(lane == 0, i1, jnp.where(lane == 1, i2, 0))[:, 0:TOP_K]
    rank_ref[...] = jnp.where(lane == 0, rank1, jnp.where(lane == 1, rank2, 0.0))[:, 0:TOP_K].astype(jnp.int32)
    wgt_ref[...] = jnp.where(lane == 0, w1, jnp.where(lane == 1, w2, 0.0))[:, 0:TOP_K]
    cnt_ref[0] = jnp.broadcast_to(run_ref[...], (8, 128)).astype(jnp.int32)


def moe_route(x, g, router, group_sizes, proj=None):
    t, d = x.shape
    groups = len(group_sizes)
    tm = _tile(min(group_sizes), 512)
    assert all(n % tm == 0 for n in group_sizes) and sum(group_sizes) == t
    group_starts = tuple(sum(group_sizes[:k]) // tm for k in range(groups))
    rpad = jnp.zeros((d, 128), F32).at[:, :N_EXPERTS].set(router).astype(BF16)
    body = functools.partial(_router_body, tm=tm, fused_proj=proj is not None, group_starts=group_starts)
    proj_args, proj_specs = _proj_specs(proj, tm, d)
    x1_spec = [pl.BlockSpec((tm, d), lambda i: (i, 0))] if proj is not None else []
    x1_shape = [jax.ShapeDtypeStruct((t, d), F32)] if proj is not None else []
    outs = pl.pallas_call(
        body,
        grid=(t // tm,),
        in_specs=[pl.BlockSpec((tm, d), lambda i: (i, 0))] + proj_specs + [
            _const_spec((1, d)),
            _const_spec((d, 128)),
        ],
        out_specs=x1_spec + [
            pl.BlockSpec((tm, TOP_K), lambda i: (i, 0)),
            pl.BlockSpec((tm, TOP_K), lambda i: (i, 0)),
            pl.BlockSpec((tm, TOP_K), lambda i: (i, 0)),
            pl.BlockSpec((1, 8, 128), lambda i: (_group_of_step(i, group_starts), 0, 0)),
        ],
        out_shape=x1_shape + [
            jax.ShapeDtypeStruct((t, TOP_K), jnp.int32),
            jax.ShapeDtypeStruct((t, TOP_K), jnp.int32),
            jax.ShapeDtypeStruct((t, TOP_K), F32),
            jax.ShapeDtypeStruct((groups, 8, 128), jnp.int32),
        ],
        scratch_shapes=[pltpu.VMEM((1, 128), F32)],
        compiler_params=_cparams(("arbitrary",)),
        name="moe_route",
    )(x, *proj_args, g.reshape(1, d), rpad)
    return outs if proj is not None else [x] + list(outs)


def _zero_pad_rows(pad_start_ref, pad_len_ref, xs_ref, zrow_ref, zsem):
    zrow_ref[...] = jnp.zeros_like(zrow_ref)
    for e in range(N_EXPERTS):
        def zero_row(r):
            return pltpu.make_async_copy(zrow_ref.at[pl.ds(0, 1)], xs_ref.at[pl.ds(pad_start_ref[e] + r, 1)], zsem)

        def zissue(r, carry):
            zero_row(r).start()
            return carry

        def zwait(r, carry):
            zero_row(r).wait()
            return carry

        lax.fori_loop(0, pad_len_ref[e], zissue, 0)
        lax.fori_loop(0, pad_len_ref[e], zwait, 0)


def _for_rows(n, fn, inline):
    if inline:
        for r in range(n):
            fn(r)
    else:
        lax.fori_loop(0, n, lambda r, c: (fn(r), c)[1], 0, unroll=8)


def _scatter_rows(dest_ref, x_ref, xs_ref, sem, n, inline):
    def issue(r):
        for kk in range(TOP_K):
            pltpu.make_async_copy(x_ref.at[pl.ds(r, 1)], xs_ref.at[pl.ds(dest_ref[r * TOP_K + kk], 1)], sem).start()

    _for_rows(n, issue, inline)


def _scatter_wait(x_ref, xs_ref, sem, n):
    for kk in range(TOP_K):
        pltpu.make_async_copy(x_ref, xs_ref.at[pl.ds(0, n)], sem).wait()


def _gather_rows(dest_ref, ys_ref, buf_ref, sem, n, inline):
    def issue(r):
        for kk in range(TOP_K):
            pltpu.make_async_copy(ys_ref.at[pl.ds(dest_ref[r * TOP_K + kk], 1)], buf_ref.at[kk, pl.ds(r, 1)], sem).start()

    _for_rows(n, issue, inline)


def _gather_wait(ys_ref, buf_ref, sem, n):
    for kk in range(TOP_K):
        pltpu.make_async_copy(ys_ref.at[pl.ds(0, n)], buf_ref.at[kk], sem).wait()


def _dispatch_body(pad_start_ref, pad_len_ref, dest_ref, x_ref, xs_ref, zrow_ref, sem, zsem, *, tm):
    @pl.when(pl.program_id(0) == 0)
    def _():
        _zero_pad_rows(pad_start_ref, pad_len_ref, xs_ref, zrow_ref, zsem)

    _scatter_rows(dest_ref, x_ref, xs_ref, sem, tm, inline=False)
    _scatter_wait(x_ref, xs_ref, sem, tm)


def moe_dispatch(x, row0, ntok, dest_flat, pad_start, pad_len, p):
    d = x.shape[1]
    tm = _tile(ntok, 512)
    tile0 = row0 // tm
    body = functools.partial(_dispatch_body, tm=tm)
    grid_spec = pltpu.PrefetchScalarGridSpec(
        num_scalar_prefetch=2,
        grid=(ntok // tm,),
        in_specs=[
            pl.BlockSpec((tm * TOP_K,), lambda i, ps, pn: (i,), memory_space=pltpu.SMEM),
            pl.BlockSpec((tm, d), lambda i, ps, pn: (tile0 + i, 0)),
        ],
        out_specs=pl.BlockSpec(memory_space=pl.ANY),
        scratch_shapes=[pltpu.VMEM((8, d), F32), pltpu.SemaphoreType.DMA(()), pltpu.SemaphoreType.DMA(())],
    )
    return pl.pallas_call(
        body,
        grid_spec=grid_spec,
        out_shape=jax.ShapeDtypeStruct((p, d), F32),
        compiler_params=_cparams(("arbitrary",)),
        name="moe_dispatch",
    )(pad_start, pad_len, dest_flat, x)


def _expert_body(*refs, dff, tf, ts_d, ts_c, nside, final_norm):
    refs = list(refs)
    te_ref, nt_ref = refs[:2]
    del refs[:2]
    if ts_d:
        pad_start_ref, pad_len_ref = refs[:2]
        del refs[:2]
    xs_ref, g_ref, wgu_ref, wd_ref = refs[:4]
    del refs[:4]
    if ts_d:
        dest_d_ref, x_d_ref = refs[:2]
        del refs[:2]
    if ts_c:
        dest_c_ref, x_c_ref, w_c_ref, gf_ref, ys_c_ref, _ = refs[:6]
        del refs[:6]
    ys_ref = refs.pop(0)
    if ts_d:
        xs_d_ref = refs.pop(0)
    if ts_c:
        out_ref = refs.pop(0)
    a_ref = refs.pop(0)
    if ts_d:
        zrow_ref, sem_d, zsem = refs[:3]
        del refs[:3]
    if ts_c:
        buf_ref, sem_c = refs[:2]
    i = pl.program_id(0)

    def experts():
        h = _rms(xs_ref[...], g_ref[...]).astype(BF16)
        ys_ref[...] = _swiglu(h, lambda lo, hi: wgu_ref[0, 0, :, lo:hi], wd_ref[0, 0], a_ref, dff, tf)

    if ts_d:
        @pl.when(i == 0)
        def _():
            _zero_pad_rows(pad_start_ref, pad_len_ref, xs_d_ref, zrow_ref, zsem)

    if not (ts_d or ts_c):
        pl.when(i < nt_ref[0])(experts)
    else:
        @pl.when(i < nside)
        def _():
            if ts_d:
                _scatter_rows(dest_d_ref, x_d_ref, xs_d_ref, sem_d, ts_d, inline=True)
            if ts_c:
                _gather_rows(dest_c_ref, ys_c_ref, buf_ref, sem_c, ts_c, inline=True)
            experts()
            if ts_d:
                _scatter_wait(x_d_ref, xs_d_ref, sem_d, ts_d)
            if ts_c:
                _gather_wait(ys_c_ref, buf_ref, sem_c, ts_c)
                out_ref[...] = _combine_rows(x_c_ref, w_c_ref, buf_ref, gf_ref, final_norm)

        pl.when(jnp.logical_and(i >= nside, i < nt_ref[0]))(experts)

    @pl.when(i >= nt_ref[0])
    def _():
        ys_ref[...] = jnp.zeros_like(ys_ref)


def _combine_rows(x_ref, w_ref, buf_ref, gf_ref, final_norm):
    w = w_ref[...]
    out = x_ref[...] + (w[:, 0:1] * buf_ref[0] + w[:, 1:2] * buf_ref[1])
    return _rms(out, gf_ref[...]) if final_norm else out


def moe_experts(xs, g, plan, w_gate_up, w_down, layer, disp=None, comb=None, x=None, wgt=None, ys_comb=None,
                out=None, g_final=None, final_norm=False):
    p, d = xs.shape
    tr = plan.tr
    dff = w_down.shape[2]
    tf = _ff_chunk(dff, 512)
    nside = plan.min_tiles
    ts_d = disp.ntok // nside if disp is not None else 0
    ts_c = comb.ntok // nside if comb is not None else 0
    body = functools.partial(_expert_body, dff=dff, tf=tf, ts_d=ts_d, ts_c=ts_c, nside=nside, final_norm=final_norm)

    def row(i, nt):
        return jnp.minimum(i, nt[0] - 1)

    def imap(fn):
        return lambda i, *pf: fn(i, pf[0], pf[1])

    def side_specs(o, ts):
        tile0 = o.row0 // ts

        def tok_map(i, te, nt):
            return (tile0 + jnp.minimum(i, nside - 1), 0)

        dest_spec = pl.BlockSpec((ts * TOP_K,), imap(lambda i, te, nt: (jnp.minimum(i, nside - 1),)),
                                 memory_space=pltpu.SMEM)
        return dest_spec, (lambda width: pl.BlockSpec((ts, width), imap(tok_map)))

    in_specs = [
        pl.BlockSpec((tr, d), imap(lambda i, te, nt: (row(i, nt), 0))),
        pl.BlockSpec((1, d), imap(lambda i, te, nt: (0, 0))),
        pl.BlockSpec((1, 1, d, 2 * dff), imap(lambda i, te, nt: (layer, te[row(i, nt)], 0, 0)),
                     pipeline_mode=pl.Buffered(1)),
        pl.BlockSpec((1, 1, dff, d), imap(lambda i, te, nt: (layer, te[row(i, nt)], 0, 0)),
                     pipeline_mode=pl.Buffered(1)),
    ]
    out_specs = [pl.BlockSpec((tr, d), imap(lambda i, te, nt: (i, 0)))]
    out_shape = [jax.ShapeDtypeStruct((p, d), F32)]
    scratch = [pltpu.VMEM((tr, dff), BF16)]
    args = [plan.tile_expert, plan.num_tiles]
    operands = [xs, g.reshape(1, d), w_gate_up, w_down]
    aliases = {}
    if disp is not None:
        dest_spec, tok_spec = side_specs(disp, ts_d)
        args += [disp.pad_start, disp.pad_len]
        in_specs += [dest_spec, tok_spec(d)]
        operands += [disp.dest, x]
        out_specs.append(pl.BlockSpec(memory_space=pl.ANY))
        out_shape.append(jax.ShapeDtypeStruct((disp.p, d), F32))
        scratch += [pltpu.VMEM((8, d), F32), pltpu.SemaphoreType.DMA(()), pltpu.SemaphoreType.DMA(())]
    if comb is not None:
        dest_spec, tok_spec = side_specs(comb, ts_c)
        in_specs += [dest_spec, tok_spec(d), tok_spec(TOP_K), pl.BlockSpec((1, d), imap(lambda i, te, nt: (0, 0))),
                     pl.BlockSpec(memory_space=pl.ANY), pl.BlockSpec(memory_space=pl.ANY)]
        operands += [comb.dest, x, wgt, g_final.reshape(1, d), ys_comb, out]
        aliases = {len(args) + len(operands) - 1: len(out_specs)}
        out_specs.append(tok_spec(d))
        out_shape.append(jax.ShapeDtypeStruct(out.shape, F32))
        scratch += [pltpu.VMEM((TOP_K, ts_c, d), F32), pltpu.SemaphoreType.DMA(())]
    grid_spec = pltpu.PrefetchScalarGridSpec(
        num_scalar_prefetch=len(args),
        grid=(p // tr,),
        in_specs=in_specs,
        out_specs=out_specs,
        scratch_shapes=scratch,
    )
    outs = pl.pallas_call(
        body,
        grid_spec=grid_spec,
        out_shape=out_shape,
        input_output_aliases=aliases,
        compiler_params=_cparams(("arbitrary",)),
        name="moe_experts" + ("_d" if disp is not None else "") + ("_c" if comb is not None else ""),
    )(*args, *operands)
    return list(outs)


def _combine_body(dest_ref, dest_next_ref, x_ref, w_ref, gf_ref, ys_ref, *rest, tm, final_norm):
    o_ref, buf_ref, sems = rest[-3:]
    i = pl.program_id(0)
    slot = i % 2

    def gather(d_ref, s):
        def issue(r, carry):
            for kk in range(TOP_K):
                src = d_ref[r * TOP_K + kk]
                pltpu.make_async_copy(ys_ref.at[pl.ds(src, 1)], buf_ref.at[s, kk, pl.ds(r, 1)], sems.at[s]).start()
            return carry

        lax.fori_loop(0, tm, issue, 0, unroll=8)

    @pl.when(i == 0)
    def _():
        gather(dest_ref, 0)

    @pl.when(i + 1 < pl.num_programs(0))
    def _():
        gather(dest_next_ref, 1 - slot)

    for kk in range(TOP_K):
        pltpu.make_async_copy(ys_ref.at[pl.ds(0, tm)], buf_ref.at[slot, kk], sems.at[slot]).wait()
    o_ref[...] = _combine_rows(x_ref, w_ref, buf_ref.at[slot], gf_ref, final_norm)


def moe_combine(x, wgt, plan, ys, out, g_final, final_norm):
    t, d = x.shape
    tm = _tile(plan.ntok, 512)
    body = functools.partial(_combine_body, tm=tm, final_norm=final_norm)
    nsteps = plan.ntok // tm
    tile0 = plan.row0 // tm
    donor = [out] if out is not None else []
    return pl.pallas_call(
        body,
        grid=(nsteps,),
        in_specs=[
            pl.BlockSpec((tm * TOP_K,), lambda i: (i,), memory_space=pltpu.SMEM),
            pl.BlockSpec((tm * TOP_K,), lambda i: (jnp.minimum(i + 1, nsteps - 1),), memory_space=pltpu.SMEM),
            pl.BlockSpec((tm, d), lambda i: (tile0 + i, 0)),
            pl.BlockSpec((tm, TOP_K), lambda i: (tile0 + i, 0)),
            pl.BlockSpec((1, d), lambda i: (0, 0)),
            pl.BlockSpec(memory_space=pl.ANY),
        ] + [pl.BlockSpec(memory_space=pl.ANY)] * len(donor),
        out_specs=pl.BlockSpec((tm, d), lambda i: (tile0 + i, 0)),
        out_shape=jax.ShapeDtypeStruct((t, d), F32),
        scratch_shapes=[pltpu.VMEM((2, TOP_K, tm, d), F32), pltpu.SemaphoreType.DMA((2,))],
        input_output_aliases={6: 0} if donor else {},
        compiler_params=_cparams(("arbitrary",)),
        name="moe_combine",
    )(plan.dest, plan.dest, x, wgt, g_final.reshape(1, d), ys, *donor)


class _GroupPlan(NamedTuple):
    row0: int
    ntok: int
    tr: int
    p: int
    min_tiles: int
    dest: jax.Array
    tile_expert: jax.Array
    num_tiles: jax.Array
    pad_start: jax.Array
    pad_len: jax.Array


def _plan_group(eid, rank, counts, row0, ntok, tr):
    padded = ((counts + tr - 1) // tr) * tr
    ends = jnp.cumsum(padded)
    base = ends - padded
    dest = (base[eid] + rank).reshape(-1).astype(jnp.int32)
    min_tiles = (ntok * TOP_K) // tr
    ntiles_max = min_tiles + N_EXPERTS
    p = ntiles_max * tr
    tile_start = jnp.arange(ntiles_max, dtype=jnp.int32) * tr
    tile_expert = jnp.minimum(jnp.sum(tile_start[:, None] >= ends[None, :], axis=1), N_EXPERTS - 1)
    pad_len = (padded - counts).at[N_EXPERTS - 1].add(p - ends[-1])
    return _GroupPlan(row0, ntok, tr, p, min_tiles, dest, tile_expert.astype(jnp.int32),
                      (ends[-1] // tr).astype(jnp.int32).reshape(1), (base + counts).astype(jnp.int32),
                      pad_len.astype(jnp.int32))


def moe_layer(x, g, router, w_gate_up, w_down, layer, g_final, final_norm, proj=None, donor=None):
    t, d = x.shape
    sizes = (t // 4, t // 2, t // 4) if t % 4096 == 0 else (t,)
    tr = min(512, sizes[0])
    x, eid, rank, wgt, cnt = moe_route(x, g, router, sizes, proj)
    plans, row0 = [], 0
    for k, n in enumerate(sizes):
        plans.append(_plan_group(eid[row0:row0 + n], rank[row0:row0 + n], cnt[k, 0, :N_EXPERTS], row0, n, tr))
        row0 += n
    out = donor if donor is not None or len(plans) == 1 else jnp.zeros_like(x)
    xs = moe_dispatch(x, plans[0].row0, plans[0].ntok, plans[0].dest, plans[0].pad_start, plans[0].pad_len,
                      plans[0].p)
    ys_prev = None
    for k, plan in enumerate(plans):
        disp = plans[k + 1] if k + 1 < len(plans) else None
        comb = plans[k - 1] if k > 0 else None
        res = moe_experts(xs, g, plan, w_gate_up, w_down, layer, disp=disp, comb=comb, x=x, wgt=wgt,
                          ys_comb=ys_prev, out=out, g_final=g_final, final_norm=final_norm)
        ys_prev = res.pop(0)
        if disp is not None:
            xs = res.pop(0)
        if comb is not None:
            out = res.pop(0)
    return moe_combine(x, wgt, plans[-1], ys_prev, out, g_final, final_norm)


def kernel(x, positions, norm_mix, norm_ffn, norm_final, conv_w_in, conv_w, conv_w_out, mla_w_down, mla_q_norm,
           mla_w_uq, mla_kv_norm, mla_w_ukv, mla_w_o, swa_w_qkv, swa_b_qkv, swa_sinks, swa_w_o, swa_b_o,
           ffn_w_gate_up, ffn_w_down, moe_router, moe_w_gate_up, moe_w_down):
    batch, seq, d = x.shape
    depth = norm_mix.shape[0]
    t = batch * seq
    xf = x.reshape(t, d)
    pos_col = positions.reshape(t, 1).astype(F32)
    pos_row = positions.reshape(t // WINDOW, 1, WINDOW).astype(F32)
    zeros_d = jnp.zeros((d,), F32)
    moe_wgu, moe_wd = moe_w_gate_up.astype(BF16), moe_w_down.astype(BF16)
    for i in range(depth):
        kind = i % N_MIXERS
        j = i // N_MIXERS
        proj = None
        x_in = xf
        if kind == 0:
            xf = conv_mixer(xf, norm_mix[i], conv_w_in[j], conv_w[j], conv_w_out[j], seq)
        elif kind == 1:
            q, k, v = mla_project(xf, pos_col, norm_mix[i], mla_w_down[j], mla_q_norm[j], mla_w_uq[j],
                                  mla_kv_norm[j], mla_w_ukv[j])
            proj = (mla_flash(q, k, v, batch, seq), mla_w_o[j], zeros_d)
        else:
            wq, bq = swa_qkv_weights(swa_w_qkv[j], swa_b_qkv[j])
            qkv = rms_linear(xf, norm_mix[i], wq, bq)
            proj = (swa_attention(qkv, swa_sinks[j], pos_col, pos_row, batch, seq), swa_w_o[j], swa_b_o[j])
        f_idx = i // 2
        last = i == depth - 1
        if i % 2 == 0:
            xf = dense_ffn(xf, norm_ffn[i], ffn_w_gate_up[f_idx], ffn_w_down[f_idx], proj)
            if last:
                xf = final_rmsnorm(xf, norm_final)
        else:
            xf = moe_layer(xf, norm_ffn[i], moe_router[f_idx], moe_wgu, moe_wd, f_idx, norm_final, last, proj,
                           donor=x_in if i > 0 else None)
    return xf.reshape(batch, seq, d)


def _final_norm_body(x_ref, g_ref, o_ref):
    o_ref[...] = _rms(x_ref[...], g_ref[...])


def final_rmsnorm(x, g):
    t, d = x.shape
    tm = _tile(t, 1024)
    return pl.pallas_call(
        _final_norm_body,
        grid=(t // tm,),
        in_specs=[pl.BlockSpec((tm, d), lambda i: (i, 0)), _const_spec((1, d))],
        out_specs=pl.BlockSpec((tm, d), lambda i: (i, 0)),
        out_shape=jax.ShapeDtypeStruct((t, d), F32),
        compiler_params=_cparams(("arbitrary",)),
        name="final_rmsnorm",
    )(x, g.reshape(1, d))
```

```python
import functools
from typing import NamedTuple

import jax
import jax.numpy as jnp
from jax import lax
from jax.experimental import pallas as pl
from jax.experimental.pallas import tpu as pltpu

F32 = jnp.float32
BF16 = jnp.bfloat16

RMS_EPS = 1e-6
NEG_INF = -1e30
CONV_WIDTH = 3
MLA_HEADS = 8
MLA_Q_LORA = 384
MLA_KV_LORA = 256
MLA_NOPE = 128
MLA_ROPE = 64
MLA_V = 128
MLA_QK_PAD = 256
LOG2E = 1.4426950408889634
MLA_Q_SCALE = float((MLA_NOPE + MLA_ROPE) ** -0.5 * LOG2E)
ROPE_THETA = 10000.0
SWA_Q_HEADS = 16
SWA_KV_HEADS = 2
SWA_GROUP = SWA_Q_HEADS // SWA_KV_HEADS
SWA_HEAD_DIM = 64
WINDOW = 128
SWA_Q_SCALE = float(SWA_HEAD_DIM ** -0.5 * LOG2E)
SWA_MASKED_DIST = 1e32
N_EXPERTS = 8
TOP_K = 2
N_MIXERS = 3

VMEM_LIMIT = 56 * 1024 * 1024


def _cparams(sem):
    return pltpu.CompilerParams(dimension_semantics=sem, vmem_limit_bytes=VMEM_LIMIT)


def _rms(xf, g):
    ms = jnp.mean(xf * xf, axis=-1, keepdims=True)
    return xf * lax.rsqrt(ms + RMS_EPS) * g


def _const_spec(shape):
    nd = len(shape)
    return pl.BlockSpec(shape, lambda *_: (0,) * nd)


def _resident_spec(shape):
    nd = len(shape)
    return pl.BlockSpec(shape, lambda *_: (0,) * nd, pipeline_mode=pl.Buffered(1))


def _tile(n, pref):
    t = min(n, pref)
    assert n % t == 0, (n, t)
    return t


def _conv_body(x_ref, g_ref, win_ref, wc_ref, wout_ref, o_ref, vpad_ref, *, tm, d, tiles_per_seq, nsub):
    i = pl.program_id(0)
    ts = tm // nsub

    @pl.when(i % tiles_per_seq == 0)
    def _():
        vpad_ref[0:8, :] = jnp.zeros((8, d), F32)

    def in_proj(r):
        h = _rms(x_ref[r * ts:(r + 1) * ts, :], g_ref[...]).astype(BF16)
        return tuple(jnp.dot(h, win_ref[:, c * d:(c + 1) * d], preferred_element_type=F32) for c in range(3))

    nxt = in_proj(0)
    for r in range(nsub):
        bg, cg, u = nxt
        if r + 1 < nsub:
            nxt = in_proj(r + 1)
        v = cg * u
        lo = r * ts
        vpad_ref[8 + lo:8 + lo + ts, :] = v
        conv = vpad_ref[6 + lo:6 + lo + ts, :] * wc_ref[0:1, :]
        conv = conv + vpad_ref[7 + lo:7 + lo + ts, :] * wc_ref[1:2, :]
        conv = conv + v * wc_ref[2:3, :]
        y = (bg * conv).astype(BF16)
        o_ref[lo:lo + ts, :] = x_ref[lo:lo + ts, :] + jnp.dot(y, wout_ref[...], preferred_element_type=F32)
    vpad_ref[0:8, :] = vpad_ref[tm:tm + 8, :]


def conv_mixer(x, g, w_in, w_conv, w_out, seq):
    t, d = x.shape
    tm = _tile(seq, 512)
    body = functools.partial(_conv_body, tm=tm, d=d, tiles_per_seq=seq // tm, nsub=2 if tm % 256 == 0 else 1)
    return pl.pallas_call(
        body,
        grid=(t // tm,),
        in_specs=[
            pl.BlockSpec((tm, d), lambda i: (i, 0)),
            _const_spec((1, d)),
            _resident_spec((d, 3 * d)),
            _const_spec((CONV_WIDTH, d)),
            _resident_spec((d, d)),
        ],
        out_specs=pl.BlockSpec((tm, d), lambda i: (i, 0)),
        out_shape=jax.ShapeDtypeStruct((t, d), F32),
        scratch_shapes=[pltpu.VMEM((tm + 8, d), F32)],
        compiler_params=_cparams(("arbitrary",)),
        name="conv_mixer",
    )(x, g.reshape(1, d), w_in.astype(BF16), w_conv, w_out.astype(BF16))


def _swiglu(h, wgu_cols, wd, a_ref, dff, tf, per_chunk=None):
    def gate_up(c):
        gate = jnp.dot(h, wgu_cols(c * tf, (c + 1) * tf), preferred_element_type=F32)
        up = jnp.dot(h, wgu_cols(dff + c * tf, dff + (c + 1) * tf), preferred_element_type=F32)
        return gate, up

    nxt = gate_up(0)
    for c in range(dff // tf):
        gate, up = nxt
        if c + 1 < dff // tf:
            nxt = gate_up(c + 1)
        if per_chunk is not None:
            per_chunk(c, dff // tf, gate)
        a_ref[:, c * tf:(c + 1) * tf] = (gate * jax.nn.sigmoid(gate) * up).astype(BF16)
    return jnp.dot(a_ref[...], wd, preferred_element_type=F32)


def _mixer_out(x_ref, proj_refs):
    if not proj_refs:
        return x_ref[...]
    a_ref, w_ref, b_ref = proj_refs
    return (x_ref[...] + b_ref[...]) + jnp.dot(a_ref[...], w_ref[...], preferred_element_type=F32)


def _proj_specs(proj, tm, d):
    if proj is None:
        return [], []
    a, w, b = proj
    k = a.shape[1]
    specs = [pl.BlockSpec((tm, k), lambda i: (i, 0)), _resident_spec((k, d)), _const_spec((1, d))]
    return [a, w.astype(BF16), b.reshape(1, d).astype(F32)], specs


def _ffn_body(*refs, dff, tf, fused_proj):
    x_ref, *proj_refs = refs[:4] if fused_proj else refs[:1]
    g_ref, wgu_ref, wd_ref, o_ref, a_ref = refs[4:] if fused_proj else refs[1:]
    x = _mixer_out(x_ref, proj_refs)
    h = _rms(x, g_ref[...]).astype(BF16)
    o_ref[...] = x + _swiglu(h, lambda lo, hi: wgu_ref[:, lo:hi], wd_ref[...], a_ref, dff, tf)


def _ff_chunk(dff, pref):
    best = None
    for c in range(128, dff + 1, 128):
        if dff % c == 0 and c <= pref:
            best = c
    return best if best is not None else dff


def dense_ffn(x, g, w_gate_up, w_down, proj=None):
    t, d = x.shape
    dff = w_down.shape[0]
    tm = _tile(t, 512)
    tf = _ff_chunk(dff, 768)
    body = functools.partial(_ffn_body, dff=dff, tf=tf, fused_proj=proj is not None)
    proj_args, proj_specs = _proj_specs(proj, tm, d)
    return pl.pallas_call(
        body,
        grid=(t // tm,),
        in_specs=[pl.BlockSpec((tm, d), lambda i: (i, 0))] + proj_specs + [
            _const_spec((1, d)),
            _resident_spec((d, 2 * dff)),
            _resident_spec((dff, d)),
        ],
        out_specs=pl.BlockSpec((tm, d), lambda i: (i, 0)),
        out_shape=jax.ShapeDtypeStruct((t, d), F32),
        scratch_shapes=[pltpu.VMEM((tm, dff), BF16)],
        compiler_params=_cparams(("arbitrary",)),
        name="dense_ffn",
    )(x, *proj_args, g.reshape(1, d), w_gate_up.astype(BF16), w_down.astype(BF16))


def _rmslin_body(x_ref, g_ref, w_ref, b_ref, o_ref):
    h = _rms(x_ref[...], g_ref[...]).astype(BF16)
    o_ref[...] = (jnp.dot(h, w_ref[...], preferred_element_type=F32) + b_ref[...]).astype(o_ref.dtype)


def rms_linear(x, g, w, b):
    t, d = x.shape
    n = w.shape[1]
    tm = _tile(t, 512)
    return pl.pallas_call(
        _rmslin_body,
        grid=(t // tm,),
        in_specs=[
            pl.BlockSpec((tm, d), lambda i: (i, 0)),
            _const_spec((1, d)),
            _const_spec((d, n)),
            _const_spec((1, n)),
        ],
        out_specs=pl.BlockSpec((tm, n), lambda i: (i, 0)),
        out_shape=jax.ShapeDtypeStruct((t, n), BF16),
        compiler_params=_cparams(("arbitrary",)),
        name="rms_linear",
    )(x, g.reshape(1, d), w.astype(BF16), b.reshape(1, n).astype(F32))


def _mla_proj_body(x_ref, pos_ref, g_ref, wd_ref, qn_ref, kvn_ref, wqa_ref, wqb_ref, wkv_ref, frq_ref,
                   q_ref, k_ref, v_ref):
    h = _rms(x_ref[...], g_ref[...]).astype(BF16)
    down = jnp.dot(h, wd_ref[...], preferred_element_type=F32)
    c_q = down[:, 0:MLA_Q_LORA]
    c_kv = down[:, MLA_Q_LORA:MLA_Q_LORA + MLA_KV_LORA]
    o = MLA_Q_LORA + MLA_KV_LORA
    kpe = down[:, o:o + 128]
    kpe_sw = down[:, o + 128:o + 256]
    hq = _rms(c_q, qn_ref[...]).astype(BF16)
    hkv = _rms(c_kv, kvn_ref[...]).astype(BF16)
    qa = jnp.dot(hq, wqa_ref[...], preferred_element_type=F32)
    qb = jnp.dot(hq, wqb_ref[...], preferred_element_type=F32)
    kv = jnp.dot(hkv, wkv_ref[...], preferred_element_type=F32)
    ang = pos_ref[...] * frq_ref[0:1, :]
    cos = jnp.cos(ang)
    sin = jnp.sin(ang) * frq_ref[1:2, :]
    kpe_r = (kpe * cos + kpe_sw * sin).astype(BF16)
    ones_col = jnp.where(lax.broadcasted_iota(jnp.int32, cos.shape, 1) == 0, 1.0, 0.0).astype(BF16)
    for hd in range(MLA_HEADS):
        qo = hd * MLA_QK_PAD
        q_ref[hd, :, 0:MLA_NOPE] = qa[:, qo:qo + MLA_NOPE].astype(BF16)
        qpe = qa[:, qo + MLA_NOPE:qo + MLA_QK_PAD] * cos + qb[:, hd * 128:(hd + 1) * 128] * sin
        q_ref[hd, :, MLA_NOPE:MLA_QK_PAD] = qpe.astype(BF16)
        ko = hd * (MLA_NOPE + MLA_V)
        k_ref[hd, :, 0:MLA_NOPE] = kv[:, ko:ko + MLA_NOPE].astype(BF16)
        k_ref[hd, :, MLA_NOPE:MLA_QK_PAD] = kpe_r
        v_ref[hd, :, 0:MLA_V] = kv[:, ko + MLA_NOPE:ko + MLA_NOPE + MLA_V].astype(BF16)
        v_ref[hd, :, MLA_V:2 * MLA_V] = ones_col


def _rope_swap(w):
    half = MLA_ROPE // 2
    return jnp.concatenate([w[..., half:], w[..., :half]], axis=-1)


def mla_project(x, pos_col, g, w_down, q_norm, w_uq, kv_norm, w_ukv):
    t, d = x.shape
    tm = _tile(t, 512)
    hn = MLA_HEADS
    z64 = jnp.zeros((d, 64), F32)
    o = MLA_Q_LORA + MLA_KV_LORA
    kpe_w = w_down[:, o:o + MLA_ROPE]
    wd = jnp.concatenate([w_down[:, :o], kpe_w, z64, _rope_swap(kpe_w), z64], axis=1).astype(BF16)
    wq = (w_uq * MLA_Q_SCALE).reshape(MLA_Q_LORA, hn, MLA_NOPE + MLA_ROPE)
    zq = jnp.zeros((MLA_Q_LORA, hn, 64), F32)
    wqa = jnp.concatenate([wq, zq], axis=-1).reshape(MLA_Q_LORA, hn * MLA_QK_PAD).astype(BF16)
    wqb = jnp.concatenate([_rope_swap(wq[..., MLA_NOPE:]), zq], axis=-1).reshape(MLA_Q_LORA, hn * 128).astype(BF16)
    inv = ROPE_THETA ** (-jnp.arange(0, MLA_ROPE, 2, dtype=F32) / MLA_ROPE)
    half = MLA_ROPE // 2
    frq = jnp.stack([
        jnp.concatenate([inv, inv, jnp.zeros((64,), F32)]),
        jnp.concatenate([-jnp.ones((half,), F32), jnp.ones((half,), F32), jnp.zeros((64,), F32)]),
    ])
    nd = wd.shape[1]
    outs = pl.pallas_call(
        _mla_proj_body,
        grid=(t // tm,),
        in_specs=[
            pl.BlockSpec((tm, d), lambda i: (i, 0)),
            pl.BlockSpec((tm, 1), lambda i: (i, 0)),
            _const_spec((1, d)),
            _resident_spec((d, nd)),
            _const_spec((1, MLA_Q_LORA)),
            _const_spec((1, MLA_KV_LORA)),
            _resident_spec((MLA_Q_LORA, hn * MLA_QK_PAD)),
            _resident_spec((MLA_Q_LORA, hn * 128)),
            _resident_spec((MLA_KV_LORA, hn * (MLA_NOPE + MLA_V))),
            _const_spec((2, 128)),
        ],
        out_specs=[
            pl.BlockSpec((hn, tm, MLA_QK_PAD), lambda i: (0, i, 0)),
            pl.BlockSpec((hn, tm, MLA_QK_PAD), lambda i: (0, i, 0)),
            pl.BlockSpec((hn, tm, 2 * MLA_V), lambda i: (0, i, 0)),
        ],
        out_shape=[
            jax.ShapeDtypeStruct((hn, t, MLA_QK_PAD), BF16),
            jax.ShapeDtypeStruct((hn, t, MLA_QK_PAD), BF16),
            jax.ShapeDtypeStruct((hn, t, 2 * MLA_V), BF16),
        ],
        compiler_params=_cparams(("arbitrary",)),
        name="mla_project",
    )(x, pos_col, g.reshape(1, d), wd, q_norm.reshape(1, -1), kv_norm.reshape(1, -1), wqa, wqb,
      w_ukv.astype(BF16), frq)
    return outs


def _flash_body(qi_ref, kj_ref, q_ref, k_ref, v_ref, o_ref, m_ref, acc_ref, *, tq, tqs, hp, dv):
    pair = pl.program_id(2)
    i = qi_ref[pair]
    j = kj_ref[pair]

    @pl.when(j == 0)
    def _():
        m_ref[...] = jnp.full_like(m_ref, NEG_INF)
        acc_ref[...] = jnp.zeros_like(acc_ref)

    def step(diagonal):
        nsub = tq // tqs
        units = [(h, r) for h in range(hp) for r in range(nsub)]
        causal = (lax.broadcasted_iota(jnp.int32, (tqs, tqs), 1) <= lax.broadcasted_iota(jnp.int32, (tqs, tqs), 0))

        def scores(u):
            h, r = u
            nk = (r + 1) * tqs if diagonal else tq
            return lax.dot_general(q_ref[h, r * tqs:(r + 1) * tqs, :], k_ref[h, 0:nk, :], (((1,), (1,)), ((), ())),
                                   preferred_element_type=F32)

        s_next = scores(units[0])
        for n, (h, r) in enumerate(units):
            rows = slice(r * tqs, (r + 1) * tqs)
            nk = (r + 1) * tqs if diagonal else tq
            s = s_next
            if n + 1 < len(units):
                s_next = scores(units[n + 1])
            if diagonal:
                s_diag = jnp.where(causal, s[:, nk - tqs:nk], NEG_INF)
                s = s_diag if r == 0 else jnp.concatenate([s[:, 0:nk - tqs], s_diag], axis=1)
            m_prev = m_ref[h, rows, :]
            m_new = jnp.maximum(m_prev, jnp.max(s, axis=-1, keepdims=True))
            alpha = jnp.exp2(m_prev - m_new)
            p = jnp.exp2(s - m_new)
            pv = jnp.dot(p.astype(BF16), v_ref[h, 0:nk, :], preferred_element_type=F32)
            acc_ref[h, rows, :] = alpha * acc_ref[h, rows, :] + pv
            m_ref[h, rows, :] = m_new

    @pl.when(j < i)
    def _():
        step(False)

    @pl.when(j == i)
    def _():
        step(True)
        for h in range(hp):
            o_ref[:, h * dv:(h + 1) * dv] = (acc_ref[h, :, 0:dv] / acc_ref[h, :, dv:dv + 1]).astype(o_ref.dtype)


def mla_flash(q, k, v, batch, seq):
    hn, t, dq = q.shape
    dvp = v.shape[2]
    dv = MLA_V
    tq = tk = _tile(seq, 1024)
    nq = nk = seq // tq
    hp = 4
    body = functools.partial(_flash_body, tq=tq, tqs=_tile(tq, 256), hp=hp, dv=dv)
    pairs = [(i, j) for i in range(nq) for j in range(i + 1)]
    qi = jnp.asarray([p[0] for p in pairs], jnp.int32)
    kj = jnp.asarray([p[1] for p in pairs], jnp.int32)
    grid_spec = pltpu.PrefetchScalarGridSpec(
        num_scalar_prefetch=2,
        grid=(batch, hn // hp, len(pairs)),
        in_specs=[
            pl.BlockSpec((hp, tq, dq), lambda b, h, p, qi, kj: (h, b * nq + qi[p], 0)),
            pl.BlockSpec((hp, tk, dq), lambda b, h, p, qi, kj: (h, b * nk + kj[p], 0)),
            pl.BlockSpec((hp, tk, dvp), lambda b, h, p, qi, kj: (h, b * nk + kj[p], 0)),
        ],
        out_specs=pl.BlockSpec((tq, hp * dv), lambda b, h, p, qi, kj: (b * nq + qi[p], h)),
        scratch_shapes=[pltpu.VMEM((hp, tq, 1), F32), pltpu.VMEM((hp, tq, dvp), F32)],
    )
    return pl.pallas_call(
        body,
        grid_spec=grid_spec,
        out_shape=jax.ShapeDtypeStruct((t, hn * dv), BF16),
        compiler_params=_cparams(("arbitrary", "arbitrary", "arbitrary")),
        name="mla_flash",
    )(qi, kj, q, k, v)


def _alibi_slopes():
    return [2.0 ** (-8.0 * (h + 1) / SWA_Q_HEADS) for h in range(SWA_Q_HEADS)]


def _swa_body(sink_ref, q_ref, kc_ref, kp_ref, vc_ref, vp_ref, pq_ref, pkc_ref, pkp_ref, o_ref, *, steps_per_seq, nblk):
    w = WINDOW
    seq_start = pl.program_id(0) % steps_per_seq == 0
    for r in range(nblk):
        cur = slice(r * w, (r + 1) * w)
        prv = slice((r - 1) * w, r * w)
        _swa_block(
            sink_ref, q_ref.at[cur], kc_ref.at[cur], kp_ref if r == 0 else kc_ref.at[prv],
            vc_ref.at[cur], vp_ref if r == 0 else vc_ref.at[prv], pq_ref[cur, :], pkc_ref[r],
            pkp_ref[0] if r == 0 else pkc_ref[r - 1], o_ref.at[cur],
            has_prev=jnp.logical_not(seq_start) if r == 0 else True)


def _swa_block(sink_ref, q_ref, kc_ref, kp_ref, vc_ref, vp_ref, pq, pkc, pkp, o_ref, *, has_prev):
    w = WINDOW
    dist_c = jnp.abs(pq - pkc)
    dist_p = jnp.abs(pq - pkp)
    iq = lax.broadcasted_iota(jnp.int32, (w, w), 0)
    ik = lax.broadcasted_iota(jnp.int32, (w, w), 1)
    dist_c = jnp.where(ik <= iq, dist_c, SWA_MASKED_DIST)
    dist_p = jnp.where(jnp.logical_and(ik > iq, has_prev), dist_p, SWA_MASKED_DIST)
    lane = lax.broadcasted_iota(jnp.int32, (w, 2 * SWA_HEAD_DIM), 1)
    left = lane < SWA_HEAD_DIM
    slopes = _alibi_slopes()
    pairs = SWA_GROUP // 2
    for kh in range(SWA_KV_HEADS):
        kc = kc_ref[:, kh * 128:(kh + 1) * 128]
        kp = kp_ref[:, kh * 128:(kh + 1) * 128]
        vc_l = vc_ref[:, kh * 256:kh * 256 + 128]
        vc_r = vc_ref[:, kh * 256 + 128:kh * 256 + 256]
        vp_l = vp_ref[:, kh * 256:kh * 256 + 128]
        vp_r = vp_ref[:, kh * 256 + 128:kh * 256 + 256]
        zero = jnp.zeros((w, 128), BF16)
        rows = []
        for half in range(2):
            for p in range(pairs):
                qp = q_ref[:, (kh * pairs + p) * 128:(kh * pairs + p + 1) * 128]
                rows.append(jnp.where(left if half == 0 else jnp.logical_not(left), qp, zero))
        qs = jnp.concatenate(rows, axis=0)
        dn = (((1,), (1,)), ((), ()))
        s_c = lax.dot_general(qs, kc, dn, preferred_element_type=F32)
        s_p = lax.dot_general(qs, kp, dn, preferred_element_type=F32)
        pc_rows, pp_rows = [], []
        for half in range(2):
            for p in range(pairs):
                hd = kh * SWA_GROUP + 2 * p + half
                r0 = (half * pairs + p) * w
                sink = sink_ref[hd] * LOG2E
                a_c = s_c[r0:r0 + w] - (slopes[hd] * LOG2E) * dist_c
                a_p = s_p[r0:r0 + w] - (slopes[hd] * LOG2E) * dist_p
                m = jnp.maximum(jnp.max(jnp.maximum(a_c, a_p), axis=-1, keepdims=True), sink)
                e_c = jnp.exp2(a_c - m)
                e_p = jnp.exp2(a_p - m)
                den = jnp.sum(e_c + e_p, axis=-1, keepdims=True) + jnp.exp2(sink - m)
                inv = 1.0 / den
                pc_rows.append((e_c * inv).astype(BF16))
                pp_rows.append((e_p * inv).astype(BF16))
        hw = pairs * w
        pc = jnp.concatenate(pc_rows, axis=0)
        pp = jnp.concatenate(pp_rows, axis=0)
        o_pair = (jnp.dot(pc[0:hw], vc_l, preferred_element_type=F32)
                  + jnp.dot(pp[0:hw], vp_l, preferred_element_type=F32)
                  + jnp.dot(pc[hw:2 * hw], vc_r, preferred_element_type=F32)
                  + jnp.dot(pp[hw:2 * hw], vp_r, preferred_element_type=F32))
        for p in range(pairs):
            c0 = (kh * pairs + p) * 128
            o_ref[:, c0:c0 + 128] = o_pair[p * w:(p + 1) * w].astype(o_ref.dtype)


def swa_attention(qkv, sinks, pos_col, pos_row, batch, seq):
    t = qkv.shape[0]
    w = WINDOW
    nblk = 4 if (seq // w) % 4 == 0 else 1
    tq = nblk * w
    steps_per_seq = seq // tq
    nq = SWA_Q_HEADS * SWA_HEAD_DIM
    body = functools.partial(_swa_body, steps_per_seq=steps_per_seq, nblk=nblk)

    def prev(i):
        return jnp.where(i % steps_per_seq == 0, i * nblk, i * nblk - 1)

    kcol = (nq + 4 * 128) // 256
    return pl.pallas_call(
        body,
        grid=(t // tq,),
        in_specs=[
            pl.BlockSpec(memory_space=pltpu.SMEM),
            pl.BlockSpec((tq, nq), lambda i: (i, 0)),
            pl.BlockSpec((tq, 256), lambda i: (i, kcol)),
            pl.BlockSpec((w, 256), lambda i: (prev(i), kcol)),
            pl.BlockSpec((tq, 512), lambda i: (i, nq // 512)),
            pl.BlockSpec((w, 512), lambda i: (prev(i), nq // 512)),
            pl.BlockSpec((tq, 1), lambda i: (i, 0)),
            pl.BlockSpec((nblk, 1, w), lambda i: (i, 0, 0)),
            pl.BlockSpec((1, 1, w), lambda i: (prev(i), 0, 0)),
        ],
        out_specs=pl.BlockSpec((tq, nq), lambda i: (i, 0)),
        out_shape=jax.ShapeDtypeStruct((t, nq), BF16),
        compiler_params=_cparams(("arbitrary",)),
        name="swa_attention",
    )(sinks.astype(F32), qkv, qkv, qkv, qkv, qkv, pos_col, pos_row, pos_row)


def swa_qkv_weights(w_qkv, b_qkv):
    nq = SWA_Q_HEADS * SWA_HEAD_DIM
    nk = SWA_KV_HEADS * SWA_HEAD_DIM
    hd = SWA_HEAD_DIM

    def arrange(m):
        q = m[..., :nq] * SWA_Q_SCALE
        k = m[..., nq:nq + nk]
        v = m[..., nq + nk:]
        z = jnp.zeros(m.shape[:-1] + (hd,), m.dtype)
        parts = [q]
        for kh in range(SWA_KV_HEADS):
            vh = v[..., kh * hd:(kh + 1) * hd]
            parts += [vh, z, z, vh]
        for kh in range(SWA_KV_HEADS):
            kk = k[..., kh * hd:(kh + 1) * hd]
            parts += [kk, kk]
        return jnp.concatenate(parts, axis=-1)

    return arrange(w_qkv), arrange(b_qkv)


def _group_of_step(i, group_starts):
    gid = 0
    for s in group_starts[1:]:
        gid = gid + (i >= s).astype(jnp.int32)
    return gid


def _router_body(*refs, tm, fused_proj, group_starts):
    x_ref, *proj_refs = refs[:4] if fused_proj else refs[:1]
    rest = refs[4:] if fused_proj else refs[1:]
    if fused_proj:
        g_ref, r_ref, x1_ref, eid_ref, rank_ref, wgt_ref, cnt_ref, run_ref = rest
    else:
        g_ref, r_ref, eid_ref, rank_ref, wgt_ref, cnt_ref, run_ref = rest
    i = pl.program_id(0)

    is_start = functools.reduce(jnp.logical_or, [i == s for s in group_starts])

    @pl.when(is_start)
    def _():
        run_ref[...] = jnp.zeros_like(run_ref)

    x = _mixer_out(x_ref, proj_refs)
    if fused_proj:
        x1_ref[...] = x
    h = _rms(x, g_ref[...]).astype(BF16)
    logits = jnp.dot(h, r_ref[...], preferred_element_type=F32)
    lane = lax.broadcasted_iota(jnp.int32, logits.shape, 1)
    logits = jnp.where(lane < N_EXPERTS, logits, -jnp.inf)
    m1 = jnp.max(logits, axis=-1, keepdims=True)
    i1 = jnp.min(jnp.where(logits == m1, lane, 128), axis=-1, keepdims=True)
    rest = jnp.where(lane == i1, -jnp.inf, logits)
    m2 = jnp.max(rest, axis=-1, keepdims=True)
    i2 = jnp.min(jnp.where(rest == m2, lane, 128), axis=-1, keepdims=True)
    e2 = jnp.exp(m2 - m1)
    w1 = 1.0 / (1.0 + e2)
    w2 = e2 / (1.0 + e2)
    oh1 = (lane == i1)
    oh2 = (lane == i2)
    sel = jnp.where(jnp.logical_or(oh1, oh2), 1.0, 0.0).astype(BF16)
    r = lax.broadcasted_iota(jnp.int32, (tm, tm), 0)
    c = lax.broadcasted_iota(jnp.int32, (tm, tm), 1)
    tri = jnp.where(c < r, 1.0, 0.0).astype(BF16)
    before = jnp.dot(tri, sel, preferred_element_type=F32) + run_ref[...]
    rank1 = jnp.sum(jnp.where(oh1, before, 0.0), axis=-1, keepdims=True)
    rank2 = jnp.sum(jnp.where(oh2, before, 0.0), axis=-1, keepdims=True)
    run_ref[...] = run_ref[...] + jnp.sum(sel.astype(F32), axis=0, keepdims=True)
    eid_ref[...] = jnp.where(lane == 0, i1, jnp.where(lane == 1, i2, 0))[:, 0:TOP_K]
    rank_ref[...] = jnp.where(lane == 0, rank1, jnp.where(lane == 1, rank2, 0.0))[:, 0:TOP_K].astype(jnp.int32)
    wgt_ref[...] = jnp.where(lane == 0, w1, jnp.where(lane == 1, w2, 0.0))[:, 0:TOP_K]
    cnt_ref[0] = jnp.broadcast_to(run_ref[...], (8, 128)).astype(jnp.int32)


def moe_route(x, g, router, group_sizes, proj=None):
    t, d = x.shape
    groups = len(group_sizes)
    tm = _tile(min(group_sizes), 512)
    assert all(n % tm == 0 for n in group_sizes) and sum(group_sizes) == t
    group_starts = tuple(sum(group_sizes[:k]) // tm for k in range(groups))
    rpad = jnp.zeros((d, 128), F32).at[:, :N_EXPERTS].set(router).astype(BF16)
    body = functools.partial(_router_body, tm=tm, fused_proj=proj is not None, group_starts=group_starts)
    proj_args, proj_specs = _proj_specs(proj, tm, d)
    x1_spec = [pl.BlockSpec((tm, d), lambda i: (i, 0))] if proj is not None else []
    x1_shape = [jax.ShapeDtypeStruct((t, d), F32)] if proj is not None else []
    outs = pl.pallas_call(
        body,
        grid=(t // tm,),
        in_specs=[pl.BlockSpec((tm, d), lambda i: (i, 0))] + proj_specs + [
            _const_spec((1, d)),
            _const_spec((d, 128)),
        ],
        out_specs=x1_spec + [
            pl.BlockSpec((tm, TOP_K), lambda i: (i, 0)),
            pl.BlockSpec((tm, TOP_K), lambda i: (i, 0)),
            pl.BlockSpec((tm, TOP_K), lambda i: (i, 0)),
            pl.BlockSpec((1, 8, 128), lambda i: (_group_of_step(i, group_starts), 0, 0)),
        ],
        out_shape=x1_shape + [
            jax.ShapeDtypeStruct((t, TOP_K), jnp.int32),
            jax.ShapeDtypeStruct((t, TOP_K), jnp.int32),
            jax.ShapeDtypeStruct((t, TOP_K), F32),
            jax.ShapeDtypeStruct((groups, 8, 128), jnp.int32),
        ],
        scratch_shapes=[pltpu.VMEM((1, 128), F32)],
        compiler_params=_cparams(("arbitrary",)),
        name="moe_route",
    )(x, *proj_args, g.reshape(1, d), rpad)
    return outs if proj is not None else [x] + list(outs)


def _zero_pad_rows(pad_start_ref, pad_len_ref, xs_ref, zrow_ref, zsem):
    zrow_ref[...] = jnp.zeros_like(zrow_ref)
    for e in range(N_EXPERTS):
        def zero_row(r):
            return pltpu.make_async_copy(zrow_ref.at[pl.ds(0, 1)], xs_ref.at[pl.ds(pad_start_ref[e] + r, 1)], zsem)

        def zissue(r, carry):
            zero_row(r).start()
            return carry

        def zwait(r, carry):
            zero_row(r).wait()
            return carry

        lax.fori_loop(0, pad_len_ref[e], zissue, 0)
        lax.fori_loop(0, pad_len_ref[e], zwait, 0)


def _for_rows(rows, fn, inline):
    lo, hi = rows
    if inline:
        for r in range(lo, hi):
            fn(r)
    else:
        lax.fori_loop(lo, hi, lambda r, c: (fn(r), c)[1], 0, unroll=8)


def _scatter_rows(dest_ref, x_ref, xs_ref, sem, rows, inline, anchor=0):
    def issue(r):
        for kk in range(TOP_K):
            dst = dest_ref[r * TOP_K + kk] + anchor
            pltpu.make_async_copy(x_ref.at[pl.ds(r, 1)], xs_ref.at[pl.ds(dst, 1)], sem).start()

    _for_rows(rows, issue, inline)


def _issue_anchor(v):
    bits = lax.bitcast_convert_type(v[0, 0], jnp.int32)
    return jnp.where((bits | 1) == 0, 1, 0)


def _scatter_wait(x_ref, xs_ref, sem, n):
    for kk in range(TOP_K):
        pltpu.make_async_copy(x_ref, xs_ref.at[pl.ds(0, n)], sem).wait()


def _gather_rows(dest_ref, ys_ref, buf_ref, sem, rows, inline, anchor=0):
    def issue(r):
        for kk in range(TOP_K):
            src = dest_ref[r * TOP_K + kk] + anchor
            pltpu.make_async_copy(ys_ref.at[pl.ds(src, 1)], buf_ref.at[kk, pl.ds(r, 1)], sem).start()

    _for_rows(rows, issue, inline)


def _gather_wait(ys_ref, buf_ref, sem, n):
    for kk in range(TOP_K):
        pltpu.make_async_copy(ys_ref.at[pl.ds(0, n)], buf_ref.at[kk], sem).wait()


def _dispatch_body(pad_start_ref, pad_len_ref, dest_ref, x_ref, xs_ref, zrow_ref, sem, zsem, *, tm):
    @pl.when(pl.program_id(0) == 0)
    def _():
        _zero_pad_rows(pad_start_ref, pad_len_ref, xs_ref, zrow_ref, zsem)

    _scatter_rows(dest_ref, x_ref, xs_ref, sem, (0, tm), inline=False)
    _scatter_wait(x_ref, xs_ref, sem, tm)


def moe_dispatch(x, row0, ntok, dest_flat, pad_start, pad_len, p):
    d = x.shape[1]
    tm = _tile(ntok, 512)
    tile0 = row0 // tm
    body = functools.partial(_dispatch_body, tm=tm)
    grid_spec = pltpu.PrefetchScalarGridSpec(
        num_scalar_prefetch=2,
        grid=(ntok // tm,),
        in_specs=[
            pl.BlockSpec((tm * TOP_K,), lambda i, ps, pn: (i,), memory_space=pltpu.SMEM),
            pl.BlockSpec((tm, d), lambda i, ps, pn: (tile0 + i, 0)),
        ],
        out_specs=pl.BlockSpec(memory_space=pl.ANY),
        scratch_shapes=[pltpu.VMEM((8, d), F32), pltpu.SemaphoreType.DMA(()), pltpu.SemaphoreType.DMA(())],
    )
    return pl.pallas_call(
        body,
        grid_spec=grid_spec,
        out_shape=jax.ShapeDtypeStruct((p, d), F32),
        compiler_params=_cparams(("arbitrary",)),
        name="moe_dispatch",
    )(pad_start, pad_len, dest_flat, x)


def _expert_body(*refs, dff, tf, ts_d, ts_c, nside, final_norm):
    refs = list(refs)
    te_ref, nt_ref = refs[:2]
    del refs[:2]
    if ts_d:
        pad_start_ref, pad_len_ref = refs[:2]
        del refs[:2]
    xs_ref, g_ref, wgu_ref, wd_ref = refs[:4]
    del refs[:4]
    if ts_d:
        dest_d_ref, x_d_ref = refs[:2]
        del refs[:2]
    if ts_c:
        dest_c_ref, x_c_ref, w_c_ref, gf_ref, ys_c_ref, _ = refs[:6]
        del refs[:6]
    ys_ref = refs.pop(0)
    if ts_d:
        xs_d_ref = refs.pop(0)
    if ts_c:
        out_ref = refs.pop(0)
    a_ref = refs.pop(0)
    if ts_d:
        zrow_ref, sem_d, zsem = refs[:3]
        del refs[:3]
    if ts_c:
        buf_ref, sem_c = refs[:2]
    i = pl.program_id(0)

    def experts(per_chunk=None):
        h = _rms(xs_ref[...], g_ref[...]).astype(BF16)
        ys_ref[...] = _swiglu(h, lambda lo, hi: wgu_ref[0, 0, :, lo:hi], wd_ref[0, 0], a_ref, dff, tf, per_chunk)

    def side_rows(c, nchunks, gate):
        anchor = _issue_anchor(gate)
        if ts_d:
            _scatter_rows(dest_d_ref, x_d_ref, xs_d_ref, sem_d, (c * ts_d // nchunks, (c + 1) * ts_d // nchunks),
                          inline=True, anchor=anchor)
        if ts_c:
            _gather_rows(dest_c_ref, ys_c_ref, buf_ref, sem_c, (c * ts_c // nchunks, (c + 1) * ts_c // nchunks),
                         inline=True, anchor=anchor)

    if ts_d:
        @pl.when(i == 0)
        def _():
            _zero_pad_rows(pad_start_ref, pad_len_ref, xs_d_ref, zrow_ref, zsem)

    if not (ts_d or ts_c):
        pl.when(i < nt_ref[0])(experts)
    else:
        @pl.when(i < nside)
        def _():
            experts(side_rows)
            if ts_d:
                _scatter_wait(x_d_ref, xs_d_ref, sem_d, ts_d)
            if ts_c:
                _gather_wait(ys_c_ref, buf_ref, sem_c, ts_c)
                out_ref[...] = _combine_rows(x_c_ref, w_c_ref, buf_ref, gf_ref, final_norm)

        pl.when(jnp.logical_and(i >= nside, i < nt_ref[0]))(experts)

    @pl.when(i >= nt_ref[0])
    def _():
        ys_ref[...] = jnp.zeros_like(ys_ref)


def _combine_rows(x_ref, w_ref, buf_ref, gf_ref, final_norm):
    w = w_ref[...]
    out = x_ref[...] + (w[:, 0:1] * buf_ref[0] + w[:, 1:2] * buf_ref[1])
    return _rms(out, gf_ref[...]) if final_norm else out


def moe_experts(xs, g, plan, w_gate_up, w_down, layer, disp=None, comb=None, x=None, wgt=None, ys_comb=None,
                out=None, g_final=None, final_norm=False):
    p, d = xs.shape
    tr = plan.tr
    dff = w_down.shape[2]
    tf = _ff_chunk(dff, 512)
    nside = plan.min_tiles
    ts_d = disp.ntok // nside if disp is not None else 0
    ts_c = comb.ntok // nside if comb is not None else 0
    body = functools.partial(_expert_body, dff=dff, tf=tf, ts_d=ts_d, ts_c=ts_c, nside=nside, final_norm=final_norm)

    def row(i, nt):
        return jnp.minimum(i, nt[0] - 1)

    def imap(fn):
        return lambda i, *pf: fn(i, pf[0], pf[1])

    def side_specs(o, ts):
        tile0 = o.row0 // ts

        def tok_map(i, te, nt):
            return (tile0 + jnp.minimum(i, nside - 1), 0)

        dest_spec = pl.BlockSpec((ts * TOP_K,), imap(lambda i, te, nt: (jnp.minimum(i, nside - 1),)),
                                 memory_space=pltpu.SMEM)
        return dest_spec, (lambda width: pl.BlockSpec((ts, width), imap(tok_map)))

    in_specs = [
        pl.BlockSpec((tr, d), imap(lambda i, te, nt: (row(i, nt), 0))),
        pl.BlockSpec((1, d), imap(lambda i, te, nt: (0, 0))),
        pl.BlockSpec((1, 1, d, 2 * dff), imap(lambda i, te, nt: (layer, te[row(i, nt)], 0, 0)),
                     pipeline_mode=pl.Buffered(1)),
        pl.BlockSpec((1, 1, dff, d), imap(lambda i, te, nt: (layer, te[row(i, nt)], 0, 0)),
                     pipeline_mode=pl.Buffered(1)),
    ]
    out_specs = [pl.BlockSpec((tr, d), imap(lambda i, te, nt: (i, 0)))]
    out_shape = [jax.ShapeDtypeStruct((p, d), F32)]
    scratch = [pltpu.VMEM((tr, dff), BF16)]
    args = [plan.tile_expert, plan.num_tiles]
    operands = [xs, g.reshape(1, d), w_gate_up, w_down]
    aliases = {}
    if disp is not None:
        dest_spec, tok_spec = side_specs(disp, ts_d)
        args += [disp.pad_start, disp.pad_len]
        in_specs += [dest_spec, tok_spec(d)]
        operands += [disp.dest, x]
        out_specs.append(pl.BlockSpec(memory_space=pl.ANY))
        out_shape.append(jax.ShapeDtypeStruct((disp.p, d), F32))
        scratch += [pltpu.VMEM((8, d), F32), pltpu.SemaphoreType.DMA(()), pltpu.SemaphoreType.DMA(())]
    if comb is not None:
        dest_spec, tok_spec = side_specs(comb, ts_c)
        in_specs += [dest_spec, tok_spec(d), tok_spec(TOP_K), pl.BlockSpec((1, d), imap(lambda i, te, nt: (0, 0))),
                     pl.BlockSpec(memory_space=pl.ANY), pl.BlockSpec(memory_space=pl.ANY)]
        operands += [comb.dest, x, wgt, g_final.reshape(1, d), ys_comb, out]
        aliases = {len(args) + len(operands) - 1: len(out_specs)}
        out_specs.append(tok_spec(d))
        out_shape.append(jax.ShapeDtypeStruct(out.shape, F32))
        scratch += [pltpu.VMEM((TOP_K, ts_c, d), F32), pltpu.SemaphoreType.DMA(())]
    grid_spec = pltpu.PrefetchScalarGridSpec(
        num_scalar_prefetch=len(args),
        grid=(p // tr,),
        in_specs=in_specs,
        out_specs=out_specs,
        scratch_shapes=scratch,
    )
    outs = pl.pallas_call(
        body,
        grid_spec=grid_spec,
        out_shape=out_shape,
        input_output_aliases=aliases,
        compiler_params=_cparams(("arbitrary",)),
        name="moe_experts" + ("_d" if disp is not None else "") + ("_c" if comb is not None else ""),
    )(*args, *operands)
    return list(outs)


def _combine_body(dest_ref, dest_next_ref, x_ref, w_ref, gf_ref, ys_ref, *rest, tm, final_norm):
    o_ref, buf_ref, sems = rest[-3:]
    i = pl.program_id(0)
    slot = i % 2

    def gather(d_ref, s):
        def issue(r, carry):
            for kk in range(TOP_K):
                src = d_ref[r * TOP_K + kk]
                pltpu.make_async_copy(ys_ref.at[pl.ds(src, 1)], buf_ref.at[s, kk, pl.ds(r, 1)], sems.at[s]).start()
            return carry

        lax.fori_loop(0, tm, issue, 0, unroll=8)

    @pl.when(i == 0)
    def _():
        gather(dest_ref, 0)

    @pl.when(i + 1 < pl.num_programs(0))
    def _():
        gather(dest_next_ref, 1 - slot)

    for kk in range(TOP_K):
        pltpu.make_async_copy(ys_ref.at[pl.ds(0, tm)], buf_ref.at[slot, kk], sems.at[slot]).wait()
    o_ref[...] = _combine_rows(x_ref, w_ref, buf_ref.at[slot], gf_ref, final_norm)


def moe_combine(x, wgt, plan, ys, out, g_final, final_norm):
    t, d = x.shape
    tm = _tile(plan.ntok, 512)
    body = functools.partial(_combine_body, tm=tm, final_norm=final_norm)
    nsteps = plan.ntok // tm
    tile0 = plan.row0 // tm
    donor = [out] if out is not None else []
    return pl.pallas_call(
        body,
        grid=(nsteps,),
        in_specs=[
            pl.BlockSpec((tm * TOP_K,), lambda i: (i,), memory_space=pltpu.SMEM),
            pl.BlockSpec((tm * TOP_K,), lambda i: (jnp.minimum(i + 1, nsteps - 1),), memory_space=pltpu.SMEM),
            pl.BlockSpec((tm, d), lambda i: (tile0 + i, 0)),
            pl.BlockSpec((tm, TOP_K), lambda i: (tile0 + i, 0)),
            pl.BlockSpec((1, d), lambda i: (0, 0)),
            pl.BlockSpec(memory_space=pl.ANY),
        ] + [pl.BlockSpec(memory_space=pl.ANY)] * len(donor),
        out_specs=pl.BlockSpec((tm, d), lambda i: (tile0 + i, 0)),
        out_shape=jax.ShapeDtypeStruct((t, d), F32),
        scratch_shapes=[pltpu.VMEM((2, TOP_K, tm, d), F32), pltpu.SemaphoreType.DMA((2,))],
        input_output_aliases={6: 0} if donor else {},
        compiler_params=_cparams(("arbitrary",)),
        name="moe_combine",
    )(plan.dest, plan.dest, x, wgt, g_final.reshape(1, d), ys, *donor)


class _GroupPlan(NamedTuple):
    row0: int
    ntok: int
    tr: int
    p: int
    min_tiles: int
    dest: jax.Array
    tile_expert: jax.Array
    num_tiles: jax.Array
    pad_start: jax.Array
    pad_len: jax.Array


def _plan_group(eid, rank, counts, row0, ntok, tr):
    padded = ((counts + tr - 1) // tr) * tr
    ends = jnp.cumsum(padded)
    base = ends - padded
    dest = (base[eid] + rank).reshape(-1).astype(jnp.int32)
    min_tiles = (ntok * TOP_K) // tr
    ntiles_max = min_tiles + N_EXPERTS
    p = ntiles_max * tr
    tile_start = jnp.arange(ntiles_max, dtype=jnp.int32) * tr
    tile_expert = jnp.minimum(jnp.sum(tile_start[:, None] >= ends[None, :], axis=1), N_EXPERTS - 1)
    pad_len = (padded - counts).at[N_EXPERTS - 1].add(p - ends[-1])
    return _GroupPlan(row0, ntok, tr, p, min_tiles, dest, tile_expert.astype(jnp.int32),
                      (ends[-1] // tr).astype(jnp.int32).reshape(1), (base + counts).astype(jnp.int32),
                      pad_len.astype(jnp.int32))


def moe_layer(x, g, router, w_gate_up, w_down, layer, g_final, final_norm, proj=None, donor=None):
    t, d = x.shape
    sizes = (t // 4, t // 2, t // 4) if t % 4096 == 0 else (t,)
    tr = min(512, sizes[0])
    x, eid, rank, wgt, cnt = moe_route(x, g, router, sizes, proj)
    plans, row0 = [], 0
    for k, n in enumerate(sizes):
        plans.append(_plan_group(eid[row0:row0 + n], rank[row0:row0 + n], cnt[k, 0, :N_EXPERTS], row0, n, tr))
        row0 += n
    out = donor if donor is not None or len(plans) == 1 else jnp.zeros_like(x)
    xs = moe_dispatch(x, plans[0].row0, plans[0].ntok, plans[0].dest, plans[0].pad_start, plans[0].pad_len,
                      plans[0].p)
    ys_prev = None
    for k, plan in enumerate(plans):
        disp = plans[k + 1] if k + 1 < len(plans) else None
        comb = plans[k - 1] if k > 0 else None
        res = moe_experts(xs, g, plan, w_gate_up, w_down, layer, disp=disp, comb=comb, x=x, wgt=wgt,
                          ys_comb=ys_prev, out=out, g_final=g_final, final_norm=final_norm)
        ys_prev = res.pop(0)
        if disp is not None:
            xs = res.pop(0)
        if comb is not None:
            out = res.pop(0)
    return moe_combine(x, wgt, plans[-1], ys_prev, out, g_final, final_norm)


def kernel(x, positions, norm_mix, norm_ffn, norm_final, conv_w_in, conv_w, conv_w_out, mla_w_down, mla_q_norm,
           mla_w_uq, mla_kv_norm, mla_w_ukv, mla_w_o, swa_w_qkv, swa_b_qkv, swa_sinks, swa_w_o, swa_b_o,
           ffn_w_gate_up, ffn_w_down, moe_router, moe_w_gate_up, moe_w_down):
    batch, seq, d = x.shape
    depth = norm_mix.shape[0]
    t = batch * seq
    xf = x.reshape(t, d)
    pos_col = positions.reshape(t, 1).astype(F32)
    pos_row = positions.reshape(t // WINDOW, 1, WINDOW).astype(F32)
    zeros_d = jnp.zeros((d,), F32)
    moe_wgu, moe_wd = moe_w_gate_up.astype(BF16), moe_w_down.astype(BF16)
    for i in range(depth):
        kind = i % N_MIXERS
        j = i // N_MIXERS
        proj = None
        x_in = xf
        if kind == 0:
            xf = conv_mixer(xf, norm_mix[i], conv_w_in[j], conv_w[j], conv_w_out[j], seq)
        elif kind == 1:
            q, k, v = mla_project(xf, pos_col, norm_mix[i], mla_w_down[j], mla_q_norm[j], mla_w_uq[j],
                                  mla_kv_norm[j], mla_w_ukv[j])
            proj = (mla_flash(q, k, v, batch, seq), mla_w_o[j], zeros_d)
        else:
            wq, bq = swa_qkv_weights(swa_w_qkv[j], swa_b_qkv[j])
            qkv = rms_linear(xf, norm_mix[i], wq, bq)
            proj = (swa_attention(qkv, swa_sinks[j], pos_col, pos_row, batch, seq), swa_w_o[j], swa_b_o[j])
        f_idx = i // 2
        last = i == depth - 1
        if i % 2 == 0:
            xf = dense_ffn(xf, norm_ffn[i], ffn_w_gate_up[f_idx], ffn_w_down[f_idx], proj)
            if last:
                xf = final_rmsnorm(xf, norm_final)
        else:
            xf = moe_layer(xf, norm_ffn[i], moe_router[f_idx], moe_wgu, moe_wd, f_idx, norm_final, last, proj,
                           donor=x_in if i > 0 else None)
    return xf.reshape(batch, seq, d)


def _final_norm_body(x_ref, g_ref, o_ref):
    o_ref[...] = _rms(x_ref[...], g_ref[...])


def final_rmsnorm(x, g):
    t, d = x.shape
    tm = _tile(t, 1024)
    return pl.pallas_call(
        _final_norm_body,
        grid=(t // tm,),
        in_specs=[pl.BlockSpec((tm, d), lambda i: (i, 0)), _const_spec((1, d))],
        out_specs=pl.BlockSpec((tm, d), lambda i: (i, 0)),
        out_shape=jax.ShapeDtypeStruct((t, d), F32),
        compiler_params=_cparams(("arbitrary",)),
        name="final_rmsnorm",
    )(x, g.reshape(1, d))
```

```python
import functools
from typing import NamedTuple

import jax
import jax.numpy as jnp
from jax import lax
from jax.experimental import pallas as pl
from jax.experimental.pallas import tpu as pltpu

F32 = jnp.float32
BF16 = jnp.bfloat16

RMS_EPS = 1e-6
NEG_INF = -1e30
CONV_WIDTH = 3
MLA_HEADS = 8
MLA_Q_LORA = 384
MLA_KV_LORA = 256
MLA_NOPE = 128
MLA_ROPE = 64
MLA_V = 128
MLA_QK_PAD = 256
LOG2E = 1.4426950408889634
MLA_Q_SCALE = float((MLA_NOPE + MLA_ROPE) ** -0.5 * LOG2E)
ROPE_THETA = 10000.0
SWA_Q_HEADS = 16
SWA_KV_HEADS = 2
SWA_GROUP = SWA_Q_HEADS // SWA_KV_HEADS
SWA_HEAD_DIM = 64
WINDOW = 128
SWA_Q_SCALE = float(SWA_HEAD_DIM ** -0.5 * LOG2E)
SWA_MASKED_DIST = 1e32
N_EXPERTS = 8
TOP_K = 2
N_MIXERS = 3

TOKEN_TILE = 512
EXPERT_ROW_TILE = 512
VMEM_LIMIT = 56 * 1024 * 1024


def _cparams(sem):
    return pltpu.CompilerParams(dimension_semantics=sem, vmem_limit_bytes=VMEM_LIMIT)


def _rms(xf, g):
    ms = jnp.mean(xf * xf, axis=-1, keepdims=True)
    return xf * lax.rsqrt(ms + RMS_EPS) * g


def _const_spec(shape):
    nd = len(shape)
    return pl.BlockSpec(shape, lambda *_: (0,) * nd)


def _resident_spec(shape):
    nd = len(shape)
    return pl.BlockSpec(shape, lambda *_: (0,) * nd, pipeline_mode=pl.Buffered(1))


def _tile(n, pref):
    t = min(n, pref)
    assert n % t == 0, (n, t)
    return t


def _conv_body(x_ref, g_ref, win_ref, wc_ref, wout_ref, o_ref, vpad_ref, *, tm, d, tiles_per_seq, nsub):
    i = pl.program_id(0)
    ts = tm // nsub

    @pl.when(i % tiles_per_seq == 0)
    def _():
        vpad_ref[0:8, :] = jnp.zeros((8, d), F32)

    def in_proj(r):
        h = _rms(x_ref[r * ts:(r + 1) * ts, :], g_ref[...]).astype(BF16)
        return tuple(jnp.dot(h, win_ref[:, c * d:(c + 1) * d], preferred_element_type=F32) for c in range(3))

    nxt = in_proj(0)
    for r in range(nsub):
        bg, cg, u = nxt
        if r + 1 < nsub:
            nxt = in_proj(r + 1)
        v = cg * u
        lo = r * ts
        vpad_ref[8 + lo:8 + lo + ts, :] = v
        conv = vpad_ref[6 + lo:6 + lo + ts, :] * wc_ref[0:1, :]
        conv = conv + vpad_ref[7 + lo:7 + lo + ts, :] * wc_ref[1:2, :]
        conv = conv + v * wc_ref[2:3, :]
        y = (bg * conv).astype(BF16)
        o_ref[lo:lo + ts, :] = x_ref[lo:lo + ts, :] + jnp.dot(y, wout_ref[...], preferred_element_type=F32)
    vpad_ref[0:8, :] = vpad_ref[tm:tm + 8, :]


def conv_mixer(x, g, w_in, w_conv, w_out, seq):
    t, d = x.shape
    tm = _tile(seq, TOKEN_TILE)
    body = functools.partial(_conv_body, tm=tm, d=d, tiles_per_seq=seq // tm, nsub=2 if tm % 256 == 0 else 1)
    return pl.pallas_call(
        body,
        grid=(t // tm,),
        in_specs=[
            pl.BlockSpec((tm, d), lambda i: (i, 0)),
            _const_spec((1, d)),
            _resident_spec((d, 3 * d)),
            _const_spec((CONV_WIDTH, d)),
            _resident_spec((d, d)),
        ],
        out_specs=pl.BlockSpec((tm, d), lambda i: (i, 0)),
        out_shape=jax.ShapeDtypeStruct((t, d), F32),
        scratch_shapes=[pltpu.VMEM((tm + 8, d), F32)],
        compiler_params=_cparams(("arbitrary",)),
        name="conv_mixer",
    )(x, g.reshape(1, d), w_in.astype(BF16), w_conv, w_out.astype(BF16))


def _swiglu(h, wgu_cols, wd, a_ref, dff, tf):
    def gate_up(c):
        gate = jnp.dot(h, wgu_cols(c * tf, (c + 1) * tf), preferred_element_type=F32)
        up = jnp.dot(h, wgu_cols(dff + c * tf, dff + (c + 1) * tf), preferred_element_type=F32)
        return gate, up

    nxt = gate_up(0)
    for c in range(dff // tf):
        gate, up = nxt
        if c + 1 < dff // tf:
            nxt = gate_up(c + 1)
        a_ref[:, c * tf:(c + 1) * tf] = (gate * jax.nn.sigmoid(gate) * up).astype(BF16)
    return jnp.dot(a_ref[...], wd, preferred_element_type=F32)


def _mixer_out(x_ref, proj_refs):
    if not proj_refs:
        return x_ref[...]
    a_ref, w_ref, b_ref = proj_refs
    return (x_ref[...] + b_ref[...]) + jnp.dot(a_ref[...], w_ref[...], preferred_element_type=F32)


def _proj_specs(proj, tm, d):
    if proj is None:
        return [], []
    a, w, b = proj
    k = a.shape[1]
    specs = [pl.BlockSpec((tm, k), lambda i: (i, 0)), _resident_spec((k, d)), _const_spec((1, d))]
    return [a, w.astype(BF16), b.reshape(1, d).astype(F32)], specs


def _ffn_body(*refs, dff, tf, fused_proj):
    x_ref, *proj_refs = refs[:4] if fused_proj else refs[:1]
    g_ref, wgu_ref, wd_ref, o_ref, a_ref = refs[4:] if fused_proj else refs[1:]
    x = _mixer_out(x_ref, proj_refs)
    h = _rms(x, g_ref[...]).astype(BF16)
    o_ref[...] = x + _swiglu(h, lambda lo, hi: wgu_ref[:, lo:hi], wd_ref[...], a_ref, dff, tf)


def _ff_chunk(dff, pref):
    best = None
    for c in range(128, dff + 1, 128):
        if dff % c == 0 and c <= pref:
            best = c
    return best if best is not None else dff


def dense_ffn(x, g, w_gate_up, w_down, proj=None):
    t, d = x.shape
    dff = w_down.shape[0]
    tm = _tile(t, TOKEN_TILE)
    tf = _ff_chunk(dff, 768)
    body = functools.partial(_ffn_body, dff=dff, tf=tf, fused_proj=proj is not None)
    proj_args, proj_specs = _proj_specs(proj, tm, d)
    return pl.pallas_call(
        body,
        grid=(t // tm,),
        in_specs=[pl.BlockSpec((tm, d), lambda i: (i, 0))] + proj_specs + [
            _const_spec((1, d)),
            _resident_spec((d, 2 * dff)),
            _resident_spec((dff, d)),
        ],
        out_specs=pl.BlockSpec((tm, d), lambda i: (i, 0)),
        out_shape=jax.ShapeDtypeStruct((t, d), F32),
        scratch_shapes=[pltpu.VMEM((tm, dff), BF16)],
        compiler_params=_cparams(("arbitrary",)),
        name="dense_ffn",
    )(x, *proj_args, g.reshape(1, d), w_gate_up.astype(BF16), w_down.astype(BF16))


def _rmslin_body(x_ref, g_ref, w_ref, b_ref, o_ref):
    h = _rms(x_ref[...], g_ref[...]).astype(BF16)
    o_ref[...] = (jnp.dot(h, w_ref[...], preferred_element_type=F32) + b_ref[...]).astype(o_ref.dtype)


def rms_linear(x, g, w, b):
    t, d = x.shape
    n = w.shape[1]
    tm = _tile(t, TOKEN_TILE)
    return pl.pallas_call(
        _rmslin_body,
        grid=(t // tm,),
        in_specs=[
            pl.BlockSpec((tm, d), lambda i: (i, 0)),
            _const_spec((1, d)),
            _const_spec((d, n)),
            _const_spec((1, n)),
        ],
        out_specs=pl.BlockSpec((tm, n), lambda i: (i, 0)),
        out_shape=jax.ShapeDtypeStruct((t, n), BF16),
        compiler_params=_cparams(("arbitrary",)),
        name="rms_linear",
    )(x, g.reshape(1, d), w.astype(BF16), b.reshape(1, n).astype(F32))


def _mla_proj_body(x_ref, pos_ref, g_ref, wd_ref, qn_ref, kvn_ref, wqa_ref, wqb_ref, wkv_ref, frq_ref,
                   q_ref, k_ref, v_ref):
    h = _rms(x_ref[...], g_ref[...]).astype(BF16)
    down = jnp.dot(h, wd_ref[...], preferred_element_type=F32)
    c_q = down[:, 0:MLA_Q_LORA]
    c_kv = down[:, MLA_Q_LORA:MLA_Q_LORA + MLA_KV_LORA]
    o = MLA_Q_LORA + MLA_KV_LORA
    kpe = down[:, o:o + 128]
    kpe_sw = down[:, o + 128:o + 256]
    hq = _rms(c_q, qn_ref[...]).astype(BF16)
    hkv = _rms(c_kv, kvn_ref[...]).astype(BF16)
    qa = jnp.dot(hq, wqa_ref[...], preferred_element_type=F32)
    qb = jnp.dot(hq, wqb_ref[...], preferred_element_type=F32)
    kv = jnp.dot(hkv, wkv_ref[...], preferred_element_type=F32)
    ang = pos_ref[...] * frq_ref[0:1, :]
    cos = jnp.cos(ang)
    sin = jnp.sin(ang) * frq_ref[1:2, :]
    kpe_r = (kpe * cos + kpe_sw * sin).astype(BF16)
    ones_col = jnp.where(lax.broadcasted_iota(jnp.int32, cos.shape, 1) == 0, 1.0, 0.0).astype(BF16)
    for hd in range(MLA_HEADS):
        qo = hd * MLA_QK_PAD
        q_ref[hd, :, 0:MLA_NOPE] = qa[:, qo:qo + MLA_NOPE].astype(BF16)
        qpe = qa[:, qo + MLA_NOPE:qo + MLA_QK_PAD] * cos + qb[:, hd * 128:(hd + 1) * 128] * sin
        q_ref[hd, :, MLA_NOPE:MLA_QK_PAD] = qpe.astype(BF16)
        ko = hd * (MLA_NOPE + MLA_V)
        k_ref[hd, :, 0:MLA_NOPE] = kv[:, ko:ko + MLA_NOPE].astype(BF16)
        k_ref[hd, :, MLA_NOPE:MLA_QK_PAD] = kpe_r
        v_ref[hd, :, 0:MLA_V] = kv[:, ko + MLA_NOPE:ko + MLA_NOPE + MLA_V].astype(BF16)
        v_ref[hd, :, MLA_V:2 * MLA_V] = ones_col


def _rope_swap(w):
    half = MLA_ROPE // 2
    return jnp.concatenate([w[..., half:], w[..., :half]], axis=-1)


def mla_project(x, pos_col, g, w_down, q_norm, w_uq, kv_norm, w_ukv):
    t, d = x.shape
    tm = _tile(t, TOKEN_TILE)
    hn = MLA_HEADS
    z64 = jnp.zeros((d, 64), F32)
    o = MLA_Q_LORA + MLA_KV_LORA
    kpe_w = w_down[:, o:o + MLA_ROPE]
    wd = jnp.concatenate([w_down[:, :o], kpe_w, z64, _rope_swap(kpe_w), z64], axis=1).astype(BF16)
    wq = (w_uq * MLA_Q_SCALE).reshape(MLA_Q_LORA, hn, MLA_NOPE + MLA_ROPE)
    zq = jnp.zeros((MLA_Q_LORA, hn, 64), F32)
    wqa = jnp.concatenate([wq, zq], axis=-1).reshape(MLA_Q_LORA, hn * MLA_QK_PAD).astype(BF16)
    wqb = jnp.concatenate([_rope_swap(wq[..., MLA_NOPE:]), zq], axis=-1).reshape(MLA_Q_LORA, hn * 128).astype(BF16)
    inv = ROPE_THETA ** (-jnp.arange(0, MLA_ROPE, 2, dtype=F32) / MLA_ROPE)
    half = MLA_ROPE // 2
    frq = jnp.stack([
        jnp.concatenate([inv, inv, jnp.zeros((64,), F32)]),
        jnp.concatenate([-jnp.ones((half,), F32), jnp.ones((half,), F32), jnp.zeros((64,), F32)]),
    ])
    nd = wd.shape[1]
    outs = pl.pallas_call(
        _mla_proj_body,
        grid=(t // tm,),
        in_specs=[
            pl.BlockSpec((tm, d), lambda i: (i, 0)),
            pl.BlockSpec((tm, 1), lambda i: (i, 0)),
            _const_spec((1, d)),
            _resident_spec((d, nd)),
            _const_spec((1, MLA_Q_LORA)),
            _const_spec((1, MLA_KV_LORA)),
            _resident_spec((MLA_Q_LORA, hn * MLA_QK_PAD)),
            _resident_spec((MLA_Q_LORA, hn * 128)),
            _resident_spec((MLA_KV_LORA, hn * (MLA_NOPE + MLA_V))),
            _const_spec((2, 128)),
        ],
        out_specs=[
            pl.BlockSpec((hn, tm, MLA_QK_PAD), lambda i: (0, i, 0)),
            pl.BlockSpec((hn, tm, MLA_QK_PAD), lambda i: (0, i, 0)),
            pl.BlockSpec((hn, tm, 2 * MLA_V), lambda i: (0, i, 0)),
        ],
        out_shape=[
            jax.ShapeDtypeStruct((hn, t, MLA_QK_PAD), BF16),
            jax.ShapeDtypeStruct((hn, t, MLA_QK_PAD), BF16),
            jax.ShapeDtypeStruct((hn, t, 2 * MLA_V), BF16),
        ],
        compiler_params=_cparams(("arbitrary",)),
        name="mla_project",
    )(x, pos_col, g.reshape(1, d), wd, q_norm.reshape(1, -1), kv_norm.reshape(1, -1), wqa, wqb,
      w_ukv.astype(BF16), frq)
    return outs


def _flash_body(qi_ref, kj_ref, q_ref, k_ref, v_ref, o_ref, m_ref, acc_ref, *, tq, tqs, hp, dv):
    pair = pl.program_id(2)
    i = qi_ref[pair]
    j = kj_ref[pair]

    @pl.when(j == 0)
    def _():
        m_ref[...] = jnp.full_like(m_ref, NEG_INF)
        acc_ref[...] = jnp.zeros_like(acc_ref)

    def step(diagonal):
        nsub = tq // tqs
        units = [(h, r) for h in range(hp) for r in range(nsub)]
        causal = (lax.broadcasted_iota(jnp.int32, (tqs, tqs), 1) <= lax.broadcasted_iota(jnp.int32, (tqs, tqs), 0))

        def scores(u):
            h, r = u
            nk = (r + 1) * tqs if diagonal else tq
            return lax.dot_general(q_ref[h, r * tqs:(r + 1) * tqs, :], k_ref[h, 0:nk, :], (((1,), (1,)), ((), ())),
                                   preferred_element_type=F32)

        s_next = scores(units[0])
        for n, (h, r) in enumerate(units):
            rows = slice(r * tqs, (r + 1) * tqs)
            nk = (r + 1) * tqs if diagonal else tq
            s = s_next
            if n + 1 < len(units):
                s_next = scores(units[n + 1])
            if diagonal:
                s_diag = jnp.where(causal, s[:, nk - tqs:nk], NEG_INF)
                s = s_diag if r == 0 else jnp.concatenate([s[:, 0:nk - tqs], s_diag], axis=1)
            m_prev = m_ref[h, rows, :]
            m_new = jnp.maximum(m_prev, jnp.max(s, axis=-1, keepdims=True))
            alpha = jnp.exp2(m_prev - m_new)
            p = jnp.exp2(s - m_new)
            pv = jnp.dot(p.astype(BF16), v_ref[h, 0:nk, :], preferred_element_type=F32)
            acc_ref[h, rows, :] = alpha * acc_ref[h, rows, :] + pv
            m_ref[h, rows, :] = m_new

    @pl.when(j < i)
    def _():
        step(False)

    @pl.when(j == i)
    def _():
        step(True)
        for h in range(hp):
            o_ref[:, h * dv:(h + 1) * dv] = (acc_ref[h, :, 0:dv] / acc_ref[h, :, dv:dv + 1]).astype(o_ref.dtype)


def mla_flash(q, k, v, batch, seq):
    hn, t, dq = q.shape
    dvp = v.shape[2]
    dv = MLA_V
    tq = tk = _tile(seq, 1024)
    nq = nk = seq // tq
    hp = 4
    body = functools.partial(_flash_body, tq=tq, tqs=_tile(tq, 256), hp=hp, dv=dv)
    pairs = [(i, j) for i in range(nq) for j in range(i + 1)]
    qi = jnp.asarray([p[0] for p in pairs], jnp.int32)
    kj = jnp.asarray([p[1] for p in pairs], jnp.int32)
    grid_spec = pltpu.PrefetchScalarGridSpec(
        num_scalar_prefetch=2,
        grid=(batch, hn // hp, len(pairs)),
        in_specs=[
            pl.BlockSpec((hp, tq, dq), lambda b, h, p, qi, kj: (h, b * nq + qi[p], 0)),
            pl.BlockSpec((hp, tk, dq), lambda b, h, p, qi, kj: (h, b * nk + kj[p], 0)),
            pl.BlockSpec((hp, tk, dvp), lambda b, h, p, qi, kj: (h, b * nk + kj[p], 0)),
        ],
        out_specs=pl.BlockSpec((tq, hp * dv), lambda b, h, p, qi, kj: (b * nq + qi[p], h)),
        scratch_shapes=[pltpu.VMEM((hp, tq, 1), F32), pltpu.VMEM((hp, tq, dvp), F32)],
    )
    return pl.pallas_call(
        body,
        grid_spec=grid_spec,
        out_shape=jax.ShapeDtypeStruct((t, hn * dv), BF16),
        compiler_params=_cparams(("arbitrary", "arbitrary", "arbitrary")),
        name="mla_flash",
    )(qi, kj, q, k, v)


def _alibi_slopes():
    return [2.0 ** (-8.0 * (h + 1) / SWA_Q_HEADS) for h in range(SWA_Q_HEADS)]


def _swa_body(sink_ref, q_ref, kc_ref, kp_ref, vc_ref, vp_ref, pq_ref, pkc_ref, pkp_ref, o_ref, *, steps_per_seq, nblk):
    w = WINDOW
    seq_start = pl.program_id(0) % steps_per_seq == 0
    for r in range(nblk):
        cur = slice(r * w, (r + 1) * w)
        prv = slice((r - 1) * w, r * w)
        _swa_block(
            sink_ref, q_ref.at[cur], kc_ref.at[cur], kp_ref if r == 0 else kc_ref.at[prv],
            vc_ref.at[cur], vp_ref if r == 0 else vc_ref.at[prv], pq_ref[cur, :], pkc_ref[r],
            pkp_ref[0] if r == 0 else pkc_ref[r - 1], o_ref.at[cur],
            has_prev=jnp.logical_not(seq_start) if r == 0 else True)


def _swa_block(sink_ref, q_ref, kc_ref, kp_ref, vc_ref, vp_ref, pq, pkc, pkp, o_ref, *, has_prev):
    w = WINDOW
    dist_c = jnp.abs(pq - pkc)
    dist_p = jnp.abs(pq - pkp)
    iq = lax.broadcasted_iota(jnp.int32, (w, w), 0)
    ik = lax.broadcasted_iota(jnp.int32, (w, w), 1)
    dist_c = jnp.where(ik <= iq, dist_c, SWA_MASKED_DIST)
    dist_p = jnp.where(jnp.logical_and(ik > iq, has_prev), dist_p, SWA_MASKED_DIST)
    lane = lax.broadcasted_iota(jnp.int32, (w, 2 * SWA_HEAD_DIM), 1)
    left = lane < SWA_HEAD_DIM
    slopes = _alibi_slopes()
    pairs = SWA_GROUP // 2
    for kh in range(SWA_KV_HEADS):
        kc = kc_ref[:, kh * 128:(kh + 1) * 128]
        kp = kp_ref[:, kh * 128:(kh + 1) * 128]
        vc_l = vc_ref[:, kh * 256:kh * 256 + 128]
        vc_r = vc_ref[:, kh * 256 + 128:kh * 256 + 256]
        vp_l = vp_ref[:, kh * 256:kh * 256 + 128]
        vp_r = vp_ref[:, kh * 256 + 128:kh * 256 + 256]
        zero = jnp.zeros((w, 128), BF16)
        rows = []
        for half in range(2):
            for p in range(pairs):
                qp = q_ref[:, (kh * pairs + p) * 128:(kh * pairs + p + 1) * 128]
                rows.append(jnp.where(left if half == 0 else jnp.logical_not(left), qp, zero))
        qs = jnp.concatenate(rows, axis=0)
        dn = (((1,), (1,)), ((), ()))
        s_c = lax.dot_general(qs, kc, dn, preferred_element_type=F32)
        s_p = lax.dot_general(qs, kp, dn, preferred_element_type=F32)
        pc_rows, pp_rows = [], []
        for half in range(2):
            for p in range(pairs):
                hd = kh * SWA_GROUP + 2 * p + half
                r0 = (half * pairs + p) * w
                sink = sink_ref[hd] * LOG2E
                a_c = s_c[r0:r0 + w] - (slopes[hd] * LOG2E) * dist_c
                a_p = s_p[r0:r0 + w] - (slopes[hd] * LOG2E) * dist_p
                m = jnp.maximum(jnp.max(jnp.maximum(a_c, a_p), axis=-1, keepdims=True), sink)
                e_c = jnp.exp2(a_c - m)
                e_p = jnp.exp2(a_p - m)
                den = jnp.sum(e_c + e_p, axis=-1, keepdims=True) + jnp.exp2(sink - m)
                inv = 1.0 / den
                pc_rows.append((e_c * inv).astype(BF16))
                pp_rows.append((e_p * inv).astype(BF16))
        hw = pairs * w
        pc = jnp.concatenate(pc_rows, axis=0)
        pp = jnp.concatenate(pp_rows, axis=0)
        o_pair = (jnp.dot(pc[0:hw], vc_l, preferred_element_type=F32)
                  + jnp.dot(pp[0:hw], vp_l, preferred_element_type=F32)
                  + jnp.dot(pc[hw:2 * hw], vc_r, preferred_element_type=F32)
                  + jnp.dot(pp[hw:2 * hw], vp_r, preferred_element_type=F32))
        for p in range(pairs):
            c0 = (kh * pairs + p) * 128
            o_ref[:, c0:c0 + 128] = o_pair[p * w:(p + 1) * w].astype(o_ref.dtype)


def swa_attention(qkv, sinks, pos_col, pos_row, batch, seq):
    t = qkv.shape[0]
    w = WINDOW
    nblk = 8 if (seq // w) % 8 == 0 else (4 if (seq // w) % 4 == 0 else 1)
    tq = nblk * w
    steps_per_seq = seq // tq
    nq = SWA_Q_HEADS * SWA_HEAD_DIM
    body = functools.partial(_swa_body, steps_per_seq=steps_per_seq, nblk=nblk)

    def prev(i):
        return jnp.where(i % steps_per_seq == 0, i * nblk, i * nblk - 1)

    kcol = (nq + 4 * 128) // 256
    return pl.pallas_call(
        body,
        grid=(t // tq,),
        in_specs=[
            pl.BlockSpec(memory_space=pltpu.SMEM),
            pl.BlockSpec((tq, nq), lambda i: (i, 0)),
            pl.BlockSpec((tq, 256), lambda i: (i, kcol)),
            pl.BlockSpec((w, 256), lambda i: (prev(i), kcol)),
            pl.BlockSpec((tq, 512), lambda i: (i, nq // 512)),
            pl.BlockSpec((w, 512), lambda i: (prev(i), nq // 512)),
            pl.BlockSpec((tq, 1), lambda i: (i, 0)),
            pl.BlockSpec((nblk, 1, w), lambda i: (i, 0, 0)),
            pl.BlockSpec((1, 1, w), lambda i: (prev(i), 0, 0)),
        ],
        out_specs=pl.BlockSpec((tq, nq), lambda i: (i, 0)),
        out_shape=jax.ShapeDtypeStruct((t, nq), BF16),
        compiler_params=_cparams(("arbitrary",)),
        name="swa_attention",
    )(sinks.astype(F32), qkv, qkv, qkv, qkv, qkv, pos_col, pos_row, pos_row)


def swa_qkv_weights(w_qkv, b_qkv):
    nq = SWA_Q_HEADS * SWA_HEAD_DIM
    nk = SWA_KV_HEADS * SWA_HEAD_DIM
    hd = SWA_HEAD_DIM

    def arrange(m):
        q = m[..., :nq] * SWA_Q_SCALE
        k = m[..., nq:nq + nk]
        v = m[..., nq + nk:]
        z = jnp.zeros(m.shape[:-1] + (hd,), m.dtype)
        parts = [q]
        for kh in range(SWA_KV_HEADS):
            vh = v[..., kh * hd:(kh + 1) * hd]
            parts += [vh, z, z, vh]
        for kh in range(SWA_KV_HEADS):
            kk = k[..., kh * hd:(kh + 1) * hd]
            parts += [kk, kk]
        return jnp.concatenate(parts, axis=-1)

    return arrange(w_qkv), arrange(b_qkv)


def _group_of_step(i, group_starts):
    gid = 0
    for s in group_starts[1:]:
        gid = gid + (i >= s).astype(jnp.int32)
    return gid


def _router_body(*refs, tm, fused_proj, group_starts):
    x_ref, *proj_refs = refs[:4] if fused_proj else refs[:1]
    rest = refs[4:] if fused_proj else refs[1:]
    if fused_proj:
        g_ref, r_ref, x1_ref, eid_ref, rank_ref, wgt_ref, cnt_ref, run_ref = rest
    else:
        g_ref, r_ref, eid_ref, rank_ref, wgt_ref, cnt_ref, run_ref = rest
    i = pl.program_id(0)

    is_start = functools.reduce(jnp.logical_or, [i == s for s in group_starts])

    @pl.when(is_start)
    def _():
        run_ref[...] = jnp.zeros_like(run_ref)

    x = _mixer_out(x_ref, proj_refs)
    if fused_proj:
        x1_ref[...] = x
    h = _rms(x, g_ref[...]).astype(BF16)
    logits = jnp.dot(h, r_ref[...], preferred_element_type=F32)
    lane = lax.broadcasted_iota(jnp.int32, logits.shape, 1)
    logits = jnp.where(lane < N_EXPERTS, logits, -jnp.inf)
    m1 = jnp.max(logits, axis=-1, keepdims=True)
    i1 = jnp.min(jnp.where(logits == m1, lane, 128), axis=-1, keepdims=True)
    rest = jnp.where(lane == i1, -jnp.inf, logits)
    m2 = jnp.max(rest, axis=-1, keepdims=True)
    i2 = jnp.min(jnp.where(rest == m2, lane, 128), axis=-1, keepdims=True)
    e2 = jnp.exp(m2 - m1)
    w1 = 1.0 / (1.0 + e2)
    w2 = e2 / (1.0 + e2)
    oh1 = (lane == i1)
    oh2 = (lane == i2)
    sel = jnp.where(jnp.logical_or(oh1, oh2), 1.0, 0.0).astype(BF16)
    r = lax.broadcasted_iota(jnp.int32, (tm, tm), 0)
    c = lax.broadcasted_iota(jnp.int32, (tm, tm), 1)
    tri = jnp.where(c < r, 1.0, 0.0).astype(BF16)
    before = jnp.dot(tri, sel, preferred_element_type=F32) + run_ref[...]
    rank1 = jnp.sum(jnp.where(oh1, before, 0.0), axis=-1, keepdims=True)
    rank2 = jnp.sum(jnp.where(oh2, before, 0.0), axis=-1, keepdims=True)
    run_ref[...] = run_ref[...] + jnp.sum(sel.astype(F32), axis=0, keepdims=True)
    eid_ref[...] = jnp.where(lane == 0, i1, jnp.where(lane == 1, i2, 0))[:, 0:TOP_K]
    rank_ref[...] = jnp.where(lane == 0, rank1, jnp.where(lane == 1, rank2, 0.0))[:, 0:TOP_K].astype(jnp.int32)
    wgt_ref[...] = jnp.where(lane == 0, w1, jnp.where(lane == 1, w2, 0.0))[:, 0:TOP_K]
    cnt_ref[0] = jnp.broadcast_to(run_ref[...], (8, 128)).astype(jnp.int32)


def moe_route(x, g, router, group_sizes, proj=None):
    t, d = x.shape
    groups = len(group_sizes)
    tm = _tile(min(group_sizes), TOKEN_TILE)
    assert all(n % tm == 0 for n in group_sizes) and sum(group_sizes) == t
    group_starts = tuple(sum(group_sizes[:k]) // tm for k in range(groups))
    rpad = jnp.zeros((d, 128), F32).at[:, :N_EXPERTS].set(router).astype(BF16)
    body = functools.partial(_router_body, tm=tm, fused_proj=proj is not None, group_starts=group_starts)
    proj_args, proj_specs = _proj_specs(proj, tm, d)
    x1_spec = [pl.BlockSpec((tm, d), lambda i: (i, 0))] if proj is not None else []
    x1_shape = [jax.ShapeDtypeStruct((t, d), F32)] if proj is not None else []
    outs = pl.pallas_call(
        body,
        grid=(t // tm,),
        in_specs=[pl.BlockSpec((tm, d), lambda i: (i, 0))] + proj_specs + [
            _const_spec((1, d)),
            _const_spec((d, 128)),
        ],
        out_specs=x1_spec + [
            pl.BlockSpec((tm, TOP_K), lambda i: (i, 0)),
            pl.BlockSpec((tm, TOP_K), lambda i: (i, 0)),
            pl.BlockSpec((tm, TOP_K), lambda i: (i, 0)),
            pl.BlockSpec((1, 8, 128), lambda i: (_group_of_step(i, group_starts), 0, 0)),
        ],
        out_shape=x1_shape + [
            jax.ShapeDtypeStruct((t, TOP_K), jnp.int32),
            jax.ShapeDtypeStruct((t, TOP_K), jnp.int32),
            jax.ShapeDtypeStruct((t, TOP_K), F32),
            jax.ShapeDtypeStruct((groups, 8, 128), jnp.int32),
        ],
        scratch_shapes=[pltpu.VMEM((1, 128), F32)],
        compiler_params=_cparams(("arbitrary",)),
        name="moe_route",
    )(x, *proj_args, g.reshape(1, d), rpad)
    return outs if proj is not None else [x] + list(outs)


def _zero_pad_rows(pad_start_ref, pad_len_ref, xs_ref, zrow_ref, zsem):
    zrow_ref[...] = jnp.zeros_like(zrow_ref)
    for e in range(N_EXPERTS):
        def zero_row(r):
            return pltpu.make_async_copy(zrow_ref.at[pl.ds(0, 1)], xs_ref.at[pl.ds(pad_start_ref[e] + r, 1)], zsem)

        def zissue(r, carry):
            zero_row(r).start()
            return carry

        def zwait(r, carry):
            zero_row(r).wait()
            return carry

        lax.fori_loop(0, pad_len_ref[e], zissue, 0)
        lax.fori_loop(0, pad_len_ref[e], zwait, 0)


def _for_rows(n, fn, inline):
    if inline:
        for r in range(n):
            fn(r)
    else:
        lax.fori_loop(0, n, lambda r, c: (fn(r), c)[1], 0, unroll=8)


def _scatter_rows(dest_ref, x_ref, xs_ref, sem, n, inline):
    def issue(r):
        for kk in range(TOP_K):
            pltpu.make_async_copy(x_ref.at[pl.ds(r, 1)], xs_ref.at[pl.ds(dest_ref[r * TOP_K + kk], 1)], sem).start()

    _for_rows(n, issue, inline)


def _scatter_wait(x_ref, xs_ref, sem, n):
    for kk in range(TOP_K):
        pltpu.make_async_copy(x_ref, xs_ref.at[pl.ds(0, n)], sem).wait()


def _gather_rows(dest_ref, ys_ref, buf_ref, sem, n, inline):
    def issue(r):
        for kk in range(TOP_K):
            pltpu.make_async_copy(ys_ref.at[pl.ds(dest_ref[r * TOP_K + kk], 1)], buf_ref.at[kk, pl.ds(r, 1)], sem).start()

    _for_rows(n, issue, inline)


def _gather_wait(ys_ref, buf_ref, sem, n):
    for kk in range(TOP_K):
        pltpu.make_async_copy(ys_ref.at[pl.ds(0, n)], buf_ref.at[kk], sem).wait()


def _dispatch_body(pad_start_ref, pad_len_ref, dest_ref, x_ref, xs_ref, zrow_ref, sem, zsem, *, tm):
    @pl.when(pl.program_id(0) == 0)
    def _():
        _zero_pad_rows(pad_start_ref, pad_len_ref, xs_ref, zrow_ref, zsem)

    _scatter_rows(dest_ref, x_ref, xs_ref, sem, tm, inline=False)
    _scatter_wait(x_ref, xs_ref, sem, tm)


def moe_dispatch(x, row0, ntok, dest_flat, pad_start, pad_len, p):
    d = x.shape[1]
    tm = _tile(ntok, TOKEN_TILE)
    tile0 = row0 // tm
    body = functools.partial(_dispatch_body, tm=tm)
    grid_spec = pltpu.PrefetchScalarGridSpec(
        num_scalar_prefetch=2,
        grid=(ntok // tm,),
        in_specs=[
            pl.BlockSpec((tm * TOP_K,), lambda i, ps, pn: (i,), memory_space=pltpu.SMEM),
            pl.BlockSpec((tm, d), lambda i, ps, pn: (tile0 + i, 0)),
        ],
        out_specs=pl.BlockSpec(memory_space=pl.ANY),
        scratch_shapes=[pltpu.VMEM((8, d), F32), pltpu.SemaphoreType.DMA(()), pltpu.SemaphoreType.DMA(())],
    )
    return pl.pallas_call(
        body,
        grid_spec=grid_spec,
        out_shape=jax.ShapeDtypeStruct((p, d), F32),
        compiler_params=_cparams(("arbitrary",)),
        name="moe_dispatch",
    )(pad_start, pad_len, dest_flat, x)


def _expert_body(*refs, dff, tf, ts_d, ts_c, nside, final_norm):
    refs = list(refs)
    te_ref, nt_ref = refs[:2]
    del refs[:2]
    if ts_d:
        pad_start_ref, pad_len_ref = refs[:2]
        del refs[:2]
    xs_ref, g_ref, wgu_ref, wd_ref = refs[:4]
    del refs[:4]
    if ts_d:
        dest_d_ref, x_d_ref = refs[:2]
        del refs[:2]
    if ts_c:
        dest_c_ref, x_c_ref, w_c_ref, gf_ref, ys_c_ref, _ = refs[:6]
        del refs[:6]
    ys_ref = refs.pop(0)
    if ts_d:
        xs_d_ref = refs.pop(0)
    if ts_c:
        out_ref = refs.pop(0)
    a_ref = refs.pop(0)
    if ts_d:
        zrow_ref, sem_d, zsem = refs[:3]
        del refs[:3]
    if ts_c:
        buf_ref, sem_c = refs[:2]
    i = pl.program_id(0)

    def experts():
        h = _rms(xs_ref[...], g_ref[...]).astype(BF16)
        ys_ref[...] = _swiglu(h, lambda lo, hi: wgu_ref[0, 0, :, lo:hi], wd_ref[0, 0], a_ref, dff, tf)

    if ts_d:
        @pl.when(i == 0)
        def _():
            _zero_pad_rows(pad_start_ref, pad_len_ref, xs_d_ref, zrow_ref, zsem)

    if not (ts_d or ts_c):
        pl.when(i < nt_ref[0])(experts)
    else:
        @pl.when(i < nside)
        def _():
            if ts_d:
                _scatter_rows(dest_d_ref, x_d_ref, xs_d_ref, sem_d, ts_d, inline=True)
            if ts_c:
                _gather_rows(dest_c_ref, ys_c_ref, buf_ref, sem_c, ts_c, inline=True)
            experts()
            if ts_d:
                _scatter_wait(x_d_ref, xs_d_ref, sem_d, ts_d)
            if ts_c:
                _gather_wait(ys_c_ref, buf_ref, sem_c, ts_c)
                out_ref[...] = _combine_rows(x_c_ref, w_c_ref, buf_ref, gf_ref, final_norm)

        pl.when(jnp.logical_and(i >= nside, i < nt_ref[0]))(experts)

    @pl.when(i >= nt_ref[0])
    def _():
        ys_ref[...] = jnp.zeros_like(ys_ref)


def _combine_rows(x_ref, w_ref, buf_ref, gf_ref, final_norm):
    w = w_ref[...]
    out = x_ref[...] + (w[:, 0:1] * buf_ref[0] + w[:, 1:2] * buf_ref[1])
    return _rms(out, gf_ref[...]) if final_norm else out


def moe_experts(xs, g, plan, w_gate_up, w_down, layer, disp=None, comb=None, x=None, wgt=None, ys_comb=None,
                out=None, g_final=None, final_norm=False):
    p, d = xs.shape
    tr = plan.tr
    dff = w_down.shape[2]
    tf = _ff_chunk(dff, 512)
    nside = plan.min_tiles
    ts_d = disp.ntok // nside if disp is not None else 0
    ts_c = comb.ntok // nside if comb is not None else 0
    body = functools.partial(_expert_body, dff=dff, tf=tf, ts_d=ts_d, ts_c=ts_c, nside=nside, final_norm=final_norm)

    def row(i, nt):
        return jnp.minimum(i, nt[0] - 1)

    def imap(fn):
        return lambda i, *pf: fn(i, pf[0], pf[1])

    def side_specs(o, ts):
        tile0 = o.row0 // ts

        def tok_map(i, te, nt):
            return (tile0 + jnp.minimum(i, nside - 1), 0)

        dest_spec = pl.BlockSpec((ts * TOP_K,), imap(lambda i, te, nt: (jnp.minimum(i, nside - 1),)),
                                 memory_space=pltpu.SMEM)
        return dest_spec, (lambda width: pl.BlockSpec((ts, width), imap(tok_map)))

    in_specs = [
        pl.BlockSpec((tr, d), imap(lambda i, te, nt: (row(i, nt), 0))),
        pl.BlockSpec((1, d), imap(lambda i, te, nt: (0, 0))),
        pl.BlockSpec((1, 1, d, 2 * dff), imap(lambda i, te, nt: (layer, te[row(i, nt)], 0, 0)),
                     pipeline_mode=pl.Buffered(1)),
        pl.BlockSpec((1, 1, dff, d), imap(lambda i, te, nt: (layer, te[row(i, nt)], 0, 0)),
                     pipeline_mode=pl.Buffered(1)),
    ]
    out_specs = [pl.BlockSpec((tr, d), imap(lambda i, te, nt: (i, 0)))]
    out_shape = [jax.ShapeDtypeStruct((p, d), F32)]
    scratch = [pltpu.VMEM((tr, dff), BF16)]
    args = [plan.tile_expert, plan.num_tiles]
    operands = [xs, g.reshape(1, d), w_gate_up, w_down]
    aliases = {}
    if disp is not None:
        dest_spec, tok_spec = side_specs(disp, ts_d)
        args += [disp.pad_start, disp.pad_len]
        in_specs += [dest_spec, tok_spec(d)]
        operands += [disp.dest, x]
        out_specs.append(pl.BlockSpec(memory_space=pl.ANY))
        out_shape.append(jax.ShapeDtypeStruct((disp.p, d), F32))
        scratch += [pltpu.VMEM((8, d), F32), pltpu.SemaphoreType.DMA(()), pltpu.SemaphoreType.DMA(())]
    if comb is not None:
        dest_spec, tok_spec = side_specs(comb, ts_c)
        in_specs += [dest_spec, tok_spec(d), tok_spec(TOP_K), pl.BlockSpec((1, d), imap(lambda i, te, nt: (0, 0))),
                     pl.BlockSpec(memory_space=pl.ANY), pl.BlockSpec(memory_space=pl.ANY)]
        operands += [comb.dest, x, wgt, g_final.reshape(1, d), ys_comb, out]
        aliases = {len(args) + len(operands) - 1: len(out_specs)}
        out_specs.append(tok_spec(d))
        out_shape.append(jax.ShapeDtypeStruct(out.shape, F32))
        scratch += [pltpu.VMEM((TOP_K, ts_c, d), F32), pltpu.SemaphoreType.DMA(())]
    grid_spec = pltpu.PrefetchScalarGridSpec(
        num_scalar_prefetch=len(args),
        grid=(p // tr,),
        in_specs=in_specs,
        out_specs=out_specs,
        scratch_shapes=scratch,
    )
    outs = pl.pallas_call(
        body,
        grid_spec=grid_spec,
        out_shape=out_shape,
        input_output_aliases=aliases,
        compiler_params=_cparams(("arbitrary",)),
        name="moe_experts" + ("_d" if disp is not None else "") + ("_c" if comb is not None else ""),
    )(*args, *operands)
    return list(outs)


def _combine_body(dest_ref, dest_next_ref, x_ref, w_ref, gf_ref, ys_ref, *rest, tm, final_norm):
    o_ref, buf_ref, sems = rest[-3:]
    i = pl.program_id(0)
    slot = i % 2

    def gather(d_ref, s):
        def issue(r, carry):
            for kk in range(TOP_K):
                src = d_ref[r * TOP_K + kk]
                pltpu.make_async_copy(ys_ref.at[pl.ds(src, 1)], buf_ref.at[s, kk, pl.ds(r, 1)], sems.at[s]).start()
            return carry

        lax.fori_loop(0, tm, issue, 0, unroll=8)

    @pl.when(i == 0)
    def _():
        gather(dest_ref, 0)

    @pl.when(i + 1 < pl.num_programs(0))
    def _():
        gather(dest_next_ref, 1 - slot)

    for kk in range(TOP_K):
        pltpu.make_async_copy(ys_ref.at[pl.ds(0, tm)], buf_ref.at[slot, kk], sems.at[slot]).wait()
    o_ref[...] = _combine_rows(x_ref, w_ref, buf_ref.at[slot], gf_ref, final_norm)


def moe_combine(x, wgt, plan, ys, out, g_final, final_norm):
    t, d = x.shape
    tm = _tile(plan.ntok, TOKEN_TILE)
    body = functools.partial(_combine_body, tm=tm, final_norm=final_norm)
    nsteps = plan.ntok // tm
    tile0 = plan.row0 // tm
    donor = [out] if out is not None else []
    return pl.pallas_call(
        body,
        grid=(nsteps,),
        in_specs=[
            pl.BlockSpec((tm * TOP_K,), lambda i: (i,), memory_space=pltpu.SMEM),
            pl.BlockSpec((tm * TOP_K,), lambda i: (jnp.minimum(i + 1, nsteps - 1),), memory_space=pltpu.SMEM),
            pl.BlockSpec((tm, d), lambda i: (tile0 + i, 0)),
            pl.BlockSpec((tm, TOP_K), lambda i: (tile0 + i, 0)),
            pl.BlockSpec((1, d), lambda i: (0, 0)),
            pl.BlockSpec(memory_space=pl.ANY),
        ] + [pl.BlockSpec(memory_space=pl.ANY)] * len(donor),
        out_specs=pl.BlockSpec((tm, d), lambda i: (tile0 + i, 0)),
        out_shape=jax.ShapeDtypeStruct((t, d), F32),
        scratch_shapes=[pltpu.VMEM((2, TOP_K, tm, d), F32), pltpu.SemaphoreType.DMA((2,))],
        input_output_aliases={6: 0} if donor else {},
        compiler_params=_cparams(("arbitrary",)),
        name="moe_combine",
    )(plan.dest, plan.dest, x, wgt, g_final.reshape(1, d), ys, *donor)


class _GroupPlan(NamedTuple):
    row0: int
    ntok: int
    tr: int
    p: int
    min_tiles: int
    dest: jax.Array
    tile_expert: jax.Array
    num_tiles: jax.Array
    pad_start: jax.Array
    pad_len: jax.Array


def _plan_group(eid, rank, counts, row0, ntok, tr):
    padded = ((counts + tr - 1) // tr) * tr
    ends = jnp.cumsum(padded)
    base = ends - padded
    dest = (base[eid] + rank).reshape(-1).astype(jnp.int32)
    min_tiles = (ntok * TOP_K) // tr
    ntiles_max = min_tiles + N_EXPERTS
    p = ntiles_max * tr
    tile_start = jnp.arange(ntiles_max, dtype=jnp.int32) * tr
    tile_expert = jnp.minimum(jnp.sum(tile_start[:, None] >= ends[None, :], axis=1), N_EXPERTS - 1)
    pad_len = (padded - counts).at[N_EXPERTS - 1].add(p - ends[-1])
    return _GroupPlan(row0, ntok, tr, p, min_tiles, dest, tile_expert.astype(jnp.int32),
                      (ends[-1] // tr).astype(jnp.int32).reshape(1), (base + counts).astype(jnp.int32),
                      pad_len.astype(jnp.int32))


def moe_layer(x, g, router, w_gate_up, w_down, layer, g_final, final_norm, proj=None, donor=None):
    t, d = x.shape
    sizes = (t // 4, t // 2, t // 4) if t % (8 * TOKEN_TILE) == 0 else (t,)
    tr = min(EXPERT_ROW_TILE, sizes[0])
    x, eid, rank, wgt, cnt = moe_route(x, g, router, sizes, proj)
    plans, row0 = [], 0
    for k, n in enumerate(sizes):
        plans.append(_plan_group(eid[row0:row0 + n], rank[row0:row0 + n], cnt[k, 0, :N_EXPERTS], row0, n, tr))
        row0 += n
    out = donor if donor is not None or len(plans) == 1 else jnp.zeros_like(x)
    xs = moe_dispatch(x, plans[0].row0, plans[0].ntok, plans[0].dest, plans[0].pad_start, plans[0].pad_len,
                      plans[0].p)
    ys_prev = None
    for k, plan in enumerate(plans):
        disp = plans[k + 1] if k + 1 < len(plans) else None
        comb = plans[k - 1] if k > 0 else None
        res = moe_experts(xs, g, plan, w_gate_up, w_down, layer, disp=disp, comb=comb, x=x, wgt=wgt,
                          ys_comb=ys_prev, out=out, g_final=g_final, final_norm=final_norm)
        ys_prev = res.pop(0)
        if disp is not None:
            xs = res.pop(0)
        if comb is not None:
            out = res.pop(0)
    return moe_combine(x, wgt, plans[-1], ys_prev, out, g_final, final_norm)


def kernel(x, positions, norm_mix, norm_ffn, norm_final, conv_w_in, conv_w, conv_w_out, mla_w_down, mla_q_norm,
           mla_w_uq, mla_kv_norm, mla_w_ukv, mla_w_o, swa_w_qkv, swa_b_qkv, swa_sinks, swa_w_o, swa_b_o,
           ffn_w_gate_up, ffn_w_down, moe_router, moe_w_gate_up, moe_w_down):
    batch, seq, d = x.shape
    depth = norm_mix.shape[0]
    t = batch * seq
    xf = x.reshape(t, d)
    pos_col = positions.reshape(t, 1).astype(F32)
    pos_row = positions.reshape(t // WINDOW, 1, WINDOW).astype(F32)
    zeros_d = jnp.zeros((d,), F32)
    moe_wgu, moe_wd = moe_w_gate_up.astype(BF16), moe_w_down.astype(BF16)
    for i in range(depth):
        kind = i % N_MIXERS
        j = i // N_MIXERS
        proj = None
        x_in = xf
        if kind == 0:
            xf = conv_mixer(xf, norm_mix[i], conv_w_in[j], conv_w[j], conv_w_out[j], seq)
        elif kind == 1:
            q, k, v = mla_project(xf, pos_col, norm_mix[i], mla_w_down[j], mla_q_norm[j], mla_w_uq[j],
                                  mla_kv_norm[j], mla_w_ukv[j])
            proj = (mla_flash(q, k, v, batch, seq), mla_w_o[j], zeros_d)
        else:
            wq, bq = swa_qkv_weights(swa_w_qkv[j], swa_b_qkv[j])
            qkv = rms_linear(xf, norm_mix[i], wq, bq)
            proj = (swa_attention(qkv, swa_sinks[j], pos_col, pos_row, batch, seq), swa_w_o[j], swa_b_o[j])
        f_idx = i // 2
        last = i == depth - 1
        if i % 2 == 0:
            xf = dense_ffn(xf, norm_ffn[i], ffn_w_gate_up[f_idx], ffn_w_down[f_idx], proj)
            if last:
                xf = final_rmsnorm(xf, norm_final)
        else:
            xf = moe_layer(xf, norm_ffn[i], moe_router[f_idx], moe_wgu, moe_wd, f_idx, norm_final, last, proj,
                           donor=x_in if i > 0 else None)
    return xf.reshape(batch, seq, d)


def _final_norm_body(x_ref, g_ref, o_ref):
    o_ref[...] = _rms(x_ref[...], g_ref[...])


def final_rmsnorm(x, g):
    t, d = x.shape
    tm = _tile(t, 1024)
    return pl.pallas_call(
        _final_norm_body,
        grid=(t // tm,),
        in_specs=[pl.BlockSpec((tm, d), lambda i: (i, 0)), _const_spec((1, d))],
        out_specs=pl.BlockSpec((tm, d), lambda i: (i, 0)),
        out_shape=jax.ShapeDtypeStruct((t, d), F32),
        compiler_params=_cparams(("arbitrary",)),
        name="final_rmsnorm",
    )(x, g.reshape(1, d))
```

```python
import functools
from typing import NamedTuple

import jax
import jax.numpy as jnp
from jax import lax
from jax.experimental import pallas as pl
from jax.experimental.pallas import tpu as pltpu

F32 = jnp.float32
BF16 = jnp.bfloat16

RMS_EPS = 1e-6
NEG_INF = -1e30
CONV_WIDTH = 3
MLA_HEADS = 8
MLA_Q_LORA = 384
MLA_KV_LORA = 256
MLA_NOPE = 128
MLA_ROPE = 64
MLA_V = 128
MLA_QK_PAD = 256
LOG2E = 1.4426950408889634
MLA_Q_SCALE = float((MLA_NOPE + MLA_ROPE) ** -0.5 * LOG2E)
ROPE_THETA = 10000.0
SWA_Q_HEADS = 16
SWA_KV_HEADS = 2
SWA_GROUP = SWA_Q_HEADS // SWA_KV_HEADS
SWA_HEAD_DIM = 64
WINDOW = 128
SWA_Q_SCALE = float(SWA_HEAD_DIM ** -0.5 * LOG2E)
SWA_MASKED_DIST = 1e32
N_EXPERTS = 8
TOP_K = 2
N_MIXERS = 3

TOKEN_TILE = 512
EXPERT_ROW_TILE = 512
VMEM_LIMIT = 56 * 1024 * 1024


def _cparams(sem):
    return pltpu.CompilerParams(dimension_semantics=sem, vmem_limit_bytes=VMEM_LIMIT)


def _rms(xf, g):
    ms = jnp.mean(xf * xf, axis=-1, keepdims=True)
    return xf * lax.rsqrt(ms + RMS_EPS) * g


def _const_spec(shape):
    nd = len(shape)
    return pl.BlockSpec(shape, lambda *_: (0,) * nd)


def _resident_spec(shape):
    nd = len(shape)
    return pl.BlockSpec(shape, lambda *_: (0,) * nd, pipeline_mode=pl.Buffered(1))


def _tile(n, pref):
    t = min(n, pref)
    assert n % t == 0, (n, t)
    return t


def _conv_body(x_ref, g_ref, win_ref, wc_ref, wout_ref, o_ref, vpad_ref, *, tm, d, tiles_per_seq, nsub):
    i = pl.program_id(0)
    ts = tm // nsub

    @pl.when(i % tiles_per_seq == 0)
    def _():
        vpad_ref[0:8, :] = jnp.zeros((8, d), F32)

    def in_proj(r):
        h = _rms(x_ref[r * ts:(r + 1) * ts, :], g_ref[...]).astype(BF16)
        return tuple(jnp.dot(h, win_ref[:, c * d:(c + 1) * d], preferred_element_type=F32) for c in range(3))

    nxt = in_proj(0)
    for r in range(nsub):
        bg, cg, u = nxt
        if r + 1 < nsub:
            nxt = in_proj(r + 1)
        v = cg * u
        lo = r * ts
        vpad_ref[8 + lo:8 + lo + ts, :] = v
        conv = vpad_ref[6 + lo:6 + lo + ts, :] * wc_ref[0:1, :]
        conv = conv + vpad_ref[7 + lo:7 + lo + ts, :] * wc_ref[1:2, :]
        conv = conv + v * wc_ref[2:3, :]
        y = (bg * conv).astype(BF16)
        o_ref[lo:lo + ts, :] = x_ref[lo:lo + ts, :] + jnp.dot(y, wout_ref[...], preferred_element_type=F32)
    vpad_ref[0:8, :] = vpad_ref[tm:tm + 8, :]


def conv_mixer(x, g, w_in, w_conv, w_out, seq):
    t, d = x.shape
    tm = _tile(seq, TOKEN_TILE)
    body = functools.partial(_conv_body, tm=tm, d=d, tiles_per_seq=seq // tm, nsub=2 if tm % 256 == 0 else 1)
    return pl.pallas_call(
        body,
        grid=(t // tm,),
        in_specs=[
            pl.BlockSpec((tm, d), lambda i: (i, 0)),
            _const_spec((1, d)),
            _resident_spec((d, 3 * d)),
            _const_spec((CONV_WIDTH, d)),
            _resident_spec((d, d)),
        ],
        out_specs=pl.BlockSpec((tm, d), lambda i: (i, 0)),
        out_shape=jax.ShapeDtypeStruct((t, d), F32),
        scratch_shapes=[pltpu.VMEM((tm + 8, d), F32)],
        compiler_params=_cparams(("arbitrary",)),
        name="conv_mixer",
    )(x, g.reshape(1, d), w_in.astype(BF16), w_conv, w_out.astype(BF16))


def _swiglu(h, wgu_cols, wd, a_ref, dff, tf):
    def gate_up(c):
        gate = jnp.dot(h, wgu_cols(c * tf, (c + 1) * tf), preferred_element_type=F32)
        up = jnp.dot(h, wgu_cols(dff + c * tf, dff + (c + 1) * tf), preferred_element_type=F32)
        return gate, up

    nxt = gate_up(0)
    for c in range(dff // tf):
        gate, up = nxt
        if c + 1 < dff // tf:
            nxt = gate_up(c + 1)
        a_ref[:, c * tf:(c + 1) * tf] = (gate * jax.nn.sigmoid(gate) * up).astype(BF16)
    return jnp.dot(a_ref[...], wd, preferred_element_type=F32)


def _mixer_out(x_ref, proj_refs):
    if not proj_refs:
        return x_ref[...]
    a_ref, w_ref, b_ref = proj_refs
    return (x_ref[...] + b_ref[...]) + jnp.dot(a_ref[...], w_ref[...], preferred_element_type=F32)


def _proj_specs(proj, tm, d):
    if proj is None:
        return [], []
    a, w, b = proj
    k = a.shape[1]
    specs = [pl.BlockSpec((tm, k), lambda i: (i, 0)), _resident_spec((k, d)), _const_spec((1, d))]
    return [a, w.astype(BF16), b.reshape(1, d).astype(F32)], specs


def _ffn_body(*refs, dff, tf, fused_proj):
    x_ref, *proj_refs = refs[:4] if fused_proj else refs[:1]
    g_ref, wgu_ref, wd_ref, o_ref, a_ref = refs[4:] if fused_proj else refs[1:]
    x = _mixer_out(x_ref, proj_refs)
    h = _rms(x, g_ref[...]).astype(BF16)
    o_ref[...] = x + _swiglu(h, lambda lo, hi: wgu_ref[:, lo:hi], wd_ref[...], a_ref, dff, tf)


def _ff_chunk(dff, pref):
    best = None
    for c in range(128, dff + 1, 128):
        if dff % c == 0 and c <= pref:
            best = c
    return best if best is not None else dff


def dense_ffn(x, g, w_gate_up, w_down, proj=None):
    t, d = x.shape
    dff = w_down.shape[0]
    tm = _tile(t, TOKEN_TILE)
    tf = _ff_chunk(dff, 768)
    body = functools.partial(_ffn_body, dff=dff, tf=tf, fused_proj=proj is not None)
    proj_args, proj_specs = _proj_specs(proj, tm, d)
    return pl.pallas_call(
        body,
        grid=(t // tm,),
        in_specs=[pl.BlockSpec((tm, d), lambda i: (i, 0))] + proj_specs + [
            _const_spec((1, d)),
            _resident_spec((d, 2 * dff)),
            _resident_spec((dff, d)),
        ],
        out_specs=pl.BlockSpec((tm, d), lambda i: (i, 0)),
        out_shape=jax.ShapeDtypeStruct((t, d), F32),
        scratch_shapes=[pltpu.VMEM((tm, dff), BF16)],
        compiler_params=_cparams(("arbitrary",)),
        name="dense_ffn",
    )(x, *proj_args, g.reshape(1, d), w_gate_up.astype(BF16), w_down.astype(BF16))


def _rmslin_body(x_ref, g_ref, w_ref, b_ref, o_ref):
    h = _rms(x_ref[...], g_ref[...]).astype(BF16)
    o_ref[...] = (jnp.dot(h, w_ref[...], preferred_element_type=F32) + b_ref[...]).astype(o_ref.dtype)


def rms_linear(x, g, w, b):
    t, d = x.shape
    n = w.shape[1]
    tm = _tile(t, TOKEN_TILE)
    return pl.pallas_call(
        _rmslin_body,
        grid=(t // tm,),
        in_specs=[
            pl.BlockSpec((tm, d), lambda i: (i, 0)),
            _const_spec((1, d)),
            _const_spec((d, n)),
            _const_spec((1, n)),
        ],
        out_specs=pl.BlockSpec((tm, n), lambda i: (i, 0)),
        out_shape=jax.ShapeDtypeStruct((t, n), BF16),
        compiler_params=_cparams(("arbitrary",)),
        name="rms_linear",
    )(x, g.reshape(1, d), w.astype(BF16), b.reshape(1, n).astype(F32))


def _mla_proj_body(x_ref, pos_ref, g_ref, wd_ref, qn_ref, kvn_ref, wqa_ref, wqb_ref, wkv_ref, frq_ref,
                   q_ref, k_ref, v_ref):
    h = _rms(x_ref[...], g_ref[...]).astype(BF16)
    down = jnp.dot(h, wd_ref[...], preferred_element_type=F32)
    c_q = down[:, 0:MLA_Q_LORA]
    c_kv = down[:, MLA_Q_LORA:MLA_Q_LORA + MLA_KV_LORA]
    o = MLA_Q_LORA + MLA_KV_LORA
    kpe = down[:, o:o + 128]
    kpe_sw = down[:, o + 128:o + 256]
    hq = _rms(c_q, qn_ref[...]).astype(BF16)
    hkv = _rms(c_kv, kvn_ref[...]).astype(BF16)
    qa = jnp.dot(hq, wqa_ref[...], preferred_element_type=F32)
    qb = jnp.dot(hq, wqb_ref[...], preferred_element_type=F32)
    kv = jnp.dot(hkv, wkv_ref[...], preferred_element_type=F32)
    ang = pos_ref[...] * frq_ref[0:1, :]
    cos = jnp.cos(ang)
    sin = jnp.sin(ang) * frq_ref[1:2, :]
    kpe_r = (kpe * cos + kpe_sw * sin).astype(BF16)
    ones_col = jnp.where(lax.broadcasted_iota(jnp.int32, cos.shape, 1) == 0, 1.0, 0.0).astype(BF16)
    for hd in range(MLA_HEADS):
        qo = hd * MLA_QK_PAD
        q_ref[hd, :, 0:MLA_NOPE] = qa[:, qo:qo + MLA_NOPE].astype(BF16)
        qpe = qa[:, qo + MLA_NOPE:qo + MLA_QK_PAD] * cos + qb[:, hd * 128:(hd + 1) * 128] * sin
        q_ref[hd, :, MLA_NOPE:MLA_QK_PAD] = qpe.astype(BF16)
        ko = hd * (MLA_NOPE + MLA_V)
        k_ref[hd, :, 0:MLA_NOPE] = kv[:, ko:ko + MLA_NOPE].astype(BF16)
        k_ref[hd, :, MLA_NOPE:MLA_QK_PAD] = kpe_r
        v_ref[hd, :, 0:MLA_V] = kv[:, ko + MLA_NOPE:ko + MLA_NOPE + MLA_V].astype(BF16)
        v_ref[hd, :, MLA_V:2 * MLA_V] = ones_col


def _rope_swap(w):
    half = MLA_ROPE // 2
    return jnp.concatenate([w[..., half:], w[..., :half]], axis=-1)


def mla_project(x, pos_col, g, w_down, q_norm, w_uq, kv_norm, w_ukv):
    t, d = x.shape
    tm = _tile(t, TOKEN_TILE)
    hn = MLA_HEADS
    z64 = jnp.zeros((d, 64), F32)
    o = MLA_Q_LORA + MLA_KV_LORA
    kpe_w = w_down[:, o:o + MLA_ROPE]
    wd = jnp.concatenate([w_down[:, :o], kpe_w, z64, _rope_swap(kpe_w), z64], axis=1).astype(BF16)
    wq = (w_uq * MLA_Q_SCALE).reshape(MLA_Q_LORA, hn, MLA_NOPE + MLA_ROPE)
    zq = jnp.zeros((MLA_Q_LORA, hn, 64), F32)
    wqa = jnp.concatenate([wq, zq], axis=-1).reshape(MLA_Q_LORA, hn * MLA_QK_PAD).astype(BF16)
    wqb = jnp.concatenate([_rope_swap(wq[..., MLA_NOPE:]), zq], axis=-1).reshape(MLA_Q_LORA, hn * 128).astype(BF16)
    inv = ROPE_THETA ** (-jnp.arange(0, MLA_ROPE, 2, dtype=F32) / MLA_ROPE)
    half = MLA_ROPE // 2
    frq = jnp.stack([
        jnp.concatenate([inv, inv, jnp.zeros((64,), F32)]),
        jnp.concatenate([-jnp.ones((half,), F32), jnp.ones((half,), F32), jnp.zeros((64,), F32)]),
    ])
    nd = wd.shape[1]
    outs = pl.pallas_call(
        _mla_proj_body,
        grid=(t // tm,),
        in_specs=[
            pl.BlockSpec((tm, d), lambda i: (i, 0)),
            pl.BlockSpec((tm, 1), lambda i: (i, 0)),
            _const_spec((1, d)),
            _resident_spec((d, nd)),
            _const_spec((1, MLA_Q_LORA)),
            _const_spec((1, MLA_KV_LORA)),
            _resident_spec((MLA_Q_LORA, hn * MLA_QK_PAD)),
            _resident_spec((MLA_Q_LORA, hn * 128)),
            _resident_spec((MLA_KV_LORA, hn * (MLA_NOPE + MLA_V))),
            _const_spec((2, 128)),
        ],
        out_specs=[
            pl.BlockSpec((hn, tm, MLA_QK_PAD), lambda i: (0, i, 0)),
            pl.BlockSpec((hn, tm, MLA_QK_PAD), lambda i: (0, i, 0)),
            pl.BlockSpec((hn, tm, 2 * MLA_V), lambda i: (0, i, 0)),
        ],
        out_shape=[
            jax.ShapeDtypeStruct((hn, t, MLA_QK_PAD), BF16),
            jax.ShapeDtypeStruct((hn, t, MLA_QK_PAD), BF16),
            jax.ShapeDtypeStruct((hn, t, 2 * MLA_V), BF16),
        ],
        compiler_params=_cparams(("arbitrary",)),
        name="mla_project",
    )(x, pos_col, g.reshape(1, d), wd, q_norm.reshape(1, -1), kv_norm.reshape(1, -1), wqa, wqb,
      w_ukv.astype(BF16), frq)
    return outs


def _flash_body(qi_ref, kj_ref, q_ref, k_ref, v_ref, o_ref, m_ref, acc_ref, *, tq, tqs, hp, dv):
    pair = pl.program_id(2)
    i = qi_ref[pair]
    j = kj_ref[pair]

    @pl.when(j == 0)
    def _():
        m_ref[...] = jnp.full_like(m_ref, NEG_INF)
        acc_ref[...] = jnp.zeros_like(acc_ref)

    def step(diagonal):
        nsub = tq // tqs
        units = [(h, r) for h in range(hp) for r in range(nsub)]
        causal = (lax.broadcasted_iota(jnp.int32, (tqs, tqs), 1) <= lax.broadcasted_iota(jnp.int32, (tqs, tqs), 0))

        def scores(u):
            h, r = u
            nk = (r + 1) * tqs if diagonal else tq
            return lax.dot_general(q_ref[h, r * tqs:(r + 1) * tqs, :], k_ref[h, 0:nk, :], (((1,), (1,)), ((), ())),
                                   preferred_element_type=F32)

        s_next = scores(units[0])
        for n, (h, r) in enumerate(units):
            rows = slice(r * tqs, (r + 1) * tqs)
            nk = (r + 1) * tqs if diagonal else tq
            s = s_next
            if n + 1 < len(units):
                s_next = scores(units[n + 1])
            if diagonal:
                s_diag = jnp.where(causal, s[:, nk - tqs:nk], NEG_INF)
                s = s_diag if r == 0 else jnp.concatenate([s[:, 0:nk - tqs], s_diag], axis=1)
            m_prev = m_ref[h, rows, :]
            m_new = jnp.maximum(m_prev, jnp.max(s, axis=-1, keepdims=True))
            alpha = jnp.exp2(m_prev - m_new)
            p = jnp.exp2(s - m_new)
            pv = jnp.dot(p.astype(BF16), v_ref[h, 0:nk, :], preferred_element_type=F32)
            acc_ref[h, rows, :] = alpha * acc_ref[h, rows, :] + pv
            m_ref[h, rows, :] = m_new

    @pl.when(j < i)
    def _():
        step(False)

    @pl.when(j == i)
    def _():
        step(True)
        for h in range(hp):
            o_ref[:, h * dv:(h + 1) * dv] = (acc_ref[h, :, 0:dv] / acc_ref[h, :, dv:dv + 1]).astype(o_ref.dtype)


def mla_flash(q, k, v, batch, seq):
    hn, t, dq = q.shape
    dvp = v.shape[2]
    dv = MLA_V
    tq = tk = _tile(seq, 1024)
    nq = nk = seq // tq
    hp = 4
    body = functools.partial(_flash_body, tq=tq, tqs=_tile(tq, 256), hp=hp, dv=dv)
    pairs = [(i, j) for i in range(nq) for j in range(i + 1)]
    qi = jnp.asarray([p[0] for p in pairs], jnp.int32)
    kj = jnp.asarray([p[1] for p in pairs], jnp.int32)
    grid_spec = pltpu.PrefetchScalarGridSpec(
        num_scalar_prefetch=2,
        grid=(batch, hn // hp, len(pairs)),
        in_specs=[
            pl.BlockSpec((hp, tq, dq), lambda b, h, p, qi, kj: (h, b * nq + qi[p], 0)),
            pl.BlockSpec((hp, tk, dq), lambda b, h, p, qi, kj: (h, b * nk + kj[p], 0)),
            pl.BlockSpec((hp, tk, dvp), lambda b, h, p, qi, kj: (h, b * nk + kj[p], 0)),
        ],
        out_specs=pl.BlockSpec((tq, hp * dv), lambda b, h, p, qi, kj: (b * nq + qi[p], h)),
        scratch_shapes=[pltpu.VMEM((hp, tq, 1), F32), pltpu.VMEM((hp, tq, dvp), F32)],
    )
    return pl.pallas_call(
        body,
        grid_spec=grid_spec,
        out_shape=jax.ShapeDtypeStruct((t, hn * dv), BF16),
        compiler_params=_cparams(("arbitrary", "arbitrary", "arbitrary")),
        name="mla_flash",
    )(qi, kj, q, k, v)


def _alibi_slopes():
    return [2.0 ** (-8.0 * (h + 1) / SWA_Q_HEADS) for h in range(SWA_Q_HEADS)]


def _swa_body(sink_ref, q_ref, kc_ref, kp_ref, vc_ref, vp_ref, pq_ref, pkc_ref, pkp_ref, o_ref, *, steps_per_seq, nblk):
    w = WINDOW
    seq_start = pl.program_id(0) % steps_per_seq == 0
    for r in range(nblk):
        cur = slice(r * w, (r + 1) * w)
        prv = slice((r - 1) * w, r * w)
        _swa_block(
            sink_ref, q_ref.at[cur], kc_ref.at[cur], kp_ref if r == 0 else kc_ref.at[prv],
            vc_ref.at[cur], vp_ref if r == 0 else vc_ref.at[prv], pq_ref[cur, :], pkc_ref[r],
            pkp_ref[0] if r == 0 else pkc_ref[r - 1], o_ref.at[cur],
            has_prev=jnp.logical_not(seq_start) if r == 0 else True)


def _swa_block(sink_ref, q_ref, kc_ref, kp_ref, vc_ref, vp_ref, pq, pkc, pkp, o_ref, *, has_prev):
    w = WINDOW
    dist_c = jnp.abs(pq - pkc)
    dist_p = jnp.abs(pq - pkp)
    iq = lax.broadcasted_iota(jnp.int32, (w, w), 0)
    ik = lax.broadcasted_iota(jnp.int32, (w, w), 1)
    dist_c = jnp.where(ik <= iq, dist_c, SWA_MASKED_DIST)
    dist_p = jnp.where(jnp.logical_and(ik > iq, has_prev), dist_p, SWA_MASKED_DIST)
    lane = lax.broadcasted_iota(jnp.int32, (w, 2 * SWA_HEAD_DIM), 1)
    left = lane < SWA_HEAD_DIM
    slopes = _alibi_slopes()
    pairs = SWA_GROUP // 2
    for kh in range(SWA_KV_HEADS):
        kc = kc_ref[:, kh * 128:(kh + 1) * 128]
        kp = kp_ref[:, kh * 128:(kh + 1) * 128]
        vc_l = vc_ref[:, kh * 256:kh * 256 + 128]
        vc_r = vc_ref[:, kh * 256 + 128:kh * 256 + 256]
        vp_l = vp_ref[:, kh * 256:kh * 256 + 128]
        vp_r = vp_ref[:, kh * 256 + 128:kh * 256 + 256]
        zero = jnp.zeros((w, 128), BF16)
        rows = []
        for half in range(2):
            for p in range(pairs):
                qp = q_ref[:, (kh * pairs + p) * 128:(kh * pairs + p + 1) * 128]
                rows.append(jnp.where(left if half == 0 else jnp.logical_not(left), qp, zero))
        qs = jnp.concatenate(rows, axis=0)
        dn = (((1,), (1,)), ((), ()))
        s_c = lax.dot_general(qs, kc, dn, preferred_element_type=F32)
        s_p = lax.dot_general(qs, kp, dn, preferred_element_type=F32)
        pc_rows, pp_rows = [], []
        for half in range(2):
            for p in range(pairs):
                hd = kh * SWA_GROUP + 2 * p + half
                r0 = (half * pairs + p) * w
                sink = sink_ref[hd] * LOG2E
                a_c = s_c[r0:r0 + w] - (slopes[hd] * LOG2E) * dist_c
                a_p = s_p[r0:r0 + w] - (slopes[hd] * LOG2E) * dist_p
                m = jnp.maximum(jnp.max(jnp.maximum(a_c, a_p), axis=-1, keepdims=True), sink)
                e_c = jnp.exp2(a_c - m)
                e_p = jnp.exp2(a_p - m)
                den = jnp.sum(e_c + e_p, axis=-1, keepdims=True) + jnp.exp2(sink - m)
                inv = 1.0 / den
                pc_rows.append((e_c * inv).astype(BF16))
                pp_rows.append((e_p * inv).astype(BF16))
        hw = pairs * w
        pc = jnp.concatenate(pc_rows, axis=0)
        pp = jnp.concatenate(pp_rows, axis=0)
        o_pair = (jnp.dot(pc[0:hw], vc_l, preferred_element_type=F32)
                  + jnp.dot(pp[0:hw], vp_l, preferred_element_type=F32)
                  + jnp.dot(pc[hw:2 * hw], vc_r, preferred_element_type=F32)
                  + jnp.dot(pp[hw:2 * hw], vp_r, preferred_element_type=F32))
        for p in range(pairs):
            c0 = (kh * pairs + p) * 128
            o_ref[:, c0:c0 + 128] = o_pair[p * w:(p + 1) * w].astype(o_ref.dtype)


def swa_attention(qkv, sinks, pos_col, pos_row, batch, seq):
    t = qkv.shape[0]
    w = WINDOW
    nblk = 8 if (seq // w) % 8 == 0 else (4 if (seq // w) % 4 == 0 else 1)
    tq = nblk * w
    steps_per_seq = seq // tq
    nq = SWA_Q_HEADS * SWA_HEAD_DIM
    body = functools.partial(_swa_body, steps_per_seq=steps_per_seq, nblk=nblk)

    def prev(i):
        return jnp.where(i % steps_per_seq == 0, i * nblk, i * nblk - 1)

    kcol = (nq + 4 * 128) // 256
    return pl.pallas_call(
        body,
        grid=(t // tq,),
        in_specs=[
            pl.BlockSpec(memory_space=pltpu.SMEM),
            pl.BlockSpec((tq, nq), lambda i: (i, 0)),
            pl.BlockSpec((tq, 256), lambda i: (i, kcol)),
            pl.BlockSpec((w, 256), lambda i: (prev(i), kcol)),
            pl.BlockSpec((tq, 512), lambda i: (i, nq // 512)),
            pl.BlockSpec((w, 512), lambda i: (prev(i), nq // 512)),
            pl.BlockSpec((tq, 1), lambda i: (i, 0)),
            pl.BlockSpec((nblk, 1, w), lambda i: (i, 0, 0)),
            pl.BlockSpec((1, 1, w), lambda i: (prev(i), 0, 0)),
        ],
        out_specs=pl.BlockSpec((tq, nq), lambda i: (i, 0)),
        out_shape=jax.ShapeDtypeStruct((t, nq), BF16),
        compiler_params=_cparams(("arbitrary",)),
        name="swa_attention",
    )(sinks.astype(F32), qkv, qkv, qkv, qkv, qkv, pos_col, pos_row, pos_row)


def swa_qkv_weights(w_qkv, b_qkv):
    nq = SWA_Q_HEADS * SWA_HEAD_DIM
    nk = SWA_KV_HEADS * SWA_HEAD_DIM
    hd = SWA_HEAD_DIM

    def arrange(m):
        q = m[..., :nq] * SWA_Q_SCALE
        k = m[..., nq:nq + nk]
        v = m[..., nq + nk:]
        z = jnp.zeros(m.shape[:-1] + (hd,), m.dtype)
        parts = [q]
        for kh in range(SWA_KV_HEADS):
            vh = v[..., kh * hd:(kh + 1) * hd]
            parts += [vh, z, z, vh]
        for kh in range(SWA_KV_HEADS):
            kk = k[..., kh * hd:(kh + 1) * hd]
            parts += [kk, kk]
        return jnp.concatenate(parts, axis=-1)

    return arrange(w_qkv), arrange(b_qkv)


def _group_of_step(i, group_starts):
    gid = 0
    for s in group_starts[1:]:
        gid = gid + (i >= s).astype(jnp.int32)
    return gid


def _router_body(*refs, tm, fused_proj, group_starts):
    x_ref, *proj_refs = refs[:4] if fused_proj else refs[:1]
    rest = refs[4:] if fused_proj else refs[1:]
    if fused_proj:
        g_ref, r_ref, x1_ref, eid_ref, rank_ref, wgt_ref, cnt_ref, run_ref = rest
    else:
        g_ref, r_ref, eid_ref, rank_ref, wgt_ref, cnt_ref, run_ref = rest
    i = pl.program_id(0)

    is_start = functools.reduce(jnp.logical_or, [i == s for s in group_starts])

    @pl.when(is_start)
    def _():
        run_ref[...] = jnp.zeros_like(run_ref)

    x = _mixer_out(x_ref, proj_refs)
    if fused_proj:
        x1_ref[...] = x
    h = _rms(x, g_ref[...]).astype(BF16)
    logits = jnp.dot(h, r_ref[...], preferred_element_type=F32)
    lane = lax.broadcasted_iota(jnp.int32, logits.shape, 1)
    logits = jnp.where(lane < N_EXPERTS, logits, -jnp.inf)
    m1 = jnp.max(logits, axis=-1, keepdims=True)
    i1 = jnp.min(jnp.where(logits == m1, lane, 128), axis=-1, keepdims=True)
    rest = jnp.where(lane == i1, -jnp.inf, logits)
    m2 = jnp.max(rest, axis=-1, keepdims=True)
    i2 = jnp.min(jnp.where(rest == m2, lane, 128), axis=-1, keepdims=True)
    e2 = jnp.exp(m2 - m1)
    w1 = 1.0 / (1.0 + e2)
    w2 = e2 / (1.0 + e2)
    oh1 = (lane == i1)
    oh2 = (lane == i2)
    sel = jnp.where(jnp.logical_or(oh1, oh2), 1.0, 0.0).astype(BF16)
    r = lax.broadcasted_iota(jnp.int32, (tm, tm), 0)
    c = lax.broadcasted_iota(jnp.int32, (tm, tm), 1)
    tri = jnp.where(c < r, 1.0, 0.0).astype(BF16)
    before = jnp.dot(tri, sel, preferred_element_type=F32) + run_ref[...]
    rank1 = jnp.sum(jnp.where(oh1, before, 0.0), axis=-1, keepdims=True)
    rank2 = jnp.sum(jnp.where(oh2, before, 0.0), axis=-1, keepdims=True)
    run_ref[...] = run_ref[...] + jnp.sum(sel.astype(F32), axis=0, keepdims=True)
    eid_ref[...] = jnp.where(lane == 0, i1, jnp.where(lane == 1, i2, 0))[:, 0:TOP_K]
    rank_ref[...] = jnp.where(lane == 0, rank1, jnp.where(lane == 1, rank2, 0.0))[:, 0:TOP_K].astype(jnp.int32)
    wgt_ref[...] = jnp.where(lane == 0, w1, jnp.where(lane == 1, w2, 0.0))[:, 0:TOP_K]
    cnt_ref[0] = jnp.broadcast_to(run_ref[...], (8, 128)).astype(jnp.int32)


def moe_route(x, g, router, group_sizes, proj=None):
    t, d = x.shape
    groups = len(group_sizes)
    tm = _tile(min(group_sizes), TOKEN_TILE)
    assert all(n % tm == 0 for n in group_sizes) and sum(group_sizes) == t
    group_starts = tuple(sum(group_sizes[:k]) // tm for k in range(groups))
    rpad = jnp.zeros((d, 128), F32).at[:, :N_EXPERTS].set(router).astype(BF16)
    body = functools.partial(_router_body, tm=tm, fused_proj=proj is not None, group_starts=group_starts)
    proj_args, proj_specs = _proj_specs(proj, tm, d)
    x1_spec = [pl.BlockSpec((tm, d), lambda i: (i, 0))] if proj is not None else []
    x1_shape = [jax.ShapeDtypeStruct((t, d), F32)] if proj is not None else []
    outs = pl.pallas_call(
        body,
        grid=(t // tm,),
        in_specs=[pl.BlockSpec((tm, d), lambda i: (i, 0))] + proj_specs + [
            _const_spec((1, d)),
            _const_spec((d, 128)),
        ],
        out_specs=x1_spec + [
            pl.BlockSpec((tm, TOP_K), lambda i: (i, 0)),
            pl.BlockSpec((tm, TOP_K), lambda i: (i, 0)),
            pl.BlockSpec((tm, TOP_K), lambda i: (i, 0)),
            pl.BlockSpec((1, 8, 128), lambda i: (_group_of_step(i, group_starts), 0, 0)),
        ],
        out_shape=x1_shape + [
            jax.ShapeDtypeStruct((t, TOP_K), jnp.int32),
            jax.ShapeDtypeStruct((t, TOP_K), jnp.int32),
            jax.ShapeDtypeStruct((t, TOP_K), F32),
            jax.ShapeDtypeStruct((groups, 8, 128), jnp.int32),
        ],
        scratch_shapes=[pltpu.VMEM((1, 128), F32)],
        compiler_params=_cparams(("arbitrary",)),
        name="moe_route",
    )(x, *proj_args, g.reshape(1, d), rpad)
    return outs if proj is not None else [x] + list(outs)


def _zero_pad_rows(pad_start_ref, pad_len_ref, xs_ref, zrow_ref, zsem):
    zrow_ref[...] = jnp.zeros_like(zrow_ref)
    for e in range(N_EXPERTS):
        def zero_row(r):
            return pltpu.make_async_copy(zrow_ref.at[pl.ds(0, 1)], xs_ref.at[pl.ds(pad_start_ref[e] + r, 1)], zsem)

        def zissue(r, carry):
            zero_row(r).start()
            return carry

        def zwait(r, carry):
            zero_row(r).wait()
            return carry

        lax.fori_loop(0, pad_len_ref[e], zissue, 0)
        lax.fori_loop(0, pad_len_ref[e], zwait, 0)


def _for_rows(n, fn, inline):
    if inline:
        for r in range(n):
            fn(r)
    else:
        lax.fori_loop(0, n, lambda r, c: (fn(r), c)[1], 0, unroll=8)


def _scatter_rows(dest_ref, x_ref, xs_ref, sem, n, inline):
    def issue(r):
        for kk in range(TOP_K):
            pltpu.make_async_copy(x_ref.at[pl.ds(r, 1)], xs_ref.at[pl.ds(dest_ref[r * TOP_K + kk], 1)], sem).start(
                priority=0 if inline else kk % 2)

    _for_rows(n, issue, inline)


def _scatter_wait(x_ref, xs_ref, sem, n):
    for kk in range(TOP_K):
        pltpu.make_async_copy(x_ref, xs_ref.at[pl.ds(0, n)], sem).wait()


def _gather_rows(dest_ref, ys_ref, buf_ref, sem, n, inline):
    def issue(r):
        for kk in range(TOP_K):
            pltpu.make_async_copy(ys_ref.at[pl.ds(dest_ref[r * TOP_K + kk], 1)], buf_ref.at[kk, pl.ds(r, 1)], sem).start()

    _for_rows(n, issue, inline)


def _gather_wait(ys_ref, buf_ref, sem, n):
    for kk in range(TOP_K):
        pltpu.make_async_copy(ys_ref.at[pl.ds(0, n)], buf_ref.at[kk], sem).wait()


def _dispatch_body(pad_start_ref, pad_len_ref, dest_ref, x_ref, xs_ref, zrow_ref, sem, zsem, *, tm):
    @pl.when(pl.program_id(0) == 0)
    def _():
        _zero_pad_rows(pad_start_ref, pad_len_ref, xs_ref, zrow_ref, zsem)

    _scatter_rows(dest_ref, x_ref, xs_ref, sem, tm, inline=False)
    _scatter_wait(x_ref, xs_ref, sem, tm)


def moe_dispatch(x, row0, ntok, dest_flat, pad_start, pad_len, p):
    d = x.shape[1]
    tm = _tile(ntok, TOKEN_TILE)
    tile0 = row0 // tm
    body = functools.partial(_dispatch_body, tm=tm)
    grid_spec = pltpu.PrefetchScalarGridSpec(
        num_scalar_prefetch=2,
        grid=(ntok // tm,),
        in_specs=[
            pl.BlockSpec((tm * TOP_K,), lambda i, ps, pn: (i,), memory_space=pltpu.SMEM),
            pl.BlockSpec((tm, d), lambda i, ps, pn: (tile0 + i, 0)),
        ],
        out_specs=pl.BlockSpec(memory_space=pl.ANY),
        scratch_shapes=[pltpu.VMEM((8, d), F32), pltpu.SemaphoreType.DMA(()), pltpu.SemaphoreType.DMA(())],
    )
    return pl.pallas_call(
        body,
        grid_spec=grid_spec,
        out_shape=jax.ShapeDtypeStruct((p, d), F32),
        compiler_params=_cparams(("arbitrary",)),
        name="moe_dispatch",
    )(pad_start, pad_len, dest_flat, x)


def _expert_body(*refs, dff, tf, ts_d, ts_c, nside, final_norm):
    refs = list(refs)
    te_ref, nt_ref = refs[:2]
    del refs[:2]
    if ts_d:
        pad_start_ref, pad_len_ref = refs[:2]
        del refs[:2]
    xs_ref, g_ref, wgu_ref, wd_ref = refs[:4]
    del refs[:4]
    if ts_d:
        dest_d_ref, x_d_ref = refs[:2]
        del refs[:2]
    if ts_c:
        dest_c_ref, x_c_ref, w_c_ref, gf_ref, ys_c_ref, _ = refs[:6]
        del refs[:6]
    ys_ref = refs.pop(0)
    if ts_d:
        xs_d_ref = refs.pop(0)
    if ts_c:
        out_ref = refs.pop(0)
    a_ref = refs.pop(0)
    if ts_d:
        zrow_ref, sem_d, zsem = refs[:3]
        del refs[:3]
    if ts_c:
        buf_ref, sem_c = refs[:2]
    i = pl.program_id(0)

    def experts():
        h = _rms(xs_ref[...], g_ref[...]).astype(BF16)
        ys_ref[...] = _swiglu(h, lambda lo, hi: wgu_ref[0, 0, :, lo:hi], wd_ref[0, 0], a_ref, dff, tf)

    if ts_d:
        @pl.when(i == 0)
        def _():
            _zero_pad_rows(pad_start_ref, pad_len_ref, xs_d_ref, zrow_ref, zsem)

    if not (ts_d or ts_c):
        pl.when(i < nt_ref[0])(experts)
    else:
        @pl.when(i < nside)
        def _():
            if ts_d:
                _scatter_rows(dest_d_ref, x_d_ref, xs_d_ref, sem_d, ts_d, inline=True)
            if ts_c:
                _gather_rows(dest_c_ref, ys_c_ref, buf_ref, sem_c, ts_c, inline=True)
            experts()
            if ts_d:
                _scatter_wait(x_d_ref, xs_d_ref, sem_d, ts_d)
            if ts_c:
                _gather_wait(ys_c_ref, buf_ref, sem_c, ts_c)
                out_ref[...] = _combine_rows(x_c_ref, w_c_ref, buf_ref, gf_ref, final_norm)

        pl.when(jnp.logical_and(i >= nside, i < nt_ref[0]))(experts)

    @pl.when(i >= nt_ref[0])
    def _():
        ys_ref[...] = jnp.zeros_like(ys_ref)


def _combine_rows(x_ref, w_ref, buf_ref, gf_ref, final_norm):
    w = w_ref[...]
    out = x_ref[...] + (w[:, 0:1] * buf_ref[0] + w[:, 1:2] * buf_ref[1])
    return _rms(out, gf_ref[...]) if final_norm else out


def moe_experts(xs, g, plan, w_gate_up, w_down, layer, disp=None, comb=None, x=None, wgt=None, ys_comb=None,
                out=None, g_final=None, final_norm=False):
    p, d = xs.shape
    tr = plan.tr
    dff = w_down.shape[2]
    tf = _ff_chunk(dff, 512)
    nside = plan.min_tiles
    ts_d = disp.ntok // nside if disp is not None else 0
    ts_c = comb.ntok // nside if comb is not None else 0
    body = functools.partial(_expert_body, dff=dff, tf=tf, ts_d=ts_d, ts_c=ts_c, nside=nside, final_norm=final_norm)

    def row(i, nt):
        return jnp.minimum(i, nt[0] - 1)

    def imap(fn):
        return lambda i, *pf: fn(i, pf[0], pf[1])

    def side_specs(o, ts):
        tile0 = o.row0 // ts

        def tok_map(i, te, nt):
            return (tile0 + jnp.minimum(i, nside - 1), 0)

        dest_spec = pl.BlockSpec((ts * TOP_K,), imap(lambda i, te, nt: (jnp.minimum(i, nside - 1),)),
                                 memory_space=pltpu.SMEM)
        return dest_spec, (lambda width: pl.BlockSpec((ts, width), imap(tok_map)))

    in_specs = [
        pl.BlockSpec((tr, d), imap(lambda i, te, nt: (row(i, nt), 0))),
        pl.BlockSpec((1, d), imap(lambda i, te, nt: (0, 0))),
        pl.BlockSpec((1, 1, d, 2 * dff), imap(lambda i, te, nt: (layer, te[row(i, nt)], 0, 0)),
                     pipeline_mode=pl.Buffered(1)),
        pl.BlockSpec((1, 1, dff, d), imap(lambda i, te, nt: (layer, te[row(i, nt)], 0, 0)),
                     pipeline_mode=pl.Buffered(1)),
    ]
    out_specs = [pl.BlockSpec((tr, d), imap(lambda i, te, nt: (i, 0)))]
    out_shape = [jax.ShapeDtypeStruct((p, d), F32)]
    scratch = [pltpu.VMEM((tr, dff), BF16)]
    args = [plan.tile_expert, plan.num_tiles]
    operands = [xs, g.reshape(1, d), w_gate_up, w_down]
    aliases = {}
    if disp is not None:
        dest_spec, tok_spec = side_specs(disp, ts_d)
        args += [disp.pad_start, disp.pad_len]
        in_specs += [dest_spec, tok_spec(d)]
        operands += [disp.dest, x]
        out_specs.append(pl.BlockSpec(memory_space=pl.ANY))
        out_shape.append(jax.ShapeDtypeStruct((disp.p, d), F32))
        scratch += [pltpu.VMEM((8, d), F32), pltpu.SemaphoreType.DMA(()), pltpu.SemaphoreType.DMA(())]
    if comb is not None:
        dest_spec, tok_spec = side_specs(comb, ts_c)
        in_specs += [dest_spec, tok_spec(d), tok_spec(TOP_K), pl.BlockSpec((1, d), imap(lambda i, te, nt: (0, 0))),
                     pl.BlockSpec(memory_space=pl.ANY), pl.BlockSpec(memory_space=pl.ANY)]
        operands += [comb.dest, x, wgt, g_final.reshape(1, d), ys_comb, out]
        aliases = {len(args) + len(operands) - 1: len(out_specs)}
        out_specs.append(tok_spec(d))
        out_shape.append(jax.ShapeDtypeStruct(out.shape, F32))
        scratch += [pltpu.VMEM((TOP_K, ts_c, d), F32), pltpu.SemaphoreType.DMA(())]
    grid_spec = pltpu.PrefetchScalarGridSpec(
        num_scalar_prefetch=len(args),
        grid=(p // tr,),
        in_specs=in_specs,
        out_specs=out_specs,
        scratch_shapes=scratch,
    )
    outs = pl.pallas_call(
        body,
        grid_spec=grid_spec,
        out_shape=out_shape,
        input_output_aliases=aliases,
        compiler_params=_cparams(("arbitrary",)),
        name="moe_experts" + ("_d" if disp is not None else "") + ("_c" if comb is not None else ""),
    )(*args, *operands)
    return list(outs)


def _combine_body(dest_ref, dest_next_ref, x_ref, w_ref, gf_ref, ys_ref, *rest, tm, final_norm):
    o_ref, buf_ref, sems = rest[-3:]
    i = pl.program_id(0)
    slot = i % 2

    def gather(d_ref, s):
        def issue(r, carry):
            for kk in range(TOP_K):
                src = d_ref[r * TOP_K + kk]
                pltpu.make_async_copy(ys_ref.at[pl.ds(src, 1)], buf_ref.at[s, kk, pl.ds(r, 1)], sems.at[s]).start(
                    priority=kk % 2)
            return carry

        lax.fori_loop(0, tm, issue, 0, unroll=8)

    @pl.when(i == 0)
    def _():
        gather(dest_ref, 0)

    @pl.when(i + 1 < pl.num_programs(0))
    def _():
        gather(dest_next_ref, 1 - slot)

    for kk in range(TOP_K):
        pltpu.make_async_copy(ys_ref.at[pl.ds(0, tm)], buf_ref.at[slot, kk], sems.at[slot]).wait()
    o_ref[...] = _combine_rows(x_ref, w_ref, buf_ref.at[slot], gf_ref, final_norm)


def moe_combine(x, wgt, plan, ys, out, g_final, final_norm):
    t, d = x.shape
    tm = _tile(plan.ntok, TOKEN_TILE)
    body = functools.partial(_combine_body, tm=tm, final_norm=final_norm)
    nsteps = plan.ntok // tm
    tile0 = plan.row0 // tm
    donor = [out] if out is not None else []
    return pl.pallas_call(
        body,
        grid=(nsteps,),
        in_specs=[
            pl.BlockSpec((tm * TOP_K,), lambda i: (i,), memory_space=pltpu.SMEM),
            pl.BlockSpec((tm * TOP_K,), lambda i: (jnp.minimum(i + 1, nsteps - 1),), memory_space=pltpu.SMEM),
            pl.BlockSpec((tm, d), lambda i: (tile0 + i, 0)),
            pl.BlockSpec((tm, TOP_K), lambda i: (tile0 + i, 0)),
            pl.BlockSpec((1, d), lambda i: (0, 0)),
            pl.BlockSpec(memory_space=pl.ANY),
        ] + [pl.BlockSpec(memory_space=pl.ANY)] * len(donor),
        out_specs=pl.BlockSpec((tm, d), lambda i: (tile0 + i, 0)),
        out_shape=jax.ShapeDtypeStruct((t, d), F32),
        scratch_shapes=[pltpu.VMEM((2, TOP_K, tm, d), F32), pltpu.SemaphoreType.DMA((2,))],
        input_output_aliases={6: 0} if donor else {},
        compiler_params=_cparams(("arbitrary",)),
        name="moe_combine",
    )(plan.dest, plan.dest, x, wgt, g_final.reshape(1, d), ys, *donor)


class _GroupPlan(NamedTuple):
    row0: int
    ntok: int
    tr: int
    p: int
    min_tiles: int
    dest: jax.Array
    tile_expert: jax.Array
    num_tiles: jax.Array
    pad_start: jax.Array
    pad_len: jax.Array


def _plan_group(eid, rank, counts, row0, ntok, tr):
    padded = ((counts + tr - 1) // tr) * tr
    ends = jnp.cumsum(padded)
    base = ends - padded
    dest = (base[eid] + rank).reshape(-1).astype(jnp.int32)
    min_tiles = (ntok * TOP_K) // tr
    ntiles_max = min_tiles + N_EXPERTS
    p = ntiles_max * tr
    tile_start = jnp.arange(ntiles_max, dtype=jnp.int32) * tr
    tile_expert = jnp.minimum(jnp.sum(tile_start[:, None] >= ends[None, :], axis=1), N_EXPERTS - 1)
    pad_len = (padded - counts).at[N_EXPERTS - 1].add(p - ends[-1])
    return _GroupPlan(row0, ntok, tr, p, min_tiles, dest, tile_expert.astype(jnp.int32),
                      (ends[-1] // tr).astype(jnp.int32).reshape(1), (base + counts).astype(jnp.int32),
                      pad_len.astype(jnp.int32))


def moe_layer(x, g, router, w_gate_up, w_down, layer, g_final, final_norm, proj=None, donor=None):
    t, d = x.shape
    sizes = (t // 4, t // 2, t // 4) if t % (8 * TOKEN_TILE) == 0 else (t,)
    tr = min(EXPERT_ROW_TILE, sizes[0])
    x, eid, rank, wgt, cnt = moe_route(x, g, router, sizes, proj)
    plans, row0 = [], 0
    for k, n in enumerate(sizes):
        plans.append(_plan_group(eid[row0:row0 + n], rank[row0:row0 + n], cnt[k, 0, :N_EXPERTS], row0, n, tr))
        row0 += n
    out = donor if donor is not None or len(plans) == 1 else jnp.zeros_like(x)
    xs = moe_dispatch(x, plans[0].row0, plans[0].ntok, plans[0].dest, plans[0].pad_start, plans[0].pad_len,
                      plans[0].p)
    ys_prev = None
    for k, plan in enumerate(plans):
        disp = plans[k + 1] if k + 1 < len(plans) else None
        comb = plans[k - 1] if k > 0 else None
        res = moe_experts(xs, g, plan, w_gate_up, w_down, layer, disp=disp, comb=comb, x=x, wgt=wgt,
                          ys_comb=ys_prev, out=out, g_final=g_final, final_norm=final_norm)
        ys_prev = res.pop(0)
        if disp is not None:
            xs = res.pop(0)
        if comb is not None:
            out = res.pop(0)
    return moe_combine(x, wgt, plans[-1], ys_prev, out, g_final, final_norm)


def kernel(x, positions, norm_mix, norm_ffn, norm_final, conv_w_in, conv_w, conv_w_out, mla_w_down, mla_q_norm,
           mla_w_uq, mla_kv_norm, mla_w_ukv, mla_w_o, swa_w_qkv, swa_b_qkv, swa_sinks, swa_w_o, swa_b_o,
           ffn_w_gate_up, ffn_w_down, moe_router, moe_w_gate_up, moe_w_down):
    batch, seq, d = x.shape
    depth = norm_mix.shape[0]
    t = batch * seq
    xf = x.reshape(t, d)
    pos_col = positions.reshape(t, 1).astype(F32)
    pos_row = positions.reshape(t // WINDOW, 1, WINDOW).astype(F32)
    zeros_d = jnp.zeros((d,), F32)
    moe_wgu, moe_wd = moe_w_gate_up.astype(BF16), moe_w_down.astype(BF16)
    for i in range(depth):
        kind = i % N_MIXERS
        j = i // N_MIXERS
        proj = None
        x_in = xf
        if kind == 0:
            xf = conv_mixer(xf, norm_mix[i], conv_w_in[j], conv_w[j], conv_w_out[j], seq)
        elif kind == 1:
            q, k, v = mla_project(xf, pos_col, norm_mix[i], mla_w_down[j], mla_q_norm[j], mla_w_uq[j],
                                  mla_kv_norm[j], mla_w_ukv[j])
            proj = (mla_flash(q, k, v, batch, seq), mla_w_o[j], zeros_d)
        else:
            wq, bq = swa_qkv_weights(swa_w_qkv[j], swa_b_qkv[j])
            qkv = rms_linear(xf, norm_mix[i], wq, bq)
            proj = (swa_attention(qkv, swa_sinks[j], pos_col, pos_row, batch, seq), swa_w_o[j], swa_b_o[j])
        f_idx = i // 2
        last = i == depth - 1
        if i % 2 == 0:
            xf = dense_ffn(xf, norm_ffn[i], ffn_w_gate_up[f_idx], ffn_w_down[f_idx], proj)
            if last:
                xf = final_rmsnorm(xf, norm_final)
        else:
            xf = moe_layer(xf, norm_ffn[i], moe_router[f_idx], moe_wgu, moe_wd, f_idx, norm_final, last, proj,
                           donor=x_in if i > 0 else None)
    return xf.reshape(batch, seq, d)


def _final_norm_body(x_ref, g_ref, o_ref):
    o_ref[...] = _rms(x_ref[...], g_ref[...])


def final_rmsnorm(x, g):
    t, d = x.shape
    tm = _tile(t, 1024)
    return pl.pallas_call(
        _final_norm_body,
        grid=(t // tm,),
        in_specs=[pl.BlockSpec((tm, d), lambda i: (i, 0)), _const_spec((1, d))],
        out_specs=pl.BlockSpec((tm, d), lambda i: (i, 0)),
        out_shape=jax.ShapeDtypeStruct((t, d), F32),
        compiler_params=_cparams(("arbitrary",)),
        name="final_rmsnorm",
    )(x, g.reshape(1, d))
```

```python
import functools
from typing import NamedTuple

import jax
import jax.numpy as jnp
from jax import lax
from jax.experimental import pallas as pl
from jax.experimental.pallas import tpu as pltpu

F32 = jnp.float32
BF16 = jnp.bfloat16

RMS_EPS = 1e-6
NEG_INF = -1e30
CONV_WIDTH = 3
MLA_HEADS = 8
MLA_Q_LORA = 384
MLA_KV_LORA = 256
MLA_NOPE = 128
MLA_ROPE = 64
MLA_V = 128
MLA_QK_PAD = 256
LOG2E = 1.4426950408889634
MLA_Q_SCALE = float((MLA_NOPE + MLA_ROPE) ** -0.5 * LOG2E)
ROPE_THETA = 10000.0
SWA_Q_HEADS = 16
SWA_KV_HEADS = 2
SWA_GROUP = SWA_Q_HEADS // SWA_KV_HEADS
SWA_HEAD_DIM = 64
WINDOW = 128
SWA_Q_SCALE = float(SWA_HEAD_DIM ** -0.5 * LOG2E)
SWA_MASKED_DIST = 1e32
N_EXPERTS = 8
TOP_K = 2
N_MIXERS = 3

TOKEN_TILE = 512
EXPERT_ROW_TILE = 512
VMEM_LIMIT = 56 * 1024 * 1024


def _cparams(sem):
    return pltpu.CompilerParams(dimension_semantics=sem, vmem_limit_bytes=VMEM_LIMIT)


def _rms(xf, g):
    ms = jnp.mean(xf * xf, axis=-1, keepdims=True)
    return xf * lax.rsqrt(ms + RMS_EPS) * g


def _const_spec(shape):
    nd = len(shape)
    return pl.BlockSpec(shape, lambda *_: (0,) * nd)


def _resident_spec(shape):
    nd = len(shape)
    return pl.BlockSpec(shape, lambda *_: (0,) * nd, pipeline_mode=pl.Buffered(1))


def _tile(n, pref):
    t = min(n, pref)
    assert n % t == 0, (n, t)
    return t


def _conv_body(x_ref, g_ref, win_ref, wc_ref, wout_ref, o_ref, vpad_ref, *, tm, d, tiles_per_seq, nsub):
    i = pl.program_id(0)
    ts = tm // nsub

    @pl.when(i % tiles_per_seq == 0)
    def _():
        vpad_ref[0:8, :] = jnp.zeros((8, d), F32)

    def in_proj(r):
        h = _rms(x_ref[r * ts:(r + 1) * ts, :], g_ref[...]).astype(BF16)
        return tuple(jnp.dot(h, win_ref[:, c * d:(c + 1) * d], preferred_element_type=F32) for c in range(3))

    nxt = in_proj(0)
    for r in range(nsub):
        bg, cg, u = nxt
        if r + 1 < nsub:
            nxt = in_proj(r + 1)
        v = cg * u
        lo = r * ts
        vpad_ref[8 + lo:8 + lo + ts, :] = v
        conv = vpad_ref[6 + lo:6 + lo + ts, :] * wc_ref[0:1, :]
        conv = conv + vpad_ref[7 + lo:7 + lo + ts, :] * wc_ref[1:2, :]
        conv = conv + v * wc_ref[2:3, :]
        y = (bg * conv).astype(BF16)
        o_ref[lo:lo + ts, :] = x_ref[lo:lo + ts, :] + jnp.dot(y, wout_ref[...], preferred_element_type=F32)
    vpad_ref[0:8, :] = vpad_ref[tm:tm + 8, :]


def conv_mixer(x, g, w_in, w_conv, w_out, seq):
    t, d = x.shape
    tm = _tile(seq, TOKEN_TILE)
    body = functools.partial(_conv_body, tm=tm, d=d, tiles_per_seq=seq // tm, nsub=2 if tm % 256 == 0 else 1)
    return pl.pallas_call(
        body,
        grid=(t // tm,),
        in_specs=[
            pl.BlockSpec((tm, d), lambda i: (i, 0)),
            _const_spec((1, d)),
            _resident_spec((d, 3 * d)),
            _const_spec((CONV_WIDTH, d)),
            _resident_spec((d, d)),
        ],
        out_specs=pl.BlockSpec((tm, d), lambda i: (i, 0)),
        out_shape=jax.ShapeDtypeStruct((t, d), F32),
        scratch_shapes=[pltpu.VMEM((tm + 8, d), F32)],
        compiler_params=_cparams(("arbitrary",)),
        name="conv_mixer",
    )(x, g.reshape(1, d), w_in.astype(BF16), w_conv, w_out.astype(BF16))


def _swiglu(h, wgu_cols, wd, a_ref, dff, tf):
    def gate_up(c):
        gate = jnp.dot(h, wgu_cols(c * tf, (c + 1) * tf), preferred_element_type=F32)
        up = jnp.dot(h, wgu_cols(dff + c * tf, dff + (c + 1) * tf), preferred_element_type=F32)
        return gate, up

    nxt = gate_up(0)
    for c in range(dff // tf):
        gate, up = nxt
        if c + 1 < dff // tf:
            nxt = gate_up(c + 1)
        a_ref[:, c * tf:(c + 1) * tf] = (gate * jax.nn.sigmoid(gate) * up).astype(BF16)
    return jnp.dot(a_ref[...], wd, preferred_element_type=F32)


def _mixer_out(x_ref, proj_refs):
    if not proj_refs:
        return x_ref[...]
    a_ref, w_ref, b_ref = proj_refs
    return (x_ref[...] + b_ref[...]) + jnp.dot(a_ref[...], w_ref[...], preferred_element_type=F32)


def _proj_specs(proj, tm, d):
    if proj is None:
        return [], []
    a, w, b = proj
    k = a.shape[1]
    specs = [pl.BlockSpec((tm, k), lambda i: (i, 0)), _resident_spec((k, d)), _const_spec((1, d))]
    return [a, w.astype(BF16), b.reshape(1, d).astype(F32)], specs


def _ffn_body(*refs, dff, tf, fused_proj):
    x_ref, *proj_refs = refs[:4] if fused_proj else refs[:1]
    g_ref, wgu_ref, wd_ref, o_ref, a_ref = refs[4:] if fused_proj else refs[1:]
    x = _mixer_out(x_ref, proj_refs)
    h = _rms(x, g_ref[...]).astype(BF16)
    o_ref[...] = x + _swiglu(h, lambda lo, hi: wgu_ref[:, lo:hi], wd_ref[...], a_ref, dff, tf)


def _ff_chunk(dff, pref):
    best = None
    for c in range(128, dff + 1, 128):
        if dff % c == 0 and c <= pref:
            best = c
    return best if best is not None else dff


def dense_ffn(x, g, w_gate_up, w_down, proj=None):
    t, d = x.shape
    dff = w_down.shape[0]
    tm = _tile(t, TOKEN_TILE)
    tf = _ff_chunk(dff, 768)
    body = functools.partial(_ffn_body, dff=dff, tf=tf, fused_proj=proj is not None)
    proj_args, proj_specs = _proj_specs(proj, tm, d)
    return pl.pallas_call(
        body,
        grid=(t // tm,),
        in_specs=[pl.BlockSpec((tm, d), lambda i: (i, 0))] + proj_specs + [
            _const_spec((1, d)),
            _resident_spec((d, 2 * dff)),
            _resident_spec((dff, d)),
        ],
        out_specs=pl.BlockSpec((tm, d), lambda i: (i, 0)),
        out_shape=jax.ShapeDtypeStruct((t, d), F32),
        scratch_shapes=[pltpu.VMEM((tm, dff), BF16)],
        compiler_params=_cparams(("arbitrary",)),
        name="dense_ffn",
    )(x, *proj_args, g.reshape(1, d), w_gate_up.astype(BF16), w_down.astype(BF16))


def _rmslin_body(x_ref, g_ref, w_ref, b_ref, o_ref):
    h = _rms(x_ref[...], g_ref[...]).astype(BF16)
    o_ref[...] = (jnp.dot(h, w_ref[...], preferred_element_type=F32) + b_ref[...]).astype(o_ref.dtype)


def rms_linear(x, g, w, b):
    t, d = x.shape
    n = w.shape[1]
    tm = _tile(t, TOKEN_TILE)
    return pl.pallas_call(
        _rmslin_body,
        grid=(t // tm,),
        in_specs=[
            pl.BlockSpec((tm, d), lambda i: (i, 0)),
            _const_spec((1, d)),
            _const_spec((d, n)),
            _const_spec((1, n)),
        ],
        out_specs=pl.BlockSpec((tm, n), lambda i: (i, 0)),
        out_shape=jax.ShapeDtypeStruct((t, n), BF16),
        compiler_params=_cparams(("arbitrary",)),
        name="rms_linear",
    )(x, g.reshape(1, d), w.astype(BF16), b.reshape(1, n).astype(F32))


def _mla_proj_body(x_ref, pos_ref, g_ref, wd_ref, qn_ref, kvn_ref, wqa_ref, wqb_ref, wkv_ref, frq_ref,
                   q_ref, k_ref, v_ref):
    h = _rms(x_ref[...], g_ref[...]).astype(BF16)
    down = jnp.dot(h, wd_ref[...], preferred_element_type=F32)
    c_q = down[:, 0:MLA_Q_LORA]
    c_kv = down[:, MLA_Q_LORA:MLA_Q_LORA + MLA_KV_LORA]
    o = MLA_Q_LORA + MLA_KV_LORA
    kpe = down[:, o:o + 128]
    kpe_sw = down[:, o + 128:o + 256]
    hq = _rms(c_q, qn_ref[...]).astype(BF16)
    hkv = _rms(c_kv, kvn_ref[...]).astype(BF16)
    qa = jnp.dot(hq, wqa_ref[...], preferred_element_type=F32)
    qb = jnp.dot(hq, wqb_ref[...], preferred_element_type=F32)
    kv = jnp.dot(hkv, wkv_ref[...], preferred_element_type=F32)
    ang = pos_ref[...] * frq_ref[0:1, :]
    cos = jnp.cos(ang)
    sin = jnp.sin(ang) * frq_ref[1:2, :]
    kpe_r = (kpe * cos + kpe_sw * sin).astype(BF16)
    ones_col = jnp.where(lax.broadcasted_iota(jnp.int32, cos.shape, 1) == 0, 1.0, 0.0).astype(BF16)
    for hd in range(MLA_HEADS):
        qo = hd * MLA_QK_PAD
        q_ref[hd, :, 0:MLA_NOPE] = qa[:, qo:qo + MLA_NOPE].astype(BF16)
        qpe = qa[:, qo + MLA_NOPE:qo + MLA_QK_PAD] * cos + qb[:, hd * 128:(hd + 1) * 128] * sin
        q_ref[hd, :, MLA_NOPE:MLA_QK_PAD] = qpe.astype(BF16)
        ko = hd * (MLA_NOPE + MLA_V)
        k_ref[hd, :, 0:MLA_NOPE] = kv[:, ko:ko + MLA_NOPE].astype(BF16)
        k_ref[hd, :, MLA_NOPE:MLA_QK_PAD] = kpe_r
        v_ref[hd, :, 0:MLA_V] = kv[:, ko + MLA_NOPE:ko + MLA_NOPE + MLA_V].astype(BF16)
        v_ref[hd, :, MLA_V:2 * MLA_V] = ones_col


def _rope_swap(w):
    half = MLA_ROPE // 2
    return jnp.concatenate([w[..., half:], w[..., :half]], axis=-1)


def mla_project(x, pos_col, g, w_down, q_norm, w_uq, kv_norm, w_ukv):
    t, d = x.shape
    tm = _tile(t, TOKEN_TILE)
    hn = MLA_HEADS
    z64 = jnp.zeros((d, 64), F32)
    o = MLA_Q_LORA + MLA_KV_LORA
    kpe_w = w_down[:, o:o + MLA_ROPE]
    wd = jnp.concatenate([w_down[:, :o], kpe_w, z64, _rope_swap(kpe_w), z64], axis=1).astype(BF16)
    wq = (w_uq * MLA_Q_SCALE).reshape(MLA_Q_LORA, hn, MLA_NOPE + MLA_ROPE)
    zq = jnp.zeros((MLA_Q_LORA, hn, 64), F32)
    wqa = jnp.concatenate([wq, zq], axis=-1).reshape(MLA_Q_LORA, hn * MLA_QK_PAD).astype(BF16)
    wqb = jnp.concatenate([_rope_swap(wq[..., MLA_NOPE:]), zq], axis=-1).reshape(MLA_Q_LORA, hn * 128).astype(BF16)
    inv = ROPE_THETA ** (-jnp.arange(0, MLA_ROPE, 2, dtype=F32) / MLA_ROPE)
    half = MLA_ROPE // 2
    frq = jnp.stack([
        jnp.concatenate([inv, inv, jnp.zeros((64,), F32)]),
        jnp.concatenate([-jnp.ones((half,), F32), jnp.ones((half,), F32), jnp.zeros((64,), F32)]),
    ])
    nd = wd.shape[1]
    outs = pl.pallas_call(
        _mla_proj_body,
        grid=(t // tm,),
        in_specs=[
            pl.BlockSpec((tm, d), lambda i: (i, 0)),
            pl.BlockSpec((tm, 1), lambda i: (i, 0)),
            _const_spec((1, d)),
            _resident_spec((d, nd)),
            _const_spec((1, MLA_Q_LORA)),
            _const_spec((1, MLA_KV_LORA)),
            _resident_spec((MLA_Q_LORA, hn * MLA_QK_PAD)),
            _resident_spec((MLA_Q_LORA, hn * 128)),
            _resident_spec((MLA_KV_LORA, hn * (MLA_NOPE + MLA_V))),
            _const_spec((2, 128)),
        ],
        out_specs=[
            pl.BlockSpec((hn, tm, MLA_QK_PAD), lambda i: (0, i, 0)),
            pl.BlockSpec((hn, tm, MLA_QK_PAD), lambda i: (0, i, 0)),
            pl.BlockSpec((hn, tm, 2 * MLA_V), lambda i: (0, i, 0)),
        ],
        out_shape=[
            jax.ShapeDtypeStruct((hn, t, MLA_QK_PAD), BF16),
            jax.ShapeDtypeStruct((hn, t, MLA_QK_PAD), BF16),
            jax.ShapeDtypeStruct((hn, t, 2 * MLA_V), BF16),
        ],
        compiler_params=_cparams(("arbitrary",)),
        name="mla_project",
    )(x, pos_col, g.reshape(1, d), wd, q_norm.reshape(1, -1), kv_norm.reshape(1, -1), wqa, wqb,
      w_ukv.astype(BF16), frq)
    return outs


def _flash_body(qi_ref, kj_ref, q_ref, k_ref, v_ref, o_ref, m_ref, acc_ref, *, tq, tqs, hp, dv):
    pair = pl.program_id(2)
    i = qi_ref[pair]
    j = kj_ref[pair]

    @pl.when(j == 0)
    def _():
        m_ref[...] = jnp.full_like(m_ref, NEG_INF)
        acc_ref[...] = jnp.zeros_like(acc_ref)

    def step(diagonal):
        nsub = tq // tqs
        units = [(h, r) for h in range(hp) for r in range(nsub)]
        causal = (lax.broadcasted_iota(jnp.int32, (tqs, tqs), 1) <= lax.broadcasted_iota(jnp.int32, (tqs, tqs), 0))

        def scores(u):
            h, r = u
            nk = (r + 1) * tqs if diagonal else tq
            return lax.dot_general(q_ref[h, r * tqs:(r + 1) * tqs, :], k_ref[h, 0:nk, :], (((1,), (1,)), ((), ())),
                                   preferred_element_type=F32)

        s_next = scores(units[0])
        for n, (h, r) in enumerate(units):
            rows = slice(r * tqs, (r + 1) * tqs)
            nk = (r + 1) * tqs if diagonal else tq
            s = s_next
            if n + 1 < len(units):
                s_next = scores(units[n + 1])
            if diagonal:
                s_diag = jnp.where(causal, s[:, nk - tqs:nk], NEG_INF)
                s = s_diag if r == 0 else jnp.concatenate([s[:, 0:nk - tqs], s_diag], axis=1)
            m_prev = m_ref[h, rows, :]
            m_new = jnp.maximum(m_prev, jnp.max(s, axis=-1, keepdims=True))
            alpha = jnp.exp2(m_prev - m_new)
            p = jnp.exp2(s - m_new)
            pv = jnp.dot(p.astype(BF16), v_ref[h, 0:nk, :], preferred_element_type=F32)
            acc_ref[h, rows, :] = alpha * acc_ref[h, rows, :] + pv
            m_ref[h, rows, :] = m_new

    @pl.when(j < i)
    def _():
        step(False)

    @pl.when(j == i)
    def _():
        step(True)
        for h in range(hp):
            o_ref[:, h * dv:(h + 1) * dv] = (acc_ref[h, :, 0:dv] / acc_ref[h, :, dv:dv + 1]).astype(o_ref.dtype)


def mla_flash(q, k, v, batch, seq):
    hn, t, dq = q.shape
    dvp = v.shape[2]
    dv = MLA_V
    tq = tk = _tile(seq, 1024)
    nq = nk = seq // tq
    hp = 4
    body = functools.partial(_flash_body, tq=tq, tqs=_tile(tq, 256), hp=hp, dv=dv)
    pairs = [(i, j) for i in range(nq) for j in range(i + 1)]
    qi = jnp.asarray([p[0] for p in pairs], jnp.int32)
    kj = jnp.asarray([p[1] for p in pairs], jnp.int32)
    grid_spec = pltpu.PrefetchScalarGridSpec(
        num_scalar_prefetch=2,
        grid=(batch, hn // hp, len(pairs)),
        in_specs=[
            pl.BlockSpec((hp, tq, dq), lambda b, h, p, qi, kj: (h, b * nq + qi[p], 0)),
            pl.BlockSpec((hp, tk, dq), lambda b, h, p, qi, kj: (h, b * nk + kj[p], 0)),
            pl.BlockSpec((hp, tk, dvp), lambda b, h, p, qi, kj: (h, b * nk + kj[p], 0)),
        ],
        out_specs=pl.BlockSpec((tq, hp * dv), lambda b, h, p, qi, kj: (b * nq + qi[p], h)),
        scratch_shapes=[pltpu.VMEM((hp, tq, 1), F32), pltpu.VMEM((hp, tq, dvp), F32)],
    )
    return pl.pallas_call(
        body,
        grid_spec=grid_spec,
        out_shape=jax.ShapeDtypeStruct((t, hn * dv), BF16),
        compiler_params=_cparams(("arbitrary", "arbitrary", "arbitrary")),
        name="mla_flash",
    )(qi, kj, q, k, v)


def _alibi_slopes():
    return [2.0 ** (-8.0 * (h + 1) / SWA_Q_HEADS) for h in range(SWA_Q_HEADS)]


def _swa_body(sink_ref, q_ref, kc_ref, kp_ref, vc_ref, vp_ref, pq_ref, pkc_ref, pkp_ref, o_ref, *, steps_per_seq, nblk):
    w = WINDOW
    seq_start = pl.program_id(0) % steps_per_seq == 0
    for r in range(nblk):
        cur = slice(r * w, (r + 1) * w)
        prv = slice((r - 1) * w, r * w)
        _swa_block(
            sink_ref, q_ref.at[cur], kc_ref.at[cur], kp_ref if r == 0 else kc_ref.at[prv],
            vc_ref.at[cur], vp_ref if r == 0 else vc_ref.at[prv], pq_ref[cur, :], pkc_ref[r],
            pkp_ref[0] if r == 0 else pkc_ref[r - 1], o_ref.at[cur],
            has_prev=jnp.logical_not(seq_start) if r == 0 else True)


def _swa_block(sink_ref, q_ref, kc_ref, kp_ref, vc_ref, vp_ref, pq, pkc, pkp, o_ref, *, has_prev):
    w = WINDOW
    dist_c = jnp.abs(pq - pkc)
    dist_p = jnp.abs(pq - pkp)
    iq = lax.broadcasted_iota(jnp.int32, (w, w), 0)
    ik = lax.broadcasted_iota(jnp.int32, (w, w), 1)
    dist_c = jnp.where(ik <= iq, dist_c, SWA_MASKED_DIST)
    dist_p = jnp.where(jnp.logical_and(ik > iq, has_prev), dist_p, SWA_MASKED_DIST)
    lane = lax.broadcasted_iota(jnp.int32, (w, 2 * SWA_HEAD_DIM), 1)
    left = lane < SWA_HEAD_DIM
    slopes = _alibi_slopes()
    pairs = SWA_GROUP // 2
    for kh in range(SWA_KV_HEADS):
        kc = kc_ref[:, kh * 128:(kh + 1) * 128]
        kp = kp_ref[:, kh * 128:(kh + 1) * 128]
        vc_l = vc_ref[:, kh * 256:kh * 256 + 128]
        vc_r = vc_ref[:, kh * 256 + 128:kh * 256 + 256]
        vp_l = vp_ref[:, kh * 256:kh * 256 + 128]
        vp_r = vp_ref[:, kh * 256 + 128:kh * 256 + 256]
        zero = jnp.zeros((w, 128), BF16)
        rows = []
        for half in range(2):
            for p in range(pairs):
                qp = q_ref[:, (kh * pairs + p) * 128:(kh * pairs + p + 1) * 128]
                rows.append(jnp.where(left if half == 0 else jnp.logical_not(left), qp, zero))
        qs = jnp.concatenate(rows, axis=0)
        dn = (((1,), (1,)), ((), ()))
        s_c = lax.dot_general(qs, kc, dn, preferred_element_type=F32)
        s_p = lax.dot_general(qs, kp, dn, preferred_element_type=F32)
        pc_rows, pp_rows = [], []
        for half in range(2):
            for p in range(pairs):
                hd = kh * SWA_GROUP + 2 * p + half
                r0 = (half * pairs + p) * w
                sink = sink_ref[hd] * LOG2E
                a_c = s_c[r0:r0 + w] - (slopes[hd] * LOG2E) * dist_c
                a_p = s_p[r0:r0 + w] - (slopes[hd] * LOG2E) * dist_p
                m = jnp.maximum(jnp.max(jnp.maximum(a_c, a_p), axis=-1, keepdims=True), sink)
                e_c = jnp.exp2(a_c - m)
                e_p = jnp.exp2(a_p - m)
                den = jnp.sum(e_c + e_p, axis=-1, keepdims=True) + jnp.exp2(sink - m)
                inv = 1.0 / den
                pc_rows.append((e_c * inv).astype(BF16))
                pp_rows.append((e_p * inv).astype(BF16))
        hw = pairs * w
        pc = jnp.concatenate(pc_rows, axis=0)
        pp = jnp.concatenate(pp_rows, axis=0)
        o_pair = (jnp.dot(pc[0:hw], vc_l, preferred_element_type=F32)
                  + jnp.dot(pp[0:hw], vp_l, preferred_element_type=F32)
                  + jnp.dot(pc[hw:2 * hw], vc_r, preferred_element_type=F32)
                  + jnp.dot(pp[hw:2 * hw], vp_r, preferred_element_type=F32))
        for p in range(pairs):
            c0 = (kh * pairs + p) * 128
            o_ref[:, c0:c0 + 128] = o_pair[p * w:(p + 1) * w].astype(o_ref.dtype)


def swa_attention(qkv, sinks, pos_col, pos_row, batch, seq):
    t = qkv.shape[0]
    w = WINDOW
    nblk = 8 if (seq // w) % 8 == 0 else (4 if (seq // w) % 4 == 0 else 1)
    tq = nblk * w
    steps_per_seq = seq // tq
    nq = SWA_Q_HEADS * SWA_HEAD_DIM
    body = functools.partial(_swa_body, steps_per_seq=steps_per_seq, nblk=nblk)

    def prev(i):
        return jnp.where(i % steps_per_seq == 0, i * nblk, i * nblk - 1)

    kcol = (nq + 4 * 128) // 256
    return pl.pallas_call(
        body,
        grid=(t // tq,),
        in_specs=[
            pl.BlockSpec(memory_space=pltpu.SMEM),
            pl.BlockSpec((tq, nq), lambda i: (i, 0)),
            pl.BlockSpec((tq, 256), lambda i: (i, kcol)),
            pl.BlockSpec((w, 256), lambda i: (prev(i), kcol)),
            pl.BlockSpec((tq, 512), lambda i: (i, nq // 512)),
            pl.BlockSpec((w, 512), lambda i: (prev(i), nq // 512)),
            pl.BlockSpec((tq, 1), lambda i: (i, 0)),
            pl.BlockSpec((nblk, 1, w), lambda i: (i, 0, 0)),
            pl.BlockSpec((1, 1, w), lambda i: (prev(i), 0, 0)),
        ],
        out_specs=pl.BlockSpec((tq, nq), lambda i: (i, 0)),
        out_shape=jax.ShapeDtypeStruct((t, nq), BF16),
        compiler_params=_cparams(("arbitrary",)),
        name="swa_attention",
    )(sinks.astype(F32), qkv, qkv, qkv, qkv, qkv, pos_col, pos_row, pos_row)


def swa_qkv_weights(w_qkv, b_qkv):
    nq = SWA_Q_HEADS * SWA_HEAD_DIM
    nk = SWA_KV_HEADS * SWA_HEAD_DIM
    hd = SWA_HEAD_DIM

    def arrange(m):
        q = m[..., :nq] * SWA_Q_SCALE
        k = m[..., nq:nq + nk]
        v = m[..., nq + nk:]
        z = jnp.zeros(m.shape[:-1] + (hd,), m.dtype)
        parts = [q]
        for kh in range(SWA_KV_HEADS):
            vh = v[..., kh * hd:(kh + 1) * hd]
            parts += [vh, z, z, vh]
        for kh in range(SWA_KV_HEADS):
            kk = k[..., kh * hd:(kh + 1) * hd]
            parts += [kk, kk]
        return jnp.concatenate(parts, axis=-1)

    return arrange(w_qkv), arrange(b_qkv)


def _group_of_step(i, group_starts):
    gid = 0
    for s in group_starts[1:]:
        gid = gid + (i >= s).astype(jnp.int32)
    return gid


def _router_body(*refs, tm, fused_proj, group_starts):
    x_ref, *proj_refs = refs[:4] if fused_proj else refs[:1]
    rest = refs[4:] if fused_proj else refs[1:]
    if fused_proj:
        g_ref, r_ref, x1_ref, eid_ref, rank_ref, wgt_ref, cnt_ref, run_ref = rest
    else:
        g_ref, r_ref, eid_ref, rank_ref, wgt_ref, cnt_ref, run_ref = rest
    i = pl.program_id(0)

    is_start = functools.reduce(jnp.logical_or, [i == s for s in group_starts])

    @pl.when(is_start)
    def _():
        run_ref[...] = jnp.zeros_like(run_ref)

    x = _mixer_out(x_ref, proj_refs)
    if fused_proj:
        x1_ref[...] = x
    h = _rms(x, g_ref[...]).astype(BF16)
    logits = jnp.dot(h, r_ref[...], preferred_element_type=F32)
    lane = lax.broadcasted_iota(jnp.int32, logits.shape, 1)
    logits = jnp.where(lane < N_EXPERTS, logits, -jnp.inf)
    lane_f = lane.astype(F32)
    m1 = jnp.max(logits, axis=-1, keepdims=True)
    i1 = jnp.min(jnp.where(logits == m1, lane_f, 128.0), axis=-1, keepdims=True)
    rest = jnp.where(lane_f == i1, -jnp.inf, logits)
    m2 = jnp.max(rest, axis=-1, keepdims=True)
    i2 = jnp.min(jnp.where(rest == m2, lane_f, 128.0), axis=-1, keepdims=True)
    e2 = jnp.exp(m2 - m1)
    w1 = 1.0 / (1.0 + e2)
    w2 = e2 / (1.0 + e2)
    oh1 = (lane_f == i1)
    oh2 = (lane_f == i2)
    sel = jnp.where(jnp.logical_or(oh1, oh2), 1.0, 0.0).astype(BF16)
    r = lax.broadcasted_iota(jnp.int32, (tm, tm), 0)
    c = lax.broadcasted_iota(jnp.int32, (tm, tm), 1)
    tri = jnp.where(c < r, 1.0, 0.0).astype(BF16)
    before = jnp.dot(tri, sel, preferred_element_type=F32) + run_ref[...]
    rank1 = jnp.sum(jnp.where(oh1, before, 0.0), axis=-1, keepdims=True)
    rank2 = jnp.sum(jnp.where(oh2, before, 0.0), axis=-1, keepdims=True)
    run_ref[...] = run_ref[...] + jnp.sum(sel.astype(F32), axis=0, keepdims=True)
    eid_ref[...] = jnp.where(lane == 0, i1, jnp.where(lane == 1, i2, 0.0))[:, 0:TOP_K].astype(jnp.int32)
    rank_ref[...] = jnp.where(lane == 0, rank1, jnp.where(lane == 1, rank2, 0.0))[:, 0:TOP_K].astype(jnp.int32)
    wgt_ref[...] = jnp.where(lane == 0, w1, jnp.where(lane == 1, w2, 0.0))[:, 0:TOP_K]
    cnt_ref[0] = jnp.broadcast_to(run_ref[...], (8, 128)).astype(jnp.int32)


def moe_route(x, g, router, group_sizes, proj=None):
    t, d = x.shape
    groups = len(group_sizes)
    tm = _tile(min(group_sizes), TOKEN_TILE)
    assert all(n % tm == 0 for n in group_sizes) and sum(group_sizes) == t
    group_starts = tuple(sum(group_sizes[:k]) // tm for k in range(groups))
    rpad = jnp.zeros((d, 128), F32).at[:, :N_EXPERTS].set(router).astype(BF16)
    body = functools.partial(_router_body, tm=tm, fused_proj=proj is not None, group_starts=group_starts)
    proj_args, proj_specs = _proj_specs(proj, tm, d)
    x1_spec = [pl.BlockSpec((tm, d), lambda i: (i, 0))] if proj is not None else []
    x1_shape = [jax.ShapeDtypeStruct((t, d), F32)] if proj is not None else []
    outs = pl.pallas_call(
        body,
        grid=(t // tm,),
        in_specs=[pl.BlockSpec((tm, d), lambda i: (i, 0))] + proj_specs + [
            _const_spec((1, d)),
            _const_spec((d, 128)),
        ],
        out_specs=x1_spec + [
            pl.BlockSpec((tm, TOP_K), lambda i: (i, 0)),
            pl.BlockSpec((tm, TOP_K), lambda i: (i, 0)),
            pl.BlockSpec((tm, TOP_K), lambda i: (i, 0)),
            pl.BlockSpec((1, 8, 128), lambda i: (_group_of_step(i, group_starts), 0, 0)),
        ],
        out_shape=x1_shape + [
            jax.ShapeDtypeStruct((t, TOP_K), jnp.int32),
            jax.ShapeDtypeStruct((t, TOP_K), jnp.int32),
            jax.ShapeDtypeStruct((t, TOP_K), F32),
            jax.ShapeDtypeStruct((groups, 8, 128), jnp.int32),
        ],
        scratch_shapes=[pltpu.VMEM((1, 128), F32)],
        compiler_params=_cparams(("arbitrary",)),
        name="moe_route",
    )(x, *proj_args, g.reshape(1, d), rpad)
    return outs if proj is not None else [x] + list(outs)


def _zero_pad_rows(pad_start_ref, pad_len_ref, xs_ref, zrow_ref, zsem):
    zrow_ref[...] = jnp.zeros_like(zrow_ref)
    for e in range(N_EXPERTS):
        def zero_row(r):
            return pltpu.make_async_copy(zrow_ref.at[pl.ds(0, 1)], xs_ref.at[pl.ds(pad_start_ref[e] + r, 1)], zsem)

        def zissue(r, carry):
            zero_row(r).start()
            return carry

        def zwait(r, carry):
            zero_row(r).wait()
            return carry

        lax.fori_loop(0, pad_len_ref[e], zissue, 0)
        lax.fori_loop(0, pad_len_ref[e], zwait, 0)


def _for_rows(n, fn, inline):
    if inline:
        for r in range(n):
            fn(r)
    else:
        lax.fori_loop(0, n, lambda r, c: (fn(r), c)[1], 0, unroll=8)


def _scatter_rows(dest_ref, x_ref, xs_ref, sem, n, inline):
    def issue(r):
        for kk in range(TOP_K):
            pltpu.make_async_copy(x_ref.at[pl.ds(r, 1)], xs_ref.at[pl.ds(dest_ref[r * TOP_K + kk], 1)], sem).start(
                priority=0 if inline else kk % 2)

    _for_rows(n, issue, inline)


def _scatter_wait(x_ref, xs_ref, sem, n):
    for kk in range(TOP_K):
        pltpu.make_async_copy(x_ref, xs_ref.at[pl.ds(0, n)], sem).wait()


def _gather_rows(dest_ref, ys_ref, buf_ref, sem, n, inline):
    def issue(r):
        for kk in range(TOP_K):
            pltpu.make_async_copy(ys_ref.at[pl.ds(dest_ref[r * TOP_K + kk], 1)], buf_ref.at[kk, pl.ds(r, 1)], sem).start()

    _for_rows(n, issue, inline)


def _gather_wait(ys_ref, buf_ref, sem, n):
    for kk in range(TOP_K):
        pltpu.make_async_copy(ys_ref.at[pl.ds(0, n)], buf_ref.at[kk], sem).wait()


def _dispatch_body(pad_start_ref, pad_len_ref, dest_ref, x_ref, xs_ref, zrow_ref, sem, zsem, *, tm):
    @pl.when(pl.program_id(0) == 0)
    def _():
        _zero_pad_rows(pad_start_ref, pad_len_ref, xs_ref, zrow_ref, zsem)

    _scatter_rows(dest_ref, x_ref, xs_ref, sem, tm, inline=False)
    _scatter_wait(x_ref, xs_ref, sem, tm)


def moe_dispatch(x, row0, ntok, dest_flat, pad_start, pad_len, p):
    d = x.shape[1]
    tm = _tile(ntok, TOKEN_TILE)
    tile0 = row0 // tm
    body = functools.partial(_dispatch_body, tm=tm)
    grid_spec = pltpu.PrefetchScalarGridSpec(
        num_scalar_prefetch=2,
        grid=(ntok // tm,),
        in_specs=[
            pl.BlockSpec((tm * TOP_K,), lambda i, ps, pn: (i,), memory_space=pltpu.SMEM),
            pl.BlockSpec((tm, d), lambda i, ps, pn: (tile0 + i, 0)),
        ],
        out_specs=pl.BlockSpec(memory_space=pl.ANY),
        scratch_shapes=[pltpu.VMEM((8, d), F32), pltpu.SemaphoreType.DMA(()), pltpu.SemaphoreType.DMA(())],
    )
    return pl.pallas_call(
        body,
        grid_spec=grid_spec,
        out_shape=jax.ShapeDtypeStruct((p, d), F32),
        compiler_params=_cparams(("arbitrary",)),
        name="moe_dispatch",
    )(pad_start, pad_len, dest_flat, x)


def _expert_body(*refs, dff, tf, ts_d, ts_c, nside, final_norm):
    refs = list(refs)
    te_ref, nt_ref = refs[:2]
    del refs[:2]
    if ts_d:
        pad_start_ref, pad_len_ref = refs[:2]
        del refs[:2]
    xs_ref, g_ref, wgu_ref, wd_ref = refs[:4]
    del refs[:4]
    if ts_d:
        dest_d_ref, x_d_ref = refs[:2]
        del refs[:2]
    if ts_c:
        dest_c_ref, x_c_ref, w_c_ref, gf_ref, ys_c_ref, _ = refs[:6]
        del refs[:6]
    ys_ref = refs.pop(0)
    if ts_d:
        xs_d_ref = refs.pop(0)
    if ts_c:
        out_ref = refs.pop(0)
    a_ref = refs.pop(0)
    if ts_d:
        zrow_ref, sem_d, zsem = refs[:3]
        del refs[:3]
    if ts_c:
        buf_ref, sem_c = refs[:2]
    i = pl.program_id(0)

    def experts():
        h = _rms(xs_ref[...], g_ref[...]).astype(BF16)
        ys_ref[...] = _swiglu(h, lambda lo, hi: wgu_ref[0, 0, :, lo:hi], wd_ref[0, 0], a_ref, dff, tf)

    if ts_d:
        @pl.when(i == 0)
        def _():
            _zero_pad_rows(pad_start_ref, pad_len_ref, xs_d_ref, zrow_ref, zsem)

    if not (ts_d or ts_c):
        pl.when(i < nt_ref[0])(experts)
    else:
        @pl.when(i < nside)
        def _():
            if ts_d:
                _scatter_rows(dest_d_ref, x_d_ref, xs_d_ref, sem_d, ts_d, inline=True)
            if ts_c:
                _gather_rows(dest_c_ref, ys_c_ref, buf_ref, sem_c, ts_c, inline=True)
            experts()
            if ts_d:
                _scatter_wait(x_d_ref, xs_d_ref, sem_d, ts_d)
            if ts_c:
                _gather_wait(ys_c_ref, buf_ref, sem_c, ts_c)
                out_ref[...] = _combine_rows(x_c_ref, w_c_ref, buf_ref, gf_ref, final_norm)

        pl.when(jnp.logical_and(i >= nside, i < nt_ref[0]))(experts)

    @pl.when(i >= nt_ref[0])
    def _():
        ys_ref[...] = jnp.zeros_like(ys_ref)


def _combine_rows(x_ref, w_ref, buf_ref, gf_ref, final_norm):
    w = w_ref[...]
    out = x_ref[...] + (w[:, 0:1] * buf_ref[0] + w[:, 1:2] * buf_ref[1])
    return _rms(out, gf_ref[...]) if final_norm else out


def moe_experts(xs, g, plan, w_gate_up, w_down, layer, disp=None, comb=None, x=None, wgt=None, ys_comb=None,
                out=None, g_final=None, final_norm=False):
    p, d = xs.shape
    tr = plan.tr
    dff = w_down.shape[2]
    tf = _ff_chunk(dff, 512)
    nside = plan.min_tiles
    ts_d = disp.ntok // nside if disp is not None else 0
    ts_c = comb.ntok // nside if comb is not None else 0
    body = functools.partial(_expert_body, dff=dff, tf=tf, ts_d=ts_d, ts_c=ts_c, nside=nside, final_norm=final_norm)

    def row(i, nt):
        return jnp.minimum(i, nt[0] - 1)

    def imap(fn):
        return lambda i, *pf: fn(i, pf[0], pf[1])

    def side_specs(o, ts):
        tile0 = o.row0 // ts

        def tok_map(i, te, nt):
            return (tile0 + jnp.minimum(i, nside - 1), 0)

        dest_spec = pl.BlockSpec((ts * TOP_K,), imap(lambda i, te, nt: (jnp.minimum(i, nside - 1),)),
                                 memory_space=pltpu.SMEM)
        return dest_spec, (lambda width: pl.BlockSpec((ts, width), imap(tok_map)))

    in_specs = [
        pl.BlockSpec((tr, d), imap(lambda i, te, nt: (row(i, nt), 0))),
        pl.BlockSpec((1, d), imap(lambda i, te, nt: (0, 0))),
        pl.BlockSpec((1, 1, d, 2 * dff), imap(lambda i, te, nt: (layer, te[row(i, nt)], 0, 0)),
                     pipeline_mode=pl.Buffered(1)),
        pl.BlockSpec((1, 1, dff, d), imap(lambda i, te, nt: (layer, te[row(i, nt)], 0, 0)),
                     pipeline_mode=pl.Buffered(1)),
    ]
    out_specs = [pl.BlockSpec((tr, d), imap(lambda i, te, nt: (i, 0)))]
    out_shape = [jax.ShapeDtypeStruct((p, d), F32)]
    scratch = [pltpu.VMEM((tr, dff), BF16)]
    args = [plan.tile_expert, plan.num_tiles]
    operands = [xs, g.reshape(1, d), w_gate_up, w_down]
    aliases = {}
    if disp is not None:
        dest_spec, tok_spec = side_specs(disp, ts_d)
        args += [disp.pad_start, disp.pad_len]
        in_specs += [dest_spec, tok_spec(d)]
        operands += [disp.dest, x]
        out_specs.append(pl.BlockSpec(memory_space=pl.ANY))
        out_shape.append(jax.ShapeDtypeStruct((disp.p, d), F32))
        scratch += [pltpu.VMEM((8, d), F32), pltpu.SemaphoreType.DMA(()), pltpu.SemaphoreType.DMA(())]
    if comb is not None:
        dest_spec, tok_spec = side_specs(comb, ts_c)
        in_specs += [dest_spec, tok_spec(d), tok_spec(TOP_K), pl.BlockSpec((1, d), imap(lambda i, te, nt: (0, 0))),
                     pl.BlockSpec(memory_space=pl.ANY), pl.BlockSpec(memory_space=pl.ANY)]
        operands += [comb.dest, x, wgt, g_final.reshape(1, d), ys_comb, out]
        aliases = {len(args) + len(operands) - 1: len(out_specs)}
        out_specs.append(tok_spec(d))
        out_shape.append(jax.ShapeDtypeStruct(out.shape, F32))
        scratch += [pltpu.VMEM((TOP_K, ts_c, d), F32), pltpu.SemaphoreType.DMA(())]
    grid_spec = pltpu.PrefetchScalarGridSpec(
        num_scalar_prefetch=len(args),
        grid=(p // tr,),
        in_specs=in_specs,
        out_specs=out_specs,
        scratch_shapes=scratch,
    )
    outs = pl.pallas_call(
        body,
        grid_spec=grid_spec,
        out_shape=out_shape,
        input_output_aliases=aliases,
        compiler_params=_cparams(("arbitrary",)),
        name="moe_experts" + ("_d" if disp is not None else "") + ("_c" if comb is not None else ""),
    )(*args, *operands)
    return list(outs)


def _combine_body(dest_ref, dest_next_ref, x_ref, w_ref, gf_ref, ys_ref, *rest, tm, final_norm):
    o_ref, buf_ref, sems = rest[-3:]
    i = pl.program_id(0)
    slot = i % 2

    def gather(d_ref, s):
        def issue(r, carry):
            for kk in range(TOP_K):
                src = d_ref[r * TOP_K + kk]
                pltpu.make_async_copy(ys_ref.at[pl.ds(src, 1)], buf_ref.at[s, kk, pl.ds(r, 1)], sems.at[s]).start(
                    priority=kk % 2)
            return carry

        lax.fori_loop(0, tm, issue, 0, unroll=8)

    @pl.when(i == 0)
    def _():
        gather(dest_ref, 0)

    @pl.when(i + 1 < pl.num_programs(0))
    def _():
        gather(dest_next_ref, 1 - slot)

    for kk in range(TOP_K):
        pltpu.make_async_copy(ys_ref.at[pl.ds(0, tm)], buf_ref.at[slot, kk], sems.at[slot]).wait()
    o_ref[...] = _combine_rows(x_ref, w_ref, buf_ref.at[slot], gf_ref, final_norm)


def moe_combine(x, wgt, plan, ys, out, g_final, final_norm):
    t, d = x.shape
    tm = _tile(plan.ntok, TOKEN_TILE)
    body = functools.partial(_combine_body, tm=tm, final_norm=final_norm)
    nsteps = plan.ntok // tm
    tile0 = plan.row0 // tm
    donor = [out] if out is not None else []
    return pl.pallas_call(
        body,
        grid=(nsteps,),
        in_specs=[
            pl.BlockSpec((tm * TOP_K,), lambda i: (i,), memory_space=pltpu.SMEM),
            pl.BlockSpec((tm * TOP_K,), lambda i: (jnp.minimum(i + 1, nsteps - 1),), memory_space=pltpu.SMEM),
            pl.BlockSpec((tm, d), lambda i: (tile0 + i, 0)),
            pl.BlockSpec((tm, TOP_K), lambda i: (tile0 + i, 0)),
            pl.BlockSpec((1, d), lambda i: (0, 0)),
            pl.BlockSpec(memory_space=pl.ANY),
        ] + [pl.BlockSpec(memory_space=pl.ANY)] * len(donor),
        out_specs=pl.BlockSpec((tm, d), lambda i: (tile0 + i, 0)),
        out_shape=jax.ShapeDtypeStruct((t, d), F32),
        scratch_shapes=[pltpu.VMEM((2, TOP_K, tm, d), F32), pltpu.SemaphoreType.DMA((2,))],
        input_output_aliases={6: 0} if donor else {},
        compiler_params=_cparams(("arbitrary",)),
        name="moe_combine",
    )(plan.dest, plan.dest, x, wgt, g_final.reshape(1, d), ys, *donor)


class _GroupPlan(NamedTuple):
    row0: int
    ntok: int
    tr: int
    p: int
    min_tiles: int
    dest: jax.Array
    tile_expert: jax.Array
    num_tiles: jax.Array
    pad_start: jax.Array
    pad_len: jax.Array


def _plan_group(eid, rank, counts, row0, ntok, tr):
    padded = ((counts + tr - 1) // tr) * tr
    ends = jnp.cumsum(padded)
    base = ends - padded
    dest = (base[eid] + rank).reshape(-1).astype(jnp.int32)
    min_tiles = (ntok * TOP_K) // tr
    ntiles_max = min_tiles + N_EXPERTS
    p = ntiles_max * tr
    tile_start = jnp.arange(ntiles_max, dtype=jnp.int32) * tr
    tile_expert = jnp.minimum(jnp.sum(tile_start[:, None] >= ends[None, :], axis=1), N_EXPERTS - 1)
    pad_len = (padded - counts).at[N_EXPERTS - 1].add(p - ends[-1])
    return _GroupPlan(row0, ntok, tr, p, min_tiles, dest, tile_expert.astype(jnp.int32),
                      (ends[-1] // tr).astype(jnp.int32).reshape(1), (base + counts).astype(jnp.int32),
                      pad_len.astype(jnp.int32))


def moe_layer(x, g, router, w_gate_up, w_down, layer, g_final, final_norm, proj=None, donor=None):
    t, d = x.shape
    sizes = (t // 4, t // 2, t // 4) if t % (8 * TOKEN_TILE) == 0 else (t,)
    tr = min(EXPERT_ROW_TILE, sizes[0])
    x, eid, rank, wgt, cnt = moe_route(x, g, router, sizes, proj)
    plans, row0 = [], 0
    for k, n in enumerate(sizes):
        plans.append(_plan_group(eid[row0:row0 + n], rank[row0:row0 + n], cnt[k, 0, :N_EXPERTS], row0, n, tr))
        row0 += n
    out = donor if donor is not None or len(plans) == 1 else jnp.zeros_like(x)
    xs = moe_dispatch(x, plans[0].row0, plans[0].ntok, plans[0].dest, plans[0].pad_start, plans[0].pad_len,
                      plans[0].p)
    ys_prev = None
    for k, plan in enumerate(plans):
        disp = plans[k + 1] if k + 1 < len(plans) else None
        comb = plans[k - 1] if k > 0 else None
        res = moe_experts(xs, g, plan, w_gate_up, w_down, layer, disp=disp, comb=comb, x=x, wgt=wgt,
                          ys_comb=ys_prev, out=out, g_final=g_final, final_norm=final_norm)
        ys_prev = res.pop(0)
        if disp is not None:
            xs = res.pop(0)
        if comb is not None:
            out = res.pop(0)
    return moe_combine(x, wgt, plans[-1], ys_prev, out, g_final, final_norm)


def kernel(x, positions, norm_mix, norm_ffn, norm_final, conv_w_in, conv_w, conv_w_out, mla_w_down, mla_q_norm,
           mla_w_uq, mla_kv_norm, mla_w_ukv, mla_w_o, swa_w_qkv, swa_b_qkv, swa_sinks, swa_w_o, swa_b_o,
           ffn_w_gate_up, ffn_w_down, moe_router, moe_w_gate_up, moe_w_down):
    batch, seq, d = x.shape
    depth = norm_mix.shape[0]
    t = batch * seq
    xf = x.reshape(t, d)
    pos_col = positions.reshape(t, 1).astype(F32)
    pos_row = positions.reshape(t // WINDOW, 1, WINDOW).astype(F32)
    zeros_d = jnp.zeros((d,), F32)
    moe_wgu, moe_wd = moe_w_gate_up.astype(BF16), moe_w_down.astype(BF16)
    for i in range(depth):
        kind = i % N_MIXERS
        j = i // N_MIXERS
        proj = None
        x_in = xf
        if kind == 0:
            xf = conv_mixer(xf, norm_mix[i], conv_w_in[j], conv_w[j], conv_w_out[j], seq)
        elif kind == 1:
            q, k, v = mla_project(xf, pos_col, norm_mix[i], mla_w_down[j], mla_q_norm[j], mla_w_uq[j],
                                  mla_kv_norm[j], mla_w_ukv[j])
            proj = (mla_flash(q, k, v, batch, seq), mla_w_o[j], zeros_d)
        else:
            wq, bq = swa_qkv_weights(swa_w_qkv[j], swa_b_qkv[j])
            qkv = rms_linear(xf, norm_mix[i], wq, bq)
            proj = (swa_attention(qkv, swa_sinks[j], pos_col, pos_row, batch, seq), swa_w_o[j], swa_b_o[j])
        f_idx = i // 2
        last = i == depth - 1
        if i % 2 == 0:
            xf = dense_ffn(xf, norm_ffn[i], ffn_w_gate_up[f_idx], ffn_w_down[f_idx], proj)
            if last:
                xf = final_rmsnorm(xf, norm_final)
        else:
            xf = moe_layer(xf, norm_ffn[i], moe_router[f_idx], moe_wgu, moe_wd, f_idx, norm_final, last, proj,
                           donor=x_in if i > 0 else None)
    return xf.reshape(batch, seq, d)


def _final_norm_body(x_ref, g_ref, o_ref):
    o_ref[...] = _rms(x_ref[...], g_ref[...])


def final_rmsnorm(x, g):
    t, d = x.shape
    tm = _tile(t, 1024)
    return pl.pallas_call(
        _final_norm_body,
        grid=(t // tm,),
        in_specs=[pl.BlockSpec((tm, d), lambda i: (i, 0)), _const_spec((1, d))],
        out_specs=pl.BlockSpec((tm, d), lambda i: (i, 0)),
        out_shape=jax.ShapeDtypeStruct((t, d), F32),
        compiler_params=_cparams(("arbitrary",)),
        name="final_rmsnorm",
    )(x, g.reshape(1, d))
```
